```python
import math
import jax, jax.numpy as jnp
from jax import lax
import numpy as np


D_MODEL = 1024
BATCH = 32
SEQ = 2048
DEPTH = 2

CHUNK = 64
Q_BLOCK = 128
HEAD_DIM = 64
HEADS_FOX = 6
HEADS_SB = 5
HEADS_DSA = 5
IDX_HEADS = 8
IDX_DIM = 64
DSA_TOPK_MAX = 256
D_FF = 4 * D_MODEL
PLE_DIM = 256
ROPE_THETA = 10000.0
LN_EPS = 1e-5
N_BRANCH = 3
NEG = -1e30
DEEPNORM_ALPHA = (2 * DEPTH) ** 0.25
DEEPNORM_BETA = (8 * DEPTH) ** -0.25

W_FOX = HEADS_FOX * HEAD_DIM
W_SB = HEADS_SB * HEAD_DIM
W_DSA = HEADS_DSA * HEAD_DIM
IN_SPLITS = (W_FOX, W_FOX, W_FOX, HEADS_FOX,
             W_SB, W_SB, W_SB,
             W_DSA, HEAD_DIM, HEAD_DIM,
             IDX_HEADS * IDX_DIM, IDX_DIM, IDX_HEADS,
             N_BRANCH * D_MODEL)
C_IN = sum(IN_SPLITS)

kernel_name = 'hybrid_fox_stickbreak_dsa_deepnorm_block'


def split_cols(h):
    out = []
    o = 0
    for w in IN_SPLITS:
        out.append(h[..., o:o + w])
        o += w
    return out


def layer_norm(x, g, b):
    xf = x.astype(jnp.float32)
    mu = jnp.mean(xf, axis=-1, keepdims=True)
    var = jnp.mean(jnp.square(xf - mu), axis=-1, keepdims=True)
    y = (xf - mu) * lax.rsqrt(var + LN_EPS) * g.astype(jnp.float32) + b.astype(jnp.float32)
    return y.astype(x.dtype)


def rope(x, pos):
    half = x.shape[-1] // 2
    inv = ROPE_THETA ** (-jnp.arange(half, dtype=jnp.float32) / half)
    ang = pos.astype(jnp.float32)[:, None] * inv[None, :]
    cos = jnp.cos(ang)[None, :, None, :].astype(x.dtype)
    sin = jnp.sin(ang)[None, :, None, :].astype(x.dtype)
    x1, x2 = x[..., :half], x[..., half:]
    return jnp.concatenate([x1 * cos - x2 * sin, x2 * cos + x1 * sin], axis=-1)


def to_blocks(a):
    b, s = a.shape[:2]
    return jnp.moveaxis(a.reshape(b, s // Q_BLOCK, Q_BLOCK, *a.shape[2:]), 1, 0)


def from_blocks(a):
    nb, b, qb = a.shape[:3]
    return jnp.moveaxis(a, 0, 1).reshape(b, nb * qb, *a.shape[3:])


def fox_attention(q, k, v, log_f):
    b, s, h, dh = q.shape
    c = jnp.cumsum(log_f, axis=1)
    c_k = jnp.transpose(c, (0, 2, 1))[:, :, None, :]
    kpos = jnp.arange(s)
    scale = dh ** -0.5

    def block(args):
        qb, cb, bi = args
        qpos = bi * Q_BLOCK + jnp.arange(Q_BLOCK)
        logits = jnp.einsum('bqhd,bkhd->bhqk', qb, k).astype(jnp.float32) * scale
        logits = logits + jnp.transpose(cb, (0, 2, 1))[..., None] - c_k
        causal = kpos[None, :] <= qpos[:, None]
        logits = jnp.where(causal, logits, NEG)
        w = jax.nn.softmax(logits, axis=-1)
        return jnp.einsum('bhqk,bkhd->bqhd', w.astype(v.dtype), v)

    out = lax.map(block, (to_blocks(q), to_blocks(c), jnp.arange(s // Q_BLOCK)))
    return from_blocks(out).reshape(b, s, h * dh)


def stick_breaking_attention(q, k, v):
    b, s, h, dh = q.shape
    kpos = jnp.arange(s)
    scale = dh ** -0.5

    def block(args):
        qb, bi = args
        qpos = bi * Q_BLOCK + jnp.arange(Q_BLOCK)
        z = jnp.einsum('bqhd,bkhd->bhqk', qb, k).astype(jnp.float32) * scale
        strict = kpos[None, :] < qpos[:, None]
        log_not = jnp.where(strict, jax.nn.log_sigmoid(-z), 0.0)
        after = lax.cumsum(log_not, axis=3, reverse=True) - log_not
        a = jnp.where(strict, jnp.exp(jax.nn.log_sigmoid(z) + after), 0.0)
        return jnp.einsum('bhqk,bkhd->bqhd', a.astype(v.dtype), v)

    out = lax.map(block, (to_blocks(q), jnp.arange(s // Q_BLOCK)))
    return from_blocks(out).reshape(b, s, h * dh)


def dsa_attention(q, k, v, iq, ik, iw):
    b, s, h, dh = q.shape
    n_sel = min(DSA_TOPK_MAX, s // 4)
    kchunk = jnp.arange(s) // CHUNK
    bidx = jnp.arange(b)[:, None, None]
    scale = dh ** -0.5

    def block(args):
        qb, iqb, iwb, bi = args
        qchunk = (bi * Q_BLOCK + jnp.arange(Q_BLOCK)) // CHUNK
        admiss = kchunk[None, :] <= qchunk[:, None]
        idx_logits = jnp.einsum('bqhd,bkd->bqhk', iqb, ik).astype(jnp.float32)
        score = jnp.einsum('bqh,bqhk->bqk', iwb.astype(jnp.float32), jax.nn.relu(idx_logits))
        score = jnp.where(admiss[None], score, NEG)
        _, sel = lax.top_k(score, n_sel)
        sel_ok = (sel // CHUNK) <= qchunk[None, :, None]
        ks = k[bidx, sel]
        vs = v[bidx, sel]
        logits = jnp.einsum('bqhd,bqnd->bhqn', qb, ks).astype(jnp.float32) * scale
        logits = jnp.where(sel_ok[:, None], logits, NEG)
        w = jax.nn.softmax(logits, axis=-1)
        return jnp.einsum('bhqn,bqnd->bqhd', w.astype(vs.dtype), vs)

    out = lax.map(block, (to_blocks(q), to_blocks(iq), to_blocks(iw), jnp.arange(s // Q_BLOCK)))
    return from_blocks(out).reshape(b, s, h * dh)


def setup_inputs(seed: int = 0) -> dict:
    key = jax.random.key(seed)
    ks = jax.random.split(key, 20)
    n = jax.random.normal
    f32 = jnp.float32
    L = DEPTH
    bt = DEEPNORM_BETA
    return {
        'x': n(ks[0], (BATCH, SEQ, D_MODEL), f32),
        'p': n(ks[1], (DEPTH, BATCH, SEQ, PLE_DIM), f32),
        'w_in': n(ks[2], (L, D_MODEL, C_IN), f32) * D_MODEL ** -0.5,
        'b_forget': 2.0 + 0.5 * n(ks[3], (L, HEADS_FOX), f32),
        'w_up_fox': n(ks[4], (L, W_FOX, D_MODEL), f32) * W_FOX ** -0.5 * bt,
        'w_up_sb': n(ks[5], (L, W_SB, D_MODEL), f32) * W_SB ** -0.5 * bt,
        'w_up_dsa': n(ks[6], (L, W_DSA, D_MODEL), f32) * W_DSA ** -0.5 * bt,
        'w_out': n(ks[7], (L, D_MODEL, D_MODEL), f32) * D_MODEL ** -0.5 * bt,
        'ln1_g': 1.0 + 0.05 * n(ks[8], (L, D_MODEL), f32),
        'ln1_b': 0.02 * n(ks[9], (L, D_MODEL), f32),
        'w_ff_in': n(ks[10], (L, D_MODEL, D_FF), f32) * D_MODEL ** -0.5,
        'w_ff_out': n(ks[11], (L, D_FF, D_MODEL), f32) * D_FF ** -0.5 * bt,
        'w_ple': n(ks[12], (L, PLE_DIM, D_MODEL), f32) * PLE_DIM ** -0.5 * bt,
        'w_ple_gate': n(ks[13], (L, D_MODEL, D_MODEL), f32) * D_MODEL ** -0.5,
        'ln2_g': 1.0 + 0.05 * n(ks[14], (L, D_MODEL), f32),
        'ln2_b': 0.02 * n(ks[15], (L, D_MODEL), f32),
    }


def reference(x, p, w_in, b_forget, w_up_fox, w_up_sb, w_up_dsa, w_out, ln1_g, ln1_b,
              w_ff_in, w_ff_out, w_ple, w_ple_gate, ln2_g, ln2_b):
    b, s, _ = x.shape
    pos = jnp.arange(s)
    for i in range(DEPTH):
        h = x @ w_in[i]
        (fq, fk, fv, ff_logit, sq, sk, sv, dq, dk, dv, iq, ik, iw, g) = split_cols(h)
        fq = fq.reshape(b, s, HEADS_FOX, HEAD_DIM)
        fk = fk.reshape(b, s, HEADS_FOX, HEAD_DIM)
        fv = fv.reshape(b, s, HEADS_FOX, HEAD_DIM)
        log_f = jax.nn.log_sigmoid(ff_logit.astype(jnp.float32) + b_forget[i].astype(jnp.float32))
        o_fox = fox_attention(fq, fk, fv, log_f)

        o_sb = stick_breaking_attention(sq.reshape(b, s, HEADS_SB, HEAD_DIM),
                                        sk.reshape(b, s, HEADS_SB, HEAD_DIM),
                                        sv.reshape(b, s, HEADS_SB, HEAD_DIM))

        dq = rope(dq.reshape(b, s, HEADS_DSA, HEAD_DIM), pos)
        dk = rope(dk[:, :, None, :], pos)[:, :, 0, :]
        iq = rope(iq.reshape(b, s, IDX_HEADS, IDX_DIM), pos)
        ik = rope(ik[:, :, None, :], pos)[:, :, 0, :]
        o_dsa = dsa_attention(dq, dk, dv, iq, ik, iw)

        gates = jax.nn.sigmoid(g.reshape(b, s, N_BRANCH, D_MODEL))
        merged = (gates[:, :, 0] * (o_fox @ w_up_fox[i])
                  + gates[:, :, 1] * (o_sb @ w_up_sb[i])
                  + gates[:, :, 2] * (o_dsa @ w_up_dsa[i]))
        x = layer_norm(DEEPNORM_ALPHA * x + merged @ w_out[i], ln1_g[i], ln1_b[i])

        ffn = jnp.square(jax.nn.relu(x @ w_ff_in[i])) @ w_ff_out[i]
        ple = jax.nn.sigmoid(x @ w_ple_gate[i]) * (p[i] @ w_ple[i])
        x = layer_norm(DEEPNORM_ALPHA * x + ffn + ple, ln2_g[i], ln2_b[i])
    return x
```

```python
import functools

import jax
import jax.numpy as jnp
from jax import lax
from jax.experimental import pallas as pl
from jax.experimental.pallas import tpu as pltpu

D_MODEL = 1024
HEAD_DIM = 64
HALF = HEAD_DIM // 2
HEADS_FOX = 6
HEADS_SB = 5
HEADS_DSA = 5
IDX_HEADS = 8
CHUNK = 64
DSA_TOPK_MAX = 256
D_FF = 4 * D_MODEL
PLE_DIM = 256
ROPE_THETA = 10000.0
LN_EPS = 1e-5
N_BRANCH = 3
NEG = -1e30
DEPTH = 2
DEEPNORM_ALPHA = (2 * DEPTH) ** 0.25
W_FOX = HEADS_FOX * HEAD_DIM
W_SB = HEADS_SB * HEAD_DIM
W_DSA = HEADS_DSA * HEAD_DIM
W_IDX = IDX_HEADS * HEAD_DIM
SCALE = HEAD_DIM ** -0.5

LANE = 128
W_SB_PAD = 384
IW_LANE = 8
VMEM_LIMIT = 56 * 1024 * 1024

TM_PROJ = 512
TKC = 256
TQ_ATT = 256
TQ_DSA = 128
TK_DSA = 256
TM_MLP = 512

BF16 = jnp.bfloat16
F32 = jnp.float32


def _log_sigmoid(x):
    return jnp.minimum(x, 0.0) - jnp.log1p(jnp.exp(-jnp.abs(x)))


def _dot(a, b):
    return jnp.dot(a, b, preferred_element_type=F32)


def _dot_nt(a, b):
    return lax.dot_general(a, b, (((1,), (1,)), ((), ())), preferred_element_type=F32)


def _split3(v):
    hi = v.astype(BF16)
    r = v - hi.astype(F32)
    mid = r.astype(BF16)
    lo = (r - mid.astype(F32)).astype(BF16)
    return hi, mid, lo


OFF_FQ, OFF_FK, OFF_FV = 0, 384, 768
OFF_SQ, OFF_SK, OFF_SV = 1152, 1536, 1920
OFF_R1 = 2304
OFF_R2 = 2688
OFF_R3 = 3200
OFF_SM = 3328
C_PACK = 3456


def _swap_halves(h):
    lane = lax.broadcasted_iota(jnp.int32, h.shape, 1)
    first = (lane % HEAD_DIM) < HALF
    return jnp.where(first, pltpu.roll(h, LANE - HALF, 1), pltpu.roll(h, HALF, 1))


def _rope(h, cos, sin):
    parts = []
    for c in range(h.shape[1] // LANE):
        hc = h[:, c * LANE:(c + 1) * LANE]
        parts.append(hc * cos + _swap_halves(hc) * sin)
    return parts


def _proj_kernel(x_ref, w_ref, bias_ref, tab_ref,
                 fq_ref, fk_ref, fv_ref, sq_ref, sk_ref, sv_ref,
                 dq_ref, dkk_ref, iq_ref, ikv_ref, cw_ref, ct_ref, carry_ref):
    tm = x_ref.shape[1]
    xb = x_ref[0].astype(BF16)

    def seg(off, width):
        return _dot(xb, w_ref[:, off:off + width])

    fq_ref[0] = (seg(OFF_FQ, W_FOX) * SCALE).astype(BF16)
    fk_ref[0] = seg(OFF_FK, W_FOX).astype(BF16)
    fv_ref[0] = seg(OFF_FV, W_FOX).astype(BF16)
    sq_ref[0] = (seg(OFF_SQ, W_SB_PAD)[:, :W_SB] * SCALE).astype(BF16)
    sk_ref[0] = seg(OFF_SK, W_SB_PAD)[:, :W_SB].astype(BF16)
    sv_ref[0] = seg(OFF_SV, W_SB_PAD)[:, :W_SB].astype(BF16)

    cos = tab_ref[:, 0:LANE]
    sin = tab_ref[:, LANE:2 * LANE]
    cos_k = tab_ref[:, 2 * LANE:3 * LANE]
    sin_k = tab_ref[:, 3 * LANE:4 * LANE]

    r1 = _rope(seg(OFF_R1, 384), cos, sin)
    lane = lax.broadcasted_iota(jnp.int32, (tm, LANE), 1)
    r1_last_q = jnp.where(lane < HEAD_DIM, r1[2] * SCALE, r1[2])
    dq_ref[0] = jnp.concatenate([r1[0] * SCALE, r1[1] * SCALE, r1_last_q], axis=1).astype(BF16)
    dkk_ref[0] = r1[2].astype(BF16)
    iq_ref[0] = jnp.concatenate(_rope(seg(OFF_R2, W_IDX), cos, sin), axis=1).astype(BF16)
    ikv_ref[0] = _rope(seg(OFF_R3, LANE), cos_k, sin_k)[0].astype(BF16)

    small = seg(OFF_SM, LANE)
    logf = _log_sigmoid(small + bias_ref[...])

    @pl.when(pl.program_id(1) == 0)
    def _():
        carry_ref[...] = jnp.zeros_like(carry_ref)

    row = lax.broadcasted_iota(jnp.int32, (tm, tm), 0)
    col = lax.broadcasted_iota(jnp.int32, (tm, tm), 1)
    tri = jnp.where(col <= row, 1.0, 0.0).astype(BF16)
    hi, mid, lo = _split3(logf)
    c = _dot(tri, hi) + _dot(tri, mid) + _dot(tri, lo) + carry_ref[...]
    carry_ref[...] = c[tm - 1:tm, :]
    cw_ref[0] = jnp.where(lane < IW_LANE, c, small)
    ct = c.T
    for r in range(tm // TKC):
        ct_ref[0, r] = ct[0:8, r * TKC:(r + 1) * TKC]


def _proj(x, w_pack, bias_row, rope_tab):
    b, s, d = x.shape
    tm = min(TM_PROJ, s)
    grid = (b, s // tm)

    def rows(width, dtype):
        return (jax.ShapeDtypeStruct((b, s, width), dtype),
                pl.BlockSpec((1, tm, width), lambda bi, si: (bi, si, 0)))

    outs = [rows(W_FOX, BF16), rows(W_FOX, BF16), rows(W_FOX, BF16),
            rows(W_SB, BF16), rows(W_SB, BF16), rows(W_SB, BF16),
            rows(384, BF16), rows(LANE, BF16), rows(W_IDX, BF16), rows(LANE, BF16),
            rows(LANE, F32),
            (jax.ShapeDtypeStruct((b, s // TKC, 8, TKC), F32),
             pl.BlockSpec((1, tm // TKC, 8, TKC), lambda bi, si: (bi, si, 0, 0)))]
    return pl.pallas_call(
        _proj_kernel,
        grid=grid,
        in_specs=[pl.BlockSpec((1, tm, d), lambda bi, si: (bi, si, 0)),
                  pl.BlockSpec((d, C_PACK), lambda bi, si: (0, 0)),
                  pl.BlockSpec((1, LANE), lambda bi, si: (0, 0)),
                  pl.BlockSpec((tm, 4 * LANE), lambda bi, si: (si, 0))],
        out_specs=[o[1] for o in outs],
        out_shape=[o[0] for o in outs],
        scratch_shapes=[pltpu.VMEM((1, LANE), F32)],
        compiler_params=pltpu.CompilerParams(
            dimension_semantics=("arbitrary", "arbitrary"), vmem_limit_bytes=VMEM_LIMIT),
        name="proj",
    )(x, w_pack, bias_row, rope_tab)


def _fox_kernel(q_ref, k_ref, v_ref, cq_ref, ct_ref, o_ref):
    tq = q_ref.shape[1]
    i = pl.program_id(1)
    row = lax.broadcasted_iota(jnp.int32, (tq, TKC), 0)
    col = lax.broadcasted_iota(jnp.int32, (tq, TKC), 1)
    causal = col <= row
    cq_all = cq_ref[0]
    outs = []
    for h in range(HEADS_FOX):
        hs = slice(h * HEAD_DIM, (h + 1) * HEAD_DIM)
        qh = q_ref[0, :, hs]
        cqh = cq_all[:, h:h + 1]

        def block(j, carry, masked):
            m, l, acc = carry
            ks = pl.multiple_of(j * TKC, TKC)
            kh = k_ref[0, pl.ds(ks, TKC), hs]
            vh = v_ref[0, pl.ds(ks, TKC), hs]
            ckh = ct_ref[0, j][h:h + 1, :]
            s = _dot_nt(qh, kh) + (cqh - ckh)
            if masked:
                s = jnp.where(causal, s, NEG)
            m_new = jnp.maximum(m, jnp.max(s, axis=1, keepdims=True))
            alpha = jnp.exp(m - m_new)
            p = jnp.exp(s - m_new)
            l = alpha * l + jnp.sum(p, axis=1, keepdims=True)
            acc = alpha * acc + _dot(p.astype(BF16), vh)
            return m_new, l, acc

        init = (jnp.full((tq, 1), NEG, F32), jnp.zeros((tq, 1), F32),
                jnp.zeros((tq, HEAD_DIM), F32))
        carry = lax.fori_loop(0, i, functools.partial(block, masked=False), init)
        m, l, acc = block(i, carry, True)
        outs.append(acc / l)
    o_ref[0] = jnp.concatenate(outs, axis=1).astype(o_ref.dtype)


def _fox(fq, fk, fv, cw, ct):
    b, s, w = fq.shape
    tq = TKC
    return pl.pallas_call(
        _fox_kernel,
        grid=(b, s // tq),
        in_specs=[pl.BlockSpec((1, tq, w), lambda bi, qi: (bi, qi, 0)),
                  pl.BlockSpec((1, s, w), lambda bi, qi: (bi, 0, 0)),
                  pl.BlockSpec((1, s, w), lambda bi, qi: (bi, 0, 0)),
                  pl.BlockSpec((1, tq, LANE), lambda bi, qi: (bi, qi, 0)),
                  pl.BlockSpec((1, s // TKC, 8, TKC), lambda bi, qi: (bi, 0, 0, 0))],
        out_specs=pl.BlockSpec((1, tq, w), lambda bi, qi: (bi, qi, 0)),
        out_shape=jax.ShapeDtypeStruct((b, s, w), BF16),
        compiler_params=pltpu.CompilerParams(
            dimension_semantics=("arbitrary", "arbitrary"), vmem_limit_bytes=VMEM_LIMIT),
        name="fox",
    )(fq, fk, fv, cw, ct)


def _sb_kernel(q_ref, k_ref, v_ref, o_ref):
    tq = q_ref.shape[1]
    i = pl.program_id(1)
    row = lax.broadcasted_iota(jnp.int32, (tq, TKC), 0)
    col = lax.broadcasted_iota(jnp.int32, (tq, TKC), 1)
    strict = col < row
    krow = lax.broadcasted_iota(jnp.int32, (2 * TKC, TKC), 0) % TKC
    kcol = lax.broadcasted_iota(jnp.int32, (2 * TKC, TKC), 1)
    upper2 = jnp.where(krow > kcol, 1.0, 0.0).astype(BF16)
    outs = []
    for h in range(HEADS_SB):
        hs = slice(h * HEAD_DIM, (h + 1) * HEAD_DIM)
        qh = q_ref[0, :, hs]

        def block(j, carry, masked):
            run, acc = carry
            ks = pl.multiple_of(j * TKC, TKC)
            kh = k_ref[0, pl.ds(ks, TKC), hs]
            vh = v_ref[0, pl.ds(ks, TKC), hs]
            z = _dot_nt(qh, kh)
            ln = _log_sigmoid(-z)
            if masked:
                ln = jnp.where(strict, ln, 0.0)
            ln_hi = ln.astype(BF16)
            ln_lo = (ln - ln_hi.astype(F32)).astype(BF16)
            later = _dot(jnp.concatenate([ln_hi, ln_lo], axis=1), upper2)
            a = jnp.exp(z + ln + later + run)
            if masked:
                a = jnp.where(strict, a, 0.0)
            acc = acc + _dot(a.astype(BF16), vh)
            run = run + later[:, 0:1] + ln[:, 0:1]
            return run, acc

        init = (jnp.zeros((tq, 1), F32), jnp.zeros((tq, HEAD_DIM), F32))
        carry = block(i, init, True)
        _, acc = lax.fori_loop(0, i, lambda t, c: block(i - 1 - t, c, False), carry)
        outs.append(acc)
    o_ref[0] = jnp.concatenate(outs, axis=1).astype(o_ref.dtype)


def _sb(sq, sk, sv):
    b, s, w = sq.shape
    tq = TKC
    return pl.pallas_call(
        _sb_kernel,
        grid=(b, s // tq),
        in_specs=[pl.BlockSpec((1, tq, w), lambda bi, qi: (bi, qi, 0)),
                  pl.BlockSpec((1, s, w), lambda bi, qi: (bi, 0, 0)),
                  pl.BlockSpec((1, s, w), lambda bi, qi: (bi, 0, 0))],
        out_specs=pl.BlockSpec((1, tq, w), lambda bi, qi: (bi, qi, 0)),
        out_shape=jax.ShapeDtypeStruct((b, s, w), BF16),
        compiler_params=pltpu.CompilerParams(
            dimension_semantics=("arbitrary", "arbitrary"), vmem_limit_bytes=VMEM_LIMIT),
        name="sb",
    )(sq, sk, sv)


INT_MIN = -2 ** 31


def _sortable(score):
    bits = lax.bitcast_convert_type(score, jnp.int32)
    return jnp.where(bits < 0, bits ^ jnp.int32(0x7FFFFFFF), bits)


def _dsa_kernel(dq_ref, iq_ref, cw_ref, dkk_ref, ikv_ref, o_ref, key_ref, bias_ref, *, n_sel, idx_bits):
    tq, tk = TQ_DSA, TK_DSA
    i = pl.program_id(1)
    nkb = (i * tq) // tk + 1
    qpos = i * tq + lax.broadcasted_iota(jnp.int32, (tq, tk), 0)
    kloc = lax.broadcasted_iota(jnp.int32, (tq, tk), 1)
    klimit = (qpos // CHUNK + 1) * CHUNK
    iq = iq_ref[0]
    iw = cw_ref[0][:, IW_LANE:IW_LANE + IDX_HEADS]

    def score_block(j, _):
        ks = pl.multiple_of(j * tk, tk)
        ikj = ikv_ref[0, pl.ds(ks, tk), 0:HEAD_DIM]
        sc = jnp.zeros((tq, tk), F32)
        for h in range(IDX_HEADS):
            a = _dot_nt(iq[:, h * HEAD_DIM:(h + 1) * HEAD_DIM], ikj)
            sc = sc + iw[:, h:h + 1] * jnp.maximum(a, 0.0)
        sc = sc + 0.0
        sc = jnp.where(kloc + ks < klimit, sc, NEG)
        key_ref[j] = _sortable(sc)
        return 0

    lax.fori_loop(0, nkb, score_block, 0)

    def count(pred):
        def body(j, acc):
            hit = jnp.where(pred(key_ref[j], kloc + j * tk), 1.0, 0.0)
            for c in range(tk // LANE):
                acc = acc + hit[:, c * LANE:(c + 1) * LANE]
            return acc
        acc = lax.fori_loop(0, nkb, body, jnp.zeros((tq, LANE), F32))
        return jnp.sum(acc, axis=1, keepdims=True)

    def thr_bit(t, thr_u):
        bit = lax.shift_left(jnp.int32(1), jnp.int32(31) - t)
        cand = (thr_u | bit) ^ jnp.int32(INT_MIN)
        cnt = count(lambda k, _: k >= cand)
        return jnp.where(cnt >= n_sel, thr_u | bit, thr_u)

    thr_u = lax.fori_loop(0, 32, thr_bit, jnp.zeros((tq, 1), jnp.int32))
    thr = thr_u ^ jnp.int32(INT_MIN)

    need = n_sel - count(lambda k, _: k > thr)

    def idx_bit(t, jmax):
        bit = lax.shift_left(jnp.int32(1), jnp.int32(idx_bits - 1) - t)
        cand = jmax | bit
        cnt = count(lambda k, pos: (k == thr) & (pos < cand))
        return jnp.where(cnt < need, cand, jmax)

    jmax = lax.fori_loop(0, idx_bits, idx_bit, jnp.zeros((tq, 1), jnp.int32))

    def bias_block(j, _):
        k = key_ref[j]
        pos = kloc + j * tk
        sel = ((k > thr) | ((k == thr) & (pos <= jmax))) & (pos < klimit)
        bias_ref[j] = jnp.where(sel, 0.0, NEG)
        return 0

    lax.fori_loop(0, nkb, bias_block, 0)

    q5 = jnp.concatenate([dq_ref[0, :, h * HEAD_DIM:(h + 1) * HEAD_DIM] for h in range(HEADS_DSA)], axis=0)
    rows5 = HEADS_DSA * tq

    def att_block(j, carry):
        m, l, acc = carry
        ks = pl.multiple_of(j * tk, tk)
        kj = dkk_ref[0, pl.ds(ks, tk), HEAD_DIM:2 * HEAD_DIM]
        vj = ikv_ref[0, pl.ds(ks, tk), HEAD_DIM:2 * HEAD_DIM]
        bias = bias_ref[j]
        s = _dot_nt(q5, kj) + jnp.concatenate([bias] * HEADS_DSA, axis=0)
        m_new = jnp.maximum(m, jnp.max(s, axis=1, keepdims=True))
        alpha = jnp.exp(m - m_new)
        p = jnp.exp(s - m_new)
        l = alpha * l + jnp.sum(p, axis=1, keepdims=True)
        acc = alpha * acc + _dot(p.astype(BF16), vj)
        return m_new, l, acc

    init = (jnp.full((rows5, 1), NEG, F32), jnp.zeros((rows5, 1), F32),
            jnp.zeros((rows5, HEAD_DIM), F32))
    m, l, acc = lax.fori_loop(0, nkb, att_block, init)
    out = acc / l
    o_ref[0] = jnp.concatenate([out[h * tq:(h + 1) * tq] for h in range(HEADS_DSA)],
                               axis=1).astype(o_ref.dtype)


def _dsa(dq, iq, cw, dkk, ikv):
    b, s, _ = dq.shape
    tq, tk = TQ_DSA, TK_DSA
    n_sel = min(DSA_TOPK_MAX, s // 4)
    assert n_sel <= tk and s % tk == 0
    idx_bits = max(1, (s - 1).bit_length())
    return pl.pallas_call(
        functools.partial(_dsa_kernel, n_sel=n_sel, idx_bits=idx_bits),
        grid=(b, s // tq),
        in_specs=[pl.BlockSpec((1, tq, 384), lambda bi, qi: (bi, qi, 0)),
                  pl.BlockSpec((1, tq, W_IDX), lambda bi, qi: (bi, qi, 0)),
                  pl.BlockSpec((1, tq, LANE), lambda bi, qi: (bi, qi, 0)),
                  pl.BlockSpec((1, s, LANE), lambda bi, qi: (bi, 0, 0)),
                  pl.BlockSpec((1, s, LANE), lambda bi, qi: (bi, 0, 0))],
        out_specs=pl.BlockSpec((1, tq, W_DSA), lambda bi, qi: (bi, qi, 0)),
        out_shape=jax.ShapeDtypeStruct((b, s, W_DSA), BF16),
        scratch_shapes=[pltpu.VMEM((s // tk, tq, tk), jnp.int32),
                        pltpu.VMEM((s // tk, tq, tk), F32)],
        compiler_params=pltpu.CompilerParams(
            dimension_semantics=("arbitrary", "arbitrary"), vmem_limit_bytes=VMEM_LIMIT),
        name="dsa",
    )(dq, iq, cw, dkk, ikv)


def _layer_norm(y, g, b):
    mu = jnp.mean(y, axis=1, keepdims=True)
    d = y - mu
    var = jnp.mean(d * d, axis=1, keepdims=True)
    return d * lax.rsqrt(var + LN_EPS) * g + b


def _merge_kernel(x_ref, of_ref, os_ref, od_ref, wg_ref, wuf_ref, wus_ref, wud_ref, wo_ref,
                  g_ref, b_ref, o_ref):
    x = x_ref[...]
    xb = x.astype(BF16)
    merged = None
    for n, (br_ref, wu_ref) in enumerate(((of_ref, wuf_ref), (os_ref, wus_ref), (od_ref, wud_ref))):
        gate = jax.nn.sigmoid(_dot(xb, wg_ref[:, n * D_MODEL:(n + 1) * D_MODEL]))
        term = gate * _dot(br_ref[...], wu_ref[...])
        merged = term if merged is None else merged + term
    y = DEEPNORM_ALPHA * x + _dot(merged.astype(BF16), wo_ref[...])
    o_ref[...] = _layer_norm(y, g_ref[...], b_ref[...])


def _const_spec(shape):
    return pl.BlockSpec(shape, lambda i: (0,) * len(shape), pipeline_mode=pl.Buffered(1))


def _merge(x2, o_fox, o_sb, o_dsa, wg, wuf, wus, wud, wo, g, bvec):
    t = x2.shape[0]
    tm = min(TM_MLP, t)

    def rows(width):
        return pl.BlockSpec((tm, width), lambda i: (i, 0))

    return pl.pallas_call(
        _merge_kernel,
        grid=(t // tm,),
        in_specs=[rows(D_MODEL), rows(W_FOX), rows(W_SB), rows(W_DSA),
                  _const_spec(wg.shape), _const_spec(wuf.shape), _const_spec(wus.shape),
                  _const_spec(wud.shape), _const_spec(wo.shape),
                  _const_spec(g.shape), _const_spec(bvec.shape)],
        out_specs=rows(D_MODEL),
        out_shape=jax.ShapeDtypeStruct((t, D_MODEL), F32),
        compiler_params=pltpu.CompilerParams(
            dimension_semantics=("arbitrary",), vmem_limit_bytes=VMEM_LIMIT),
        name="merge",
    )(x2, o_fox, o_sb, o_dsa, wg, wuf, wus, wud, wo, g, bvec)


FF_CHUNK = 1024


def _ffn_kernel(x_ref, p_ref, w1_ref, w2_ref, wpg_ref, wp_ref, g_ref, b_ref, o_ref):
    x = x_ref[...]
    xb = x.astype(BF16)
    acc = None
    for c in range(D_FF // FF_CHUNK):
        h = jnp.maximum(_dot(xb, w1_ref[:, c * FF_CHUNK:(c + 1) * FF_CHUNK]), 0.0)
        term = _dot((h * h).astype(BF16), w2_ref[c * FF_CHUNK:(c + 1) * FF_CHUNK, :])
        acc = term if acc is None else acc + term
    ple = jax.nn.sigmoid(_dot(xb, wpg_ref[...])) * _dot(p_ref[...].astype(BF16), wp_ref[...])
    y = DEEPNORM_ALPHA * x + acc + ple
    o_ref[...] = _layer_norm(y, g_ref[...], b_ref[...])


def _ffn(x2, p2, w1, w2, wpg, wp, g, bvec):
    t = x2.shape[0]
    tm = min(TM_MLP, t)

    def rows(width):
        return pl.BlockSpec((tm, width), lambda i: (i, 0))

    return pl.pallas_call(
        _ffn_kernel,
        grid=(t // tm,),
        in_specs=[rows(D_MODEL), rows(PLE_DIM),
                  _const_spec(w1.shape), _const_spec(w2.shape), _const_spec(wpg.shape),
                  _const_spec(wp.shape), _const_spec(g.shape), _const_spec(bvec.shape)],
        out_specs=rows(D_MODEL),
        out_shape=jax.ShapeDtypeStruct((t, D_MODEL), F32),
        compiler_params=pltpu.CompilerParams(
            dimension_semantics=("arbitrary",), vmem_limit_bytes=VMEM_LIMIT),
        name="ffn",
    )(x2, p2, w1, w2, wpg, wp, g, bvec)


def _pack_w_in(w):
    d = w.shape[0]
    offs = {}
    o = 0
    for name, width in (("fq", W_FOX), ("fk", W_FOX), ("fv", W_FOX), ("ff", HEADS_FOX),
                        ("sq", W_SB), ("sk", W_SB), ("sv", W_SB),
                        ("dq", W_DSA), ("dk", HEAD_DIM), ("dv", HEAD_DIM),
                        ("iq", W_IDX), ("ik", HEAD_DIM), ("iw", IDX_HEADS),
                        ("g", N_BRANCH * D_MODEL)):
        offs[name] = w[:, o:o + width]
        o += width

    def z(n):
        return jnp.zeros((d, n), w.dtype)

    pad_sb = W_SB_PAD - W_SB
    packed = jnp.concatenate(
        [offs["fq"], offs["fk"], offs["fv"],
         offs["sq"], z(pad_sb), offs["sk"], z(pad_sb), offs["sv"], z(pad_sb),
         offs["dq"], offs["dk"], offs["iq"], offs["ik"], offs["dv"],
         offs["ff"], z(IW_LANE - HEADS_FOX), offs["iw"], z(LANE - IW_LANE - IDX_HEADS)], axis=1)
    assert packed.shape[1] == C_PACK
    return packed.astype(BF16), offs["g"].astype(BF16)


def _rope_table(s):
    inv = ROPE_THETA ** (-jnp.arange(HALF, dtype=F32) / HALF)
    ang = jnp.arange(s, dtype=F32)[:, None] * inv[None, :]
    cos, sin = jnp.cos(ang), jnp.sin(ang)
    cos64 = jnp.concatenate([cos, cos], axis=1)
    sin64 = jnp.concatenate([-sin, sin], axis=1)
    one, zero = jnp.ones_like(cos64), jnp.zeros_like(cos64)
    return jnp.concatenate([cos64, cos64, sin64, sin64, cos64, one, sin64, zero], axis=1)


def kernel(x, p, w_in, b_forget, w_up_fox, w_up_sb, w_up_dsa, w_out, ln1_g, ln1_b,
           w_ff_in, w_ff_out, w_ple, w_ple_gate, ln2_g, ln2_b):
    b, s, d = x.shape
    depth = w_in.shape[0]
    rope_tab = _rope_table(s)
    for i in range(depth):
        w_pack, w_gate = _pack_w_in(w_in[i])
        bias_row = jnp.zeros((1, LANE), F32).at[0, :HEADS_FOX].set(b_forget[i].astype(F32))
        (fq, fk, fv, sq, sk, sv, dq, dkk, iq, ikv, cw, ct) = _proj(x, w_pack, bias_row, rope_tab)
        o_fox = _fox(fq, fk, fv, cw, ct)
        o_sb = _sb(sq, sk, sv)
        o_dsa = _dsa(dq, iq, cw, dkk, ikv)
        x1 = _merge(x.reshape(b * s, d), o_fox.reshape(b * s, W_FOX), o_sb.reshape(b * s, W_SB),
                    o_dsa.reshape(b * s, W_DSA), w_gate,
                    w_up_fox[i].astype(BF16), w_up_sb[i].astype(BF16), w_up_dsa[i].astype(BF16),
                    w_out[i].astype(BF16), ln1_g[i][None, :], ln1_b[i][None, :])
        x2 = _ffn(x1, p[i].reshape(b * s, PLE_DIM), w_ff_in[i].astype(BF16), w_ff_out[i].astype(BF16),
                  w_ple_gate[i].astype(BF16), w_ple[i].astype(BF16), ln2_g[i][None, :], ln2_b[i][None, :])
        x = x2.reshape(b, s, d)
    return x
```

```python
import functools

import jax
import jax.numpy as jnp
from jax import lax
from jax.experimental import pallas as pl
from jax.experimental.pallas import tpu as pltpu

D_MODEL = 1024
HEAD_DIM = 64
HALF = HEAD_DIM // 2
HEADS_FOX = 6
HEADS_SB = 5
HEADS_DSA = 5
IDX_HEADS = 8
CHUNK = 64
DSA_TOPK_MAX = 256
D_FF = 4 * D_MODEL
PLE_DIM = 256
ROPE_THETA = 10000.0
LN_EPS = 1e-5
N_BRANCH = 3
NEG = -1e30
DEPTH = 2
DEEPNORM_ALPHA = (2 * DEPTH) ** 0.25
W_FOX = HEADS_FOX * HEAD_DIM
W_SB = HEADS_SB * HEAD_DIM
W_DSA = HEADS_DSA * HEAD_DIM
W_IDX = IDX_HEADS * HEAD_DIM
SCALE = HEAD_DIM ** -0.5

LANE = 128
SUBLANE = 8
PACKED_ROWS = 16
W_SB_PAD = 384
IW_LANE = 8
N_CTERM = 3
VMEM_LIMIT = 56 * 1024 * 1024

TM_PROJ = 512
TKC = 256
TQ_DSA = 128
TK_DSA = TKC
TM_MLP = 512

BF16 = jnp.bfloat16
F32 = jnp.float32


def _log_sigmoid(x):
    return jnp.minimum(x, 0.0) - jnp.log(1.0 + jnp.exp(-jnp.abs(x)))


def _dot(a, b):
    return jnp.dot(a, b, preferred_element_type=F32)


def _dot_nt(a, b):
    return lax.dot_general(a, b, (((1,), (1,)), ((), ())), preferred_element_type=F32)


def _split3(v):
    hi = v.astype(BF16)
    r = v - hi.astype(F32)
    mid = r.astype(BF16)
    lo = (r - mid.astype(F32)).astype(BF16)
    return hi, mid, lo


def _pair_rows(qt_h, h):
    z = jnp.zeros_like(qt_h)
    return jnp.concatenate([z, qt_h] if h % 2 else [qt_h, z], axis=0)


OFF_FK = 0
OFF_SK = 384
OFF_DK2 = 768
OFF_IK2 = 896
OFF_SM = 1024
C_PACK = 1152
OFF_TFQ, OFF_TSQ, OFF_TDQ, OFF_TIQ = 0, 384, 704, 1024
OFF_TFV, OFF_TSV, OFF_TDV = 1536, 1920, 2240
R_PACK = 2304
CSTRIDE = PACKED_ROWS


def _swap_halves(h):
    lane = lax.broadcasted_iota(jnp.int32, h.shape, 1)
    first = (lane % HEAD_DIM) < HALF
    return jnp.where(first, pltpu.roll(h, LANE - HALF, 1), pltpu.roll(h, HALF, 1))


def _rope_rows(h, cos, sin):
    return h * cos + _swap_halves(h) * sin


def _rope_t(ht, cos_t, sin_t):
    parts = []
    for h in range(ht.shape[0] // HEAD_DIM):
        x = ht[h * HEAD_DIM:(h + 1) * HEAD_DIM]
        swapped = jnp.concatenate([x[HALF:], x[:HALF]], axis=0)
        parts.append(x * cos_t + swapped * sin_t)
    return jnp.concatenate(parts, axis=0)


def _proj_kernel(x_ref, w_ref, wt_ref, bias_ref, tab_ref, tabt_ref, place_ref, ones_ref,
                 fqt_ref, fqct_ref, fk_ref, fkc_ref, fvt_ref, sqt_ref, sk_ref, svt_ref,
                 dqt_ref, iqt_ref, dk2_ref, ik2_ref, dvt_ref, iwt_ref, carry_ref):
    tm = x_ref.shape[1]
    xb = x_ref[0].astype(BF16)

    def seg(off, width):
        return _dot(xb, w_ref[:, off:off + width])

    def seg_t(off, nrows):
        return _dot_nt(wt_ref[off:off + nrows, :], xb)

    def put_chunks(ref, val, chunk):
        for r in range(tm // chunk):
            ref[0, r] = val[:, r * chunk:(r + 1) * chunk].astype(ref.dtype)

    fk_ref[0] = seg(OFF_FK, W_FOX).astype(BF16)
    sk_ref[0] = seg(OFF_SK, W_SB_PAD)[:, :W_SB].astype(BF16)
    cos, sin = tab_ref[:, 0:LANE], tab_ref[:, LANE:2 * LANE]
    dk2_ref[0] = _rope_rows(seg(OFF_DK2, LANE), cos, sin).astype(BF16)
    ik2_ref[0] = _rope_rows(seg(OFF_IK2, LANE), cos, sin).astype(BF16)

    cos_t, sin_t = tabt_ref[0:HEAD_DIM, :], tabt_ref[HEAD_DIM:2 * HEAD_DIM, :]
    put_chunks(fqt_ref, seg_t(OFF_TFQ, W_FOX) * SCALE, TKC)
    put_chunks(sqt_ref, seg_t(OFF_TSQ, W_SB) * SCALE, TKC)
    put_chunks(dqt_ref, _rope_t(seg_t(OFF_TDQ, W_DSA), cos_t, sin_t) * SCALE, TQ_DSA)
    put_chunks(iqt_ref, _rope_t(seg_t(OFF_TIQ, W_IDX), cos_t, sin_t), TQ_DSA)
    put_chunks(fvt_ref, seg_t(OFF_TFV, W_FOX), TKC)
    put_chunks(svt_ref, seg_t(OFF_TSV, W_SB), TKC)
    put_chunks(dvt_ref, seg_t(OFF_TDV, HEAD_DIM), TKC)

    small = seg(OFF_SM, LANE)
    logf = _log_sigmoid(small + bias_ref[...])

    @pl.when(pl.program_id(1) == 0)
    def _():
        carry_ref[...] = jnp.zeros_like(carry_ref)

    row = lax.broadcasted_iota(jnp.int32, (tm, tm), 0)
    col = lax.broadcasted_iota(jnp.int32, (tm, tm), 1)
    tri = jnp.where(col <= row, 1.0, 0.0).astype(BF16)
    c = _dot(tri, jnp.concatenate(_split3(logf), axis=1))
    c = c[:, 0:LANE] + c[:, LANE:2 * LANE] + c[:, 2 * LANE:3 * LANE] + carry_ref[...]
    carry_ref[...] = c[tm - 1:tm, :]
    placed = _dot(jnp.concatenate(_split3(c), axis=1), place_ref[...]) + ones_ref[...]
    put_chunks(fqct_ref, placed[:, 0:LANE].T, TKC)
    fkc_ref[0] = placed[:, LANE:2 * LANE].astype(BF16)
    put_chunks(iwt_ref, small.T[IW_LANE:IW_LANE + IDX_HEADS], TQ_DSA)


def _proj(x, w_pack, wt_pack, bias_row, rope_tab, rope_tab_t, place, ones_row):
    b, s, d = x.shape
    tm = min(TM_PROJ, s)
    grid = (b, s // tm)

    def rows(width, dtype=BF16):
        return (jax.ShapeDtypeStruct((b, s, width), dtype),
                pl.BlockSpec((1, tm, width), lambda bi, si: (bi, si, 0)))

    def chunks(nrows, chunk, dtype=BF16):
        return (jax.ShapeDtypeStruct((b, s // chunk, nrows, chunk), dtype),
                pl.BlockSpec((1, tm // chunk, nrows, chunk), lambda bi, si: (bi, si, 0, 0)))

    def const(shape):
        return pl.BlockSpec(shape, lambda bi, si: (0,) * len(shape))

    outs = [chunks(W_FOX, TKC), chunks(LANE, TKC), rows(W_FOX), rows(LANE), chunks(W_FOX, TKC),
            chunks(W_SB, TKC), rows(W_SB), chunks(W_SB, TKC),
            chunks(W_DSA, TQ_DSA), chunks(W_IDX, TQ_DSA), rows(LANE), rows(LANE), chunks(HEAD_DIM, TKC),
            chunks(IDX_HEADS, TQ_DSA, F32)]
    return pl.pallas_call(
        _proj_kernel,
        grid=grid,
        in_specs=[pl.BlockSpec((1, tm, d), lambda bi, si: (bi, si, 0)),
                  const((d, C_PACK)), const((R_PACK, d)), const((1, LANE)),
                  pl.BlockSpec((tm, 2 * LANE), lambda bi, si: (si, 0)),
                  pl.BlockSpec((2 * HEAD_DIM, tm), lambda bi, si: (0, si)),
                  const(place.shape), const(ones_row.shape)],
        out_specs=[o[1] for o in outs],
        out_shape=[o[0] for o in outs],
        scratch_shapes=[pltpu.VMEM((1, LANE), F32)],
        compiler_params=pltpu.CompilerParams(
            dimension_semantics=("arbitrary", "arbitrary"), vmem_limit_bytes=VMEM_LIMIT),
        name="proj",
    )(x, w_pack, wt_pack, bias_row, rope_tab, rope_tab_t, place, ones_row)


def _fox_kernel(qt_ref, qct_ref, k_ref, kc_ref, vt_ref, o_ref, qcat_ref, m_ref, l_ref, acc_ref):
    tq = tk = TKC
    i = pl.program_id(1)
    krow = lax.broadcasted_iota(jnp.int32, (tk, tq), 0)
    qcol = lax.broadcasted_iota(jnp.int32, (tk, tq), 1)
    causal = krow <= qcol

    for h in range(HEADS_FOX):
        gate_rows = jnp.concatenate(
            ([jnp.zeros((CSTRIDE * h, tq), BF16)] if h else [])
            + [qct_ref[0, 0, CSTRIDE * h:CSTRIDE * (h + 1), :],
               jnp.zeros((LANE - CSTRIDE * (h + 1), tq), BF16)], axis=0)
        qcat_ref[h] = jnp.concatenate(
            [_pair_rows(qt_ref[0, 0, h * HEAD_DIM:(h + 1) * HEAD_DIM, :], h), gate_rows], axis=0)
    m_ref[...] = jnp.full(m_ref.shape, NEG, F32)
    l_ref[...] = jnp.zeros(l_ref.shape, F32)
    acc_ref[...] = jnp.zeros(acc_ref.shape, F32)

    def block(j, masked):
        ks = pl.multiple_of(j * tk, tk)
        kc = kc_ref[0, pl.ds(ks, tk), :]
        vt = vt_ref[0, j]
        scores = []
        for g in range(HEADS_FOX // 2):
            kcat = jnp.concatenate([k_ref[0, pl.ds(ks, tk), g * LANE:(g + 1) * LANE], kc], axis=1)
            for h in (2 * g, 2 * g + 1):
                scores.append(_dot(kcat, qcat_ref[h]))
        for h in range(HEADS_FOX):
            s = scores[h]
            if masked:
                s = jnp.where(causal, s, NEG)
            m_old = m_ref[h]
            m_new = jnp.maximum(m_old, jnp.max(s, axis=0, keepdims=True))
            alpha = jnp.exp(m_old - m_new)
            p = jnp.exp(s - m_new)
            m_ref[h] = m_new
            l_ref[h] = alpha * l_ref[h] + jnp.sum(p, axis=0, keepdims=True)
            acc_ref[h] = alpha * acc_ref[h] + _dot(vt[h * HEAD_DIM:(h + 1) * HEAD_DIM, :],
                                                   p.astype(BF16))

    def body(j, _):
        block(j, False)
        return 0

    lax.fori_loop(0, i, body, 0)
    block(i, True)
    o_ref[0] = jnp.concatenate([(acc_ref[h] / l_ref[h]).T for h in range(HEADS_FOX)],
                               axis=1).astype(o_ref.dtype)


def _fox(fqt, fqct, fk, fkc, fvt):
    b, s, w = fk.shape
    tq = TKC
    return pl.pallas_call(
        _fox_kernel,
        grid=(b, s // tq),
        in_specs=[pl.BlockSpec((1, 1, w, tq), lambda bi, qi: (bi, qi, 0, 0)),
                  pl.BlockSpec((1, 1, LANE, tq), lambda bi, qi: (bi, qi, 0, 0)),
                  pl.BlockSpec((1, s, w), lambda bi, qi: (bi, 0, 0)),
                  pl.BlockSpec((1, s, LANE), lambda bi, qi: (bi, 0, 0)),
                  pl.BlockSpec((1, s // TKC, w, TKC), lambda bi, qi: (bi, 0, 0, 0))],
        out_specs=pl.BlockSpec((1, tq, w), lambda bi, qi: (bi, qi, 0)),
        out_shape=jax.ShapeDtypeStruct((b, s, w), BF16),
        scratch_shapes=[pltpu.VMEM((HEADS_FOX, 2 * LANE, tq), BF16),
                        pltpu.VMEM((HEADS_FOX, 1, tq), F32),
                        pltpu.VMEM((HEADS_FOX, 1, tq), F32),
                        pltpu.VMEM((HEADS_FOX, HEAD_DIM, tq), F32)],
        compiler_params=pltpu.CompilerParams(
            dimension_semantics=("arbitrary", "arbitrary"), vmem_limit_bytes=VMEM_LIMIT),
        name="fox",
    )(fqt, fqct, fk, fkc, fvt)


def _sb_kernel(qt_ref, k_ref, vt_ref, o_ref, qm_ref, later_w_ref, run_ref, acc_ref):
    tq = tk = TKC
    i = pl.program_id(1)
    krow = lax.broadcasted_iota(jnp.int32, (tk, tq), 0)
    qcol = lax.broadcasted_iota(jnp.int32, (tk, tq), 1)
    strict = krow < qcol
    r = lax.broadcasted_iota(jnp.int32, (tk, 2 * tk), 0)
    c = lax.broadcasted_iota(jnp.int32, (tk, 2 * tk), 1) % tk
    later_w_ref[...] = jnp.where(c > r, 1.0, 0.0).astype(BF16)
    for h in range(HEADS_SB):
        qm_ref[h] = _pair_rows(qt_ref[0, 0, h * HEAD_DIM:(h + 1) * HEAD_DIM, :], h)
    run_ref[...] = jnp.zeros(run_ref.shape, F32)
    acc_ref[...] = jnp.zeros(acc_ref.shape, F32)

    def block(j, masked):
        ks = pl.multiple_of(j * tk, tk)
        vt = vt_ref[0, j]
        zs = []
        for h in range(HEADS_SB):
            g = h // 2
            width = min(LANE, W_SB - g * LANE)
            kp = k_ref[0, pl.ds(ks, tk), g * LANE:g * LANE + width]
            zs.append(_dot(kp, qm_ref[h][:width, :]))
        lns = []
        for h in range(HEADS_SB):
            ln = _log_sigmoid(-zs[h])
            lns.append(jnp.where(strict, ln, 0.0) if masked else ln)
        laters = []
        for h in range(HEADS_SB):
            ln_hi = lns[h].astype(BF16)
            ln_lo = (lns[h] - ln_hi.astype(F32)).astype(BF16)
            laters.append(_dot(later_w_ref[...], jnp.concatenate([ln_hi, ln_lo], axis=0)))
        for h in range(HEADS_SB):
            run = run_ref[h]
            a = jnp.exp(zs[h] + lns[h] + laters[h] + run)
            if masked:
                a = jnp.where(strict, a, 0.0)
            acc_ref[h] = acc_ref[h] + _dot(vt[h * HEAD_DIM:(h + 1) * HEAD_DIM, :], a.astype(BF16))
            run_ref[h] = run + laters[h][0:1, :] + lns[h][0:1, :]

    block(i, True)

    def body(t, _):
        block(i - 1 - t, False)
        return 0

    lax.fori_loop(0, i, body, 0)
    o_ref[0] = jnp.concatenate([acc_ref[h].T for h in range(HEADS_SB)], axis=1).astype(o_ref.dtype)


def _sb(sqt, sk, svt):
    b, s, w = sk.shape
    tq = TKC
    return pl.pallas_call(
        _sb_kernel,
        grid=(b, s // tq),
        in_specs=[pl.BlockSpec((1, 1, w, tq), lambda bi, qi: (bi, qi, 0, 0)),
                  pl.BlockSpec((1, s, w), lambda bi, qi: (bi, 0, 0)),
                  pl.BlockSpec((1, s // TKC, w, TKC), lambda bi, qi: (bi, 0, 0, 0))],
        out_specs=pl.BlockSpec((1, tq, w), lambda bi, qi: (bi, qi, 0)),
        out_shape=jax.ShapeDtypeStruct((b, s, w), BF16),
        scratch_shapes=[pltpu.VMEM((HEADS_SB, LANE, tq), BF16),
                        pltpu.VMEM((TKC, 2 * TKC), BF16),
                        pltpu.VMEM((HEADS_SB, 1, tq), F32),
                        pltpu.VMEM((HEADS_SB, HEAD_DIM, tq), F32)],
        compiler_params=pltpu.CompilerParams(
            dimension_semantics=("arbitrary", "arbitrary"), vmem_limit_bytes=VMEM_LIMIT),
        name="sb",
    )(sqt, sk, svt)


INT_MIN = -2 ** 31


def _sortable(score):
    bits = lax.bitcast_convert_type(score, jnp.int32)
    return jnp.where(bits < 0, bits ^ jnp.int32(0x7FFFFFFF), bits)


def _sum_sublane_groups(x):
    parts = [x[r * SUBLANE:(r + 1) * SUBLANE] for r in range(x.shape[0] // SUBLANE)]
    while len(parts) > 1:
        parts = [parts[k] + parts[k + 1] for k in range(0, len(parts) - 1, 2)] + (
            [parts[-1]] if len(parts) % 2 else [])
    return parts[0]


def _dsa_kernel(dqt_ref, iqt_ref, iwt_ref, dk2_ref, ik2_ref, dvt_ref, o_ref,
                key_ref, bias_ref, iq8_ref, q5_ref, m_ref, l_ref, acc_ref, *, n_sel, idx_bits):
    tq, tk = TQ_DSA, TK_DSA
    i = pl.program_id(1)
    nkb = (i * tq) // tk + 1
    kloc = lax.broadcasted_iota(jnp.int32, (tk, tq), 0)
    qpos = i * tq + lax.broadcasted_iota(jnp.int32, (tk, tq), 1)
    klimit = (qpos // CHUNK + 1) * CHUNK
    for h in range(IDX_HEADS):
        iq8_ref[:, h * tq:(h + 1) * tq] = _pair_rows(iqt_ref[0, 0, h * HEAD_DIM:(h + 1) * HEAD_DIM, :], h)
    for h in range(HEADS_DSA):
        q5_ref[:, h * tq:(h + 1) * tq] = _pair_rows(dqt_ref[0, 0, h * HEAD_DIM:(h + 1) * HEAD_DIM, :], h)
    iwt = iwt_ref[0, 0]

    def score_block(j, _):
        ks = pl.multiple_of(j * tk, tk)
        a = _dot(ik2_ref[0, pl.ds(ks, tk), :], iq8_ref[...])
        sc = jnp.zeros((tk, tq), F32)
        for h in range(IDX_HEADS):
            sc = sc + iwt[h:h + 1, :] * jnp.maximum(a[:, h * tq:(h + 1) * tq], 0.0)
        sc = sc + 0.0
        sc = jnp.where(kloc + ks < klimit, sc, NEG)
        key_ref[j] = _sortable(sc)
        return 0

    lax.fori_loop(0, nkb, score_block, 0)

    def count(pred):
        def body(j, acc):
            hit = jnp.where(pred(key_ref[j], kloc + j * tk), 1.0, 0.0)
            return acc + _sum_sublane_groups(hit)
        acc = lax.fori_loop(0, nkb, body, jnp.zeros((SUBLANE, tq), F32))
        return jnp.sum(acc, axis=0, keepdims=True)

    def thr_bit(t, thr_u):
        bit = lax.shift_left(jnp.int32(1), jnp.int32(31) - t)
        cand = (thr_u | bit) ^ jnp.int32(INT_MIN)
        cnt = count(lambda k, _: k >= cand)
        return jnp.where(cnt >= n_sel, thr_u | bit, thr_u)

    thr_u = lax.fori_loop(0, 32, thr_bit, jnp.zeros((1, tq), jnp.int32))
    thr = thr_u ^ jnp.int32(INT_MIN)

    need = n_sel - count(lambda k, _: k > thr)

    def idx_bit(t, jmax):
        bit = lax.shift_left(jnp.int32(1), jnp.int32(idx_bits - 1) - t)
        cand = jmax | bit
        cnt = count(lambda k, pos: (k == thr) & (pos < cand))
        return jnp.where(cnt < need, cand, jmax)

    jmax = lax.fori_loop(0, idx_bits, idx_bit, jnp.zeros((1, tq), jnp.int32))

    def bias_block(j, _):
        k = key_ref[j]
        pos = kloc + j * tk
        sel = ((k > thr) | ((k == thr) & (pos <= jmax))) & (pos < klimit)
        bias_ref[j] = jnp.where(sel, 0.0, NEG)
        return 0

    lax.fori_loop(0, nkb, bias_block, 0)

    m_ref[...] = jnp.full(m_ref.shape, NEG, F32)
    l_ref[...] = jnp.zeros(l_ref.shape, F32)
    acc_ref[...] = jnp.zeros(acc_ref.shape, F32)

    def att_block(j, _):
        ks = pl.multiple_of(j * tk, tk)
        s = _dot(dk2_ref[0, pl.ds(ks, tk), :], q5_ref[...])
        s = s + jnp.concatenate([bias_ref[j]] * HEADS_DSA, axis=1)
        m_old = m_ref[...]
        m_new = jnp.maximum(m_old, jnp.max(s, axis=0, keepdims=True))
        alpha = jnp.exp(m_old - m_new)
        p = jnp.exp(s - m_new)
        m_ref[...] = m_new
        l_ref[...] = alpha * l_ref[...] + jnp.sum(p, axis=0, keepdims=True)
        acc_ref[...] = alpha * acc_ref[...] + _dot(dvt_ref[0, j], p.astype(BF16))
        return 0

    lax.fori_loop(0, nkb, att_block, 0)
    out = acc_ref[...] / l_ref[...]
    o_ref[0] = jnp.concatenate([out[:, h * tq:(h + 1) * tq].T for h in range(HEADS_DSA)],
                               axis=1).astype(o_ref.dtype)


def _dsa(dqt, iqt, iwt, dk2, ik2, dvt):
    b, s, _ = dk2.shape
    tq, tk = TQ_DSA, TK_DSA
    n_sel = min(DSA_TOPK_MAX, s // 4)
    assert n_sel <= tk and s % tk == 0
    idx_bits = max(1, (s - 1).bit_length())
    return pl.pallas_call(
        functools.partial(_dsa_kernel, n_sel=n_sel, idx_bits=idx_bits),
        grid=(b, s // tq),
        in_specs=[pl.BlockSpec((1, 1, W_DSA, tq), lambda bi, qi: (bi, qi, 0, 0)),
                  pl.BlockSpec((1, 1, W_IDX, tq), lambda bi, qi: (bi, qi, 0, 0)),
                  pl.BlockSpec((1, 1, IDX_HEADS, tq), lambda bi, qi: (bi, qi, 0, 0)),
                  pl.BlockSpec((1, s, LANE), lambda bi, qi: (bi, 0, 0)),
                  pl.BlockSpec((1, s, LANE), lambda bi, qi: (bi, 0, 0)),
                  pl.BlockSpec((1, s // tk, HEAD_DIM, tk), lambda bi, qi: (bi, 0, 0, 0))],
        out_specs=pl.BlockSpec((1, tq, W_DSA), lambda bi, qi: (bi, qi, 0)),
        out_shape=jax.ShapeDtypeStruct((b, s, W_DSA), BF16),
        scratch_shapes=[pltpu.VMEM((s // tk, tk, tq), jnp.int32),
                        pltpu.VMEM((s // tk, tk, tq), F32),
                        pltpu.VMEM((LANE, IDX_HEADS * tq), BF16),
                        pltpu.VMEM((LANE, HEADS_DSA * tq), BF16),
                        pltpu.VMEM((1, HEADS_DSA * tq), F32),
                        pltpu.VMEM((1, HEADS_DSA * tq), F32),
                        pltpu.VMEM((HEAD_DIM, HEADS_DSA * tq), F32)],
        compiler_params=pltpu.CompilerParams(
            dimension_semantics=("arbitrary", "arbitrary"), vmem_limit_bytes=VMEM_LIMIT),
        name="dsa",
    )(dqt, iqt, iwt, dk2, ik2, dvt)


def _layer_norm(y, g, b):
    mu = jnp.mean(y, axis=1, keepdims=True)
    d = y - mu
    var = jnp.mean(d * d, axis=1, keepdims=True)
    return d * lax.rsqrt(var + LN_EPS) * g + b


def _merge_kernel(x_ref, of_ref, os_ref, od_ref, wg_ref, wuf_ref, wus_ref, wud_ref, wo_ref,
                  g_ref, b_ref, o_ref):
    x = x_ref[...]
    xb = x.astype(BF16)
    merged = None
    for n, (br_ref, wu_ref) in enumerate(((of_ref, wuf_ref), (os_ref, wus_ref), (od_ref, wud_ref))):
        gate = jax.nn.sigmoid(_dot(xb, wg_ref[:, n * D_MODEL:(n + 1) * D_MODEL]))
        term = gate * _dot(br_ref[...], wu_ref[...])
        merged = term if merged is None else merged + term
    y = DEEPNORM_ALPHA * x + _dot(merged.astype(BF16), wo_ref[...])
    o_ref[...] = _layer_norm(y, g_ref[...], b_ref[...])


def _const_spec(shape):
    return pl.BlockSpec(shape, lambda i: (0,) * len(shape), pipeline_mode=pl.Buffered(1))


def _merge(x2, o_fox, o_sb, o_dsa, wg, wuf, wus, wud, wo, g, bvec):
    t = x2.shape[0]
    tm = min(TM_MLP, t)

    def rows(width):
        return pl.BlockSpec((tm, width), lambda i: (i, 0))

    return pl.pallas_call(
        _merge_kernel,
        grid=(t // tm,),
        in_specs=[rows(D_MODEL), rows(W_FOX), rows(W_SB), rows(W_DSA),
                  _const_spec(wg.shape), _const_spec(wuf.shape), _const_spec(wus.shape),
                  _const_spec(wud.shape), _const_spec(wo.shape),
                  _const_spec(g.shape), _const_spec(bvec.shape)],
        out_specs=rows(D_MODEL),
        out_shape=jax.ShapeDtypeStruct((t, D_MODEL), F32),
        compiler_params=pltpu.CompilerParams(
            dimension_semantics=("arbitrary",), vmem_limit_bytes=VMEM_LIMIT),
        name="merge",
    )(x2, o_fox, o_sb, o_dsa, wg, wuf, wus, wud, wo, g, bvec)


FF_CHUNK = 1024


def _ffn_kernel(x_ref, p_ref, w1_ref, w2_ref, wpg_ref, wp_ref, g_ref, b_ref, o_ref):
    x = x_ref[...]
    xb = x.astype(BF16)
    acc = None
    for c in range(D_FF // FF_CHUNK):
        h = jnp.maximum(_dot(xb, w1_ref[:, c * FF_CHUNK:(c + 1) * FF_CHUNK]), 0.0)
        term = _dot((h * h).astype(BF16), w2_ref[c * FF_CHUNK:(c + 1) * FF_CHUNK, :])
        acc = term if acc is None else acc + term
    ple = jax.nn.sigmoid(_dot(xb, wpg_ref[...])) * _dot(p_ref[...].astype(BF16), wp_ref[...])
    y = DEEPNORM_ALPHA * x + acc + ple
    o_ref[...] = _layer_norm(y, g_ref[...], b_ref[...])


def _ffn(x2, p2, w1, w2, wpg, wp, g, bvec):
    t = x2.shape[0]
    tm = min(TM_MLP, t)

    def rows(width):
        return pl.BlockSpec((tm, width), lambda i: (i, 0))

    return pl.pallas_call(
        _ffn_kernel,
        grid=(t // tm,),
        in_specs=[rows(D_MODEL), rows(PLE_DIM),
                  _const_spec(w1.shape), _const_spec(w2.shape), _const_spec(wpg.shape),
                  _const_spec(wp.shape), _const_spec(g.shape), _const_spec(bvec.shape)],
        out_specs=rows(D_MODEL),
        out_shape=jax.ShapeDtypeStruct((t, D_MODEL), F32),
        compiler_params=pltpu.CompilerParams(
            dimension_semantics=("arbitrary",), vmem_limit_bytes=VMEM_LIMIT),
        name="ffn",
    )(x2, p2, w1, w2, wpg, wp, g, bvec)


def _pack_w_in(w):
    d = w.shape[0]
    offs = {}
    o = 0
    for name, width in (("fq", W_FOX), ("fk", W_FOX), ("fv", W_FOX), ("ff", HEADS_FOX),
                        ("sq", W_SB), ("sk", W_SB), ("sv", W_SB),
                        ("dq", W_DSA), ("dk", HEAD_DIM), ("dv", HEAD_DIM),
                        ("iq", W_IDX), ("ik", HEAD_DIM), ("iw", IDX_HEADS),
                        ("g", N_BRANCH * D_MODEL)):
        offs[name] = w[:, o:o + width]
        o += width

    def z(n):
        return jnp.zeros((d, n), w.dtype)

    packed = jnp.concatenate(
        [offs["fk"], offs["sk"], z(W_SB_PAD - W_SB), offs["dk"], offs["dk"], offs["ik"], offs["ik"],
         offs["ff"], z(IW_LANE - HEADS_FOX), offs["iw"], z(LANE - IW_LANE - IDX_HEADS)], axis=1)
    assert packed.shape[1] == C_PACK
    packed_t = jnp.concatenate([offs["fq"], offs["sq"], offs["dq"], offs["iq"],
                                offs["fv"], offs["sv"], offs["dv"]], axis=1).T
    assert packed_t.shape[0] == R_PACK
    return packed.astype(BF16), packed_t.astype(BF16), offs["g"].astype(BF16)


def _rope_tables(s):
    inv = ROPE_THETA ** (-jnp.arange(HALF, dtype=F32) / HALF)
    ang = jnp.arange(s, dtype=F32)[:, None] * inv[None, :]
    cos, sin = jnp.cos(ang), jnp.sin(ang)
    row_major = jnp.concatenate([cos, cos, cos, cos, -sin, sin, -sin, sin], axis=1)
    transposed = jnp.concatenate([cos, cos, -sin, sin], axis=1).T
    return row_major, transposed


def _placement():
    rows_q, cols_q, rows_k, cols_k = [], [], [], []
    for h in range(HEADS_FOX):
        for t in range(N_CTERM):
            rows_q.append(t * LANE + h), cols_q.append(CSTRIDE * h + t)
            rows_k.append(t * LANE + h), cols_k.append(LANE + CSTRIDE * h + N_CTERM + t)
    place = jnp.zeros((N_CTERM * LANE, 2 * LANE), F32)
    place = place.at[jnp.array(rows_q), jnp.array(cols_q)].set(1.0)
    place = place.at[jnp.array(rows_k), jnp.array(cols_k)].set(-1.0)
    lane = jnp.arange(2 * LANE)
    head_lane = (lane % LANE) % CSTRIDE
    in_heads = (lane % LANE) < CSTRIDE * HEADS_FOX
    ones_q = (lane < LANE) & in_heads & (head_lane >= N_CTERM) & (head_lane < 2 * N_CTERM)
    ones_k = (lane >= LANE) & in_heads & (head_lane < N_CTERM)
    ones_row = jnp.where(ones_q | ones_k, 1.0, 0.0).astype(F32)[None, :]
    return place.astype(BF16), ones_row


def kernel(x, p, w_in, b_forget, w_up_fox, w_up_sb, w_up_dsa, w_out, ln1_g, ln1_b,
           w_ff_in, w_ff_out, w_ple, w_ple_gate, ln2_g, ln2_b):
    b, s, d = x.shape
    depth = w_in.shape[0]
    rope_tab, rope_tab_t = _rope_tables(s)
    place, ones_row = _placement()
    for i in range(depth):
        w_pack, wt_pack, w_gate = _pack_w_in(w_in[i])
        bias_row = jnp.zeros((1, LANE), F32).at[0, :HEADS_FOX].set(b_forget[i].astype(F32))
        (fqt, fqct, fk, fkc, fvt, sqt, sk, svt, dqt, iqt, dk2, ik2, dvt, iwt) = _proj(
            x, w_pack, wt_pack, bias_row, rope_tab, rope_tab_t, place, ones_row)
        o_fox = _fox(fqt, fqct, fk, fkc, fvt)
        o_sb = _sb(sqt, sk, svt)
        o_dsa = _dsa(dqt, iqt, iwt, dk2, ik2, dvt)
        x1 = _merge(x.reshape(b * s, d), o_fox.reshape(b * s, W_FOX), o_sb.reshape(b * s, W_SB),
                    o_dsa.reshape(b * s, W_DSA), w_gate,
                    w_up_fox[i].astype(BF16), w_up_sb[i].astype(BF16), w_up_dsa[i].astype(BF16),
                    w_out[i].astype(BF16), ln1_g[i][None, :], ln1_b[i][None, :])
        x2 = _ffn(x1, p[i].reshape(b * s, PLE_DIM), w_ff_in[i].astype(BF16), w_ff_out[i].astype(BF16),
                  w_ple_gate[i].astype(BF16), w_ple[i].astype(BF16), ln2_g[i][None, :], ln2_b[i][None, :])
        x = x2.reshape(b, s, d)
    return x
```

```python
import functools

import jax
import jax.numpy as jnp
from jax import lax
from jax.experimental import pallas as pl
from jax.experimental.pallas import tpu as pltpu

D_MODEL = 1024
HEAD_DIM = 64
HALF = HEAD_DIM // 2
HEADS_FOX = 6
HEADS_SB = 5
HEADS_DSA = 5
IDX_HEADS = 8
CHUNK = 64
DSA_TOPK_MAX = 256
D_FF = 4 * D_MODEL
PLE_DIM = 256
ROPE_THETA = 10000.0
LN_EPS = 1e-5
N_BRANCH = 3
NEG = -1e30
DEPTH = 2
DEEPNORM_ALPHA = (2 * DEPTH) ** 0.25
W_FOX = HEADS_FOX * HEAD_DIM
W_SB = HEADS_SB * HEAD_DIM
W_DSA = HEADS_DSA * HEAD_DIM
W_IDX = IDX_HEADS * HEAD_DIM
SCALE = HEAD_DIM ** -0.5

LANE = 128
SUBLANE = 8
PACKED_ROWS = 16
W_SB_PAD = 384
IW_LANE = 8
N_CTERM = 3
VMEM_LIMIT = 56 * 1024 * 1024

TM_PROJ = 512
TKC = 256
TQ_DSA = 128
TK_DSA = TKC
TM_MLP = 512

BF16 = jnp.bfloat16
F32 = jnp.float32


def _log_sigmoid(x):
    return jnp.minimum(x, 0.0) - jnp.log(1.0 + jnp.exp(-jnp.abs(x)))


def _dot(a, b):
    return jnp.dot(a, b, preferred_element_type=F32)


def _dot_nt(a, b):
    return lax.dot_general(a, b, (((1,), (1,)), ((), ())), preferred_element_type=F32)


def _split3(v):
    hi = v.astype(BF16)
    r = v - hi.astype(F32)
    mid = r.astype(BF16)
    lo = (r - mid.astype(F32)).astype(BF16)
    return hi, mid, lo


def _pair_rows(qt_h, h):
    z = jnp.zeros_like(qt_h)
    return jnp.concatenate([z, qt_h] if h % 2 else [qt_h, z], axis=0)


OFF_FK = 0
OFF_SK = 384
OFF_DK2 = 768
OFF_IK2 = 896
OFF_SM = 1024
C_PACK = 1152
OFF_TFQ, OFF_TSQ, OFF_TDQ, OFF_TIQ = 0, 384, 704, 1024
OFF_TFV, OFF_TSV, OFF_TDV = 1536, 1920, 2240
R_PACK = 2304
CSTRIDE = PACKED_ROWS


def _swap_halves(h):
    lane = lax.broadcasted_iota(jnp.int32, h.shape, 1)
    first = (lane % HEAD_DIM) < HALF
    return jnp.where(first, pltpu.roll(h, LANE - HALF, 1), pltpu.roll(h, HALF, 1))


def _rope_rows(h, cos, sin):
    return h * cos + _swap_halves(h) * sin


def _rope_t(ht, cos_t, sin_t):
    parts = []
    for h in range(ht.shape[0] // HEAD_DIM):
        x = ht[h * HEAD_DIM:(h + 1) * HEAD_DIM]
        swapped = jnp.concatenate([x[HALF:], x[:HALF]], axis=0)
        parts.append(x * cos_t + swapped * sin_t)
    return jnp.concatenate(parts, axis=0)


def _proj_kernel(x_ref, w_ref, wt_ref, bias_ref, tab_ref, tabt_ref, place_ref, ones_ref,
                 fqt_ref, fqct_ref, fk_ref, fkc_ref, fvt_ref, sqt_ref, sk_ref, svt_ref,
                 dqt_ref, iqt_ref, dk2_ref, ik2_ref, dvt_ref, iwt_ref, carry_ref):
    tm = x_ref.shape[1]
    xb = x_ref[0].astype(BF16)

    def seg(off, width):
        return _dot(xb, w_ref[:, off:off + width])

    def seg_t(off, nrows):
        return _dot_nt(wt_ref[off:off + nrows, :], xb)

    def put_chunks(ref, val, chunk):
        for r in range(tm // chunk):
            ref[0, r] = val[:, r * chunk:(r + 1) * chunk].astype(ref.dtype)

    fk_ref[0] = seg(OFF_FK, W_FOX).astype(BF16)
    sk_ref[0] = seg(OFF_SK, W_SB_PAD)[:, :W_SB].astype(BF16)
    cos, sin = tab_ref[:, 0:LANE], tab_ref[:, LANE:2 * LANE]
    dk2_ref[0] = _rope_rows(seg(OFF_DK2, LANE), cos, sin).astype(BF16)
    ik2_ref[0] = _rope_rows(seg(OFF_IK2, LANE), cos, sin).astype(BF16)

    cos_t, sin_t = tabt_ref[0:HEAD_DIM, :], tabt_ref[HEAD_DIM:2 * HEAD_DIM, :]
    put_chunks(fqt_ref, seg_t(OFF_TFQ, W_FOX) * SCALE, TKC)
    put_chunks(sqt_ref, seg_t(OFF_TSQ, W_SB) * SCALE, TKC)
    put_chunks(dqt_ref, _rope_t(seg_t(OFF_TDQ, W_DSA), cos_t, sin_t) * SCALE, TQ_DSA)
    put_chunks(iqt_ref, _rope_t(seg_t(OFF_TIQ, W_IDX), cos_t, sin_t), TQ_DSA)
    put_chunks(fvt_ref, seg_t(OFF_TFV, W_FOX), TKC)
    put_chunks(svt_ref, seg_t(OFF_TSV, W_SB), TKC)
    put_chunks(dvt_ref, seg_t(OFF_TDV, HEAD_DIM), TKC)

    small = seg(OFF_SM, LANE)
    logf = _log_sigmoid(small + bias_ref[...])

    @pl.when(pl.program_id(1) == 0)
    def _():
        carry_ref[...] = jnp.zeros_like(carry_ref)

    row = lax.broadcasted_iota(jnp.int32, (tm, tm), 0)
    col = lax.broadcasted_iota(jnp.int32, (tm, tm), 1)
    tri = jnp.where(col <= row, 1.0, 0.0).astype(BF16)
    c = _dot(tri, jnp.concatenate(_split3(logf), axis=1))
    c = c[:, 0:LANE] + c[:, LANE:2 * LANE] + c[:, 2 * LANE:3 * LANE] + carry_ref[...]
    carry_ref[...] = c[tm - 1:tm, :]
    placed = _dot(jnp.concatenate(_split3(c), axis=1), place_ref[...]) + ones_ref[...]
    put_chunks(fqct_ref, placed[:, 0:LANE].T, TKC)
    fkc_ref[0] = placed[:, LANE:2 * LANE].astype(BF16)
    put_chunks(iwt_ref, small.T[IW_LANE:IW_LANE + IDX_HEADS], TQ_DSA)


def _proj(x, w_pack, wt_pack, bias_row, rope_tab, rope_tab_t, place, ones_row):
    b, s, d = x.shape
    tm = min(TM_PROJ, s)
    grid = (b, s // tm)

    def rows(width, dtype=BF16):
        return (jax.ShapeDtypeStruct((b, s, width), dtype),
                pl.BlockSpec((1, tm, width), lambda bi, si: (bi, si, 0)))

    def chunks(nrows, chunk, dtype=BF16):
        return (jax.ShapeDtypeStruct((b, s // chunk, nrows, chunk), dtype),
                pl.BlockSpec((1, tm // chunk, nrows, chunk), lambda bi, si: (bi, si, 0, 0)))

    def const(shape):
        return pl.BlockSpec(shape, lambda bi, si: (0,) * len(shape))

    outs = [chunks(W_FOX, TKC), chunks(LANE, TKC), rows(W_FOX), rows(LANE), chunks(W_FOX, TKC),
            chunks(W_SB, TKC), rows(W_SB), chunks(W_SB, TKC),
            chunks(W_DSA, TQ_DSA), chunks(W_IDX, TQ_DSA), rows(LANE), rows(LANE), chunks(HEAD_DIM, TKC),
            chunks(IDX_HEADS, TQ_DSA, F32)]
    return pl.pallas_call(
        _proj_kernel,
        grid=grid,
        in_specs=[pl.BlockSpec((1, tm, d), lambda bi, si: (bi, si, 0)),
                  const((d, C_PACK)), const((R_PACK, d)), const((1, LANE)),
                  pl.BlockSpec((tm, 2 * LANE), lambda bi, si: (si, 0)),
                  pl.BlockSpec((2 * HEAD_DIM, tm), lambda bi, si: (0, si)),
                  const(place.shape), const(ones_row.shape)],
        out_specs=[o[1] for o in outs],
        out_shape=[o[0] for o in outs],
        scratch_shapes=[pltpu.VMEM((1, LANE), F32)],
        compiler_params=pltpu.CompilerParams(
            dimension_semantics=("arbitrary", "arbitrary"), vmem_limit_bytes=VMEM_LIMIT),
        name="proj",
    )(x, w_pack, wt_pack, bias_row, rope_tab, rope_tab_t, place, ones_row)


def _fox_kernel(qt_ref, qct_ref, k_ref, kc_ref, vt_ref, o_ref, qcat_ref, m_ref, l_ref, acc_ref):
    tq = tk = TKC
    i = pl.program_id(1)
    krow = lax.broadcasted_iota(jnp.int32, (tk, tq), 0)
    qcol = lax.broadcasted_iota(jnp.int32, (tk, tq), 1)
    causal = krow <= qcol

    for h in range(HEADS_FOX):
        gate_rows = jnp.concatenate(
            ([jnp.zeros((CSTRIDE * h, tq), BF16)] if h else [])
            + [qct_ref[0, 0, CSTRIDE * h:CSTRIDE * (h + 1), :],
               jnp.zeros((LANE - CSTRIDE * (h + 1), tq), BF16)], axis=0)
        qcat_ref[h] = jnp.concatenate(
            [_pair_rows(qt_ref[0, 0, h * HEAD_DIM:(h + 1) * HEAD_DIM, :], h), gate_rows], axis=0)
    m_ref[...] = jnp.full(m_ref.shape, NEG, F32)
    l_ref[...] = jnp.zeros(l_ref.shape, F32)
    acc_ref[...] = jnp.zeros(acc_ref.shape, F32)

    def block(j, masked):
        ks = pl.multiple_of(j * tk, tk)
        kc = kc_ref[0, pl.ds(ks, tk), :]
        vt = vt_ref[0, j]
        scores = []
        for g in range(HEADS_FOX // 2):
            kcat = jnp.concatenate([k_ref[0, pl.ds(ks, tk), g * LANE:(g + 1) * LANE], kc], axis=1)
            for h in (2 * g, 2 * g + 1):
                scores.append(_dot(kcat, qcat_ref[h]))
        for h in range(HEADS_FOX):
            s = scores[h]
            if masked:
                s = jnp.where(causal, s, NEG)
            m_old = m_ref[h]
            m_new = jnp.maximum(m_old, jnp.max(s, axis=0, keepdims=True))
            alpha = jnp.exp(m_old - m_new)
            p = jnp.exp(s - m_new)
            m_ref[h] = m_new
            l_ref[h] = alpha * l_ref[h] + jnp.sum(p, axis=0, keepdims=True)
            acc_ref[h] = alpha * acc_ref[h] + _dot(vt[h * HEAD_DIM:(h + 1) * HEAD_DIM, :],
                                                   p.astype(BF16))

    def body(j, _):
        block(j, False)
        return 0

    lax.fori_loop(0, i, body, 0)
    block(i, True)
    o_ref[0] = jnp.concatenate([(acc_ref[h] / l_ref[h]).T for h in range(HEADS_FOX)],
                               axis=1).astype(o_ref.dtype)


def _fox(fqt, fqct, fk, fkc, fvt):
    b, s, w = fk.shape
    tq = TKC
    return pl.pallas_call(
        _fox_kernel,
        grid=(b, s // tq),
        in_specs=[pl.BlockSpec((1, 1, w, tq), lambda bi, qi: (bi, qi, 0, 0)),
                  pl.BlockSpec((1, 1, LANE, tq), lambda bi, qi: (bi, qi, 0, 0)),
                  pl.BlockSpec((1, s, w), lambda bi, qi: (bi, 0, 0)),
                  pl.BlockSpec((1, s, LANE), lambda bi, qi: (bi, 0, 0)),
                  pl.BlockSpec((1, s // TKC, w, TKC), lambda bi, qi: (bi, 0, 0, 0))],
        out_specs=pl.BlockSpec((1, tq, w), lambda bi, qi: (bi, qi, 0)),
        out_shape=jax.ShapeDtypeStruct((b, s, w), BF16),
        scratch_shapes=[pltpu.VMEM((HEADS_FOX, 2 * LANE, tq), BF16),
                        pltpu.VMEM((HEADS_FOX, 1, tq), F32),
                        pltpu.VMEM((HEADS_FOX, 1, tq), F32),
                        pltpu.VMEM((HEADS_FOX, HEAD_DIM, tq), F32)],
        compiler_params=pltpu.CompilerParams(
            dimension_semantics=("arbitrary", "arbitrary"), vmem_limit_bytes=VMEM_LIMIT),
        name="fox",
    )(fqt, fqct, fk, fkc, fvt)


SB_DEAD_LOG = -110.0

def _sb_kernel(qt_ref, k_ref, vt_ref, o_ref, qm_ref, later_w_ref, run_ref, acc_ref):
    tq = tk = TKC
    i = pl.program_id(1)
    krow = lax.broadcasted_iota(jnp.int32, (tk, tq), 0)
    qcol = lax.broadcasted_iota(jnp.int32, (tk, tq), 1)
    strict = krow < qcol
    r = lax.broadcasted_iota(jnp.int32, (tk, 2 * tk), 0)
    c = lax.broadcasted_iota(jnp.int32, (tk, 2 * tk), 1) % tk
    later_w_ref[...] = jnp.where(c > r, 1.0, 0.0).astype(BF16)
    for h in range(HEADS_SB):
        qm_ref[h] = _pair_rows(qt_ref[0, 0, h * HEAD_DIM:(h + 1) * HEAD_DIM, :], h)
    run_ref[...] = jnp.zeros(run_ref.shape, F32)
    acc_ref[...] = jnp.zeros(acc_ref.shape, F32)

    def block(j, masked):
        ks = pl.multiple_of(j * tk, tk)
        vt = vt_ref[0, j]
        zs = []
        for h in range(HEADS_SB):
            g = h // 2
            width = min(LANE, W_SB - g * LANE)
            kp = k_ref[0, pl.ds(ks, tk), g * LANE:g * LANE + width]
            zs.append(_dot(kp, qm_ref[h][:width, :]))
        lns = []
        for h in range(HEADS_SB):
            ln = _log_sigmoid(-zs[h])
            lns.append(jnp.where(strict, ln, 0.0) if masked else ln)
        laters = []
        for h in range(HEADS_SB):
            ln_hi = lns[h].astype(BF16)
            ln_lo = (lns[h] - ln_hi.astype(F32)).astype(BF16)
            laters.append(_dot(later_w_ref[...], jnp.concatenate([ln_hi, ln_lo], axis=0)))
        for h in range(HEADS_SB):
            run = run_ref[h]
            a = jnp.exp(zs[h] + lns[h] + laters[h] + run)
            if masked:
                a = jnp.where(strict, a, 0.0)
            acc_ref[h] = acc_ref[h] + _dot(vt[h * HEAD_DIM:(h + 1) * HEAD_DIM, :], a.astype(BF16))
            run_ref[h] = run + laters[h][0:1, :] + lns[h][0:1, :]

    def alive():
        return jnp.max(run_ref[...]) > SB_DEAD_LOG

    block(i, True)

    def body(carry):
        t, _ = carry
        block(i - 1 - t, False)
        return t + 1, alive()

    lax.while_loop(lambda carry: (carry[0] < i) & carry[1], body, (jnp.int32(0), alive()))
    o_ref[0] = jnp.concatenate([acc_ref[h].T for h in range(HEADS_SB)], axis=1).astype(o_ref.dtype)


def _sb(sqt, sk, svt):
    b, s, w = sk.shape
    tq = TKC
    return pl.pallas_call(
        _sb_kernel,
        grid=(b, s // tq),
        in_specs=[pl.BlockSpec((1, 1, w, tq), lambda bi, qi: (bi, qi, 0, 0)),
                  pl.BlockSpec((1, s, w), lambda bi, qi: (bi, 0, 0)),
                  pl.BlockSpec((1, s // TKC, w, TKC), lambda bi, qi: (bi, 0, 0, 0))],
        out_specs=pl.BlockSpec((1, tq, w), lambda bi, qi: (bi, qi, 0)),
        out_shape=jax.ShapeDtypeStruct((b, s, w), BF16),
        scratch_shapes=[pltpu.VMEM((HEADS_SB, LANE, tq), BF16),
                        pltpu.VMEM((TKC, 2 * TKC), BF16),
                        pltpu.VMEM((HEADS_SB, 1, tq), F32),
                        pltpu.VMEM((HEADS_SB, HEAD_DIM, tq), F32)],
        compiler_params=pltpu.CompilerParams(
            dimension_semantics=("arbitrary", "arbitrary"), vmem_limit_bytes=VMEM_LIMIT),
        name="sb",
    )(sqt, sk, svt)


NEG_KEY = -0x7149F2CB


def _sortable(score):
    bits = lax.bitcast_convert_type(score, jnp.int32)
    return jnp.where(bits < 0, bits ^ jnp.int32(0x7FFFFFFF), bits)


def _sum_sublane_groups(x):
    parts = [x[r * SUBLANE:(r + 1) * SUBLANE] for r in range(x.shape[0] // SUBLANE)]
    while len(parts) > 1:
        parts = [parts[k] + parts[k + 1] for k in range(0, len(parts) - 1, 2)] + (
            [parts[-1]] if len(parts) % 2 else [])
    return parts[0]


def _count_ge16(ref, nkb, cand):
    tk, tq = ref.shape[1], ref.shape[2]
    cand16 = jnp.broadcast_to(cand, (PACKED_ROWS, tq)).astype(jnp.int16)
    one, zero = jnp.ones((PACKED_ROWS, tq), jnp.int16), jnp.zeros((PACKED_ROWS, tq), jnp.int16)

    def body(j, acc):
        blk = ref[j]
        parts = [jnp.where(blk[r * PACKED_ROWS:(r + 1) * PACKED_ROWS] >= cand16, one, zero)
                 for r in range(tk // PACKED_ROWS)]
        while len(parts) > 1:
            parts = [parts[k] + parts[k + 1] for k in range(0, len(parts), 2)]
        return acc + parts[0]

    acc = lax.fori_loop(0, nkb, body, zero)
    return jnp.sum(acc.astype(F32), axis=0, keepdims=True)


def _search16(ref, nkb, n_sel):
    tq = ref.shape[2]

    def bit_step(t, thr_u):
        bit = lax.shift_left(jnp.int32(1), jnp.int32(15) - t)
        cand_u = thr_u | bit
        cnt = _count_ge16(ref, nkb, cand_u - 32768)
        return jnp.where(cnt >= n_sel, cand_u, thr_u)

    return lax.fori_loop(0, 16, bit_step, jnp.zeros((1, tq), jnp.int32)) - 32768


def _dsa_kernel(dqt_ref, iqt_ref, iwt_ref, dk2_ref, ik2_ref, dvt_ref, o_ref,
                key_ref, hi_ref, lo_ref, bias_ref, iq8_ref, q5_ref, jmax_ref, m_ref, l_ref, acc_ref,
                *, n_sel, idx_bits):
    tq, tk = TQ_DSA, TK_DSA
    i = pl.program_id(1)
    nkb = (i * tq) // tk + 1
    kloc = lax.broadcasted_iota(jnp.int32, (tk, tq), 0)
    qpos = i * tq + lax.broadcasted_iota(jnp.int32, (tk, tq), 1)
    klimit = (qpos // CHUNK + 1) * CHUNK
    for h in range(IDX_HEADS):
        iq8_ref[:, h * tq:(h + 1) * tq] = _pair_rows(iqt_ref[0, 0, h * HEAD_DIM:(h + 1) * HEAD_DIM, :], h)
    for h in range(HEADS_DSA):
        q5_ref[:, h * tq:(h + 1) * tq] = _pair_rows(dqt_ref[0, 0, h * HEAD_DIM:(h + 1) * HEAD_DIM, :], h)
    iwt = iwt_ref[0, 0]

    def score_block(j, _):
        ks = pl.multiple_of(j * tk, tk)
        a = _dot(ik2_ref[0, pl.ds(ks, tk), :], iq8_ref[...])
        sc = jnp.zeros((tk, tq), F32)
        for h in range(IDX_HEADS):
            sc = sc + iwt[h:h + 1, :] * jnp.maximum(a[:, h * tq:(h + 1) * tq], 0.0)
        sc = sc + 0.0
        sc = jnp.where(kloc + ks < klimit, sc, NEG)
        key = _sortable(sc)
        key_ref[j] = key
        hi_ref[j] = lax.shift_right_arithmetic(key, jnp.int32(16)).astype(jnp.int16)
        lo_ref[j] = ((key & jnp.int32(0xFFFF)) - 32768).astype(jnp.int16)
        return 0

    lax.fori_loop(0, nkb, score_block, 0)

    thr_hi = _search16(hi_ref, nkb, n_sel)
    thr_hi16 = jnp.broadcast_to(thr_hi, (PACKED_ROWS, tq)).astype(jnp.int16)

    def low_block(j, _):
        for r in range(tk // PACKED_ROWS):
            rows = slice(r * PACKED_ROWS, (r + 1) * PACKED_ROWS)
            hi = hi_ref[j, rows, :]
            pinned = jnp.where(hi > thr_hi16, jnp.int16(32767), jnp.int16(-32768))
            lo_ref[j, rows, :] = jnp.where(hi == thr_hi16, lo_ref[j, rows, :], pinned)
        return 0

    lax.fori_loop(0, nkb, low_block, 0)
    thr = thr_hi * 65536 + (_search16(lo_ref, nkb, n_sel) + 32768)

    def count(pred):
        def body(j, acc):
            hit = jnp.where(pred(key_ref[j], kloc + j * tk), 1.0, 0.0)
            return acc + _sum_sublane_groups(hit)
        acc = lax.fori_loop(0, nkb, body, jnp.zeros((SUBLANE, tq), F32))
        return jnp.sum(acc, axis=0, keepdims=True)

    cnt_ge = count(lambda k, _: k >= thr)
    jmax_ref[...] = jnp.full((1, tq), 2 ** idx_bits - 1, jnp.int32)
    tied = (cnt_ge > n_sel) & (thr > NEG_KEY)

    @pl.when(jnp.max(jnp.where(tied, 1.0, 0.0)) > 0.0)
    def _():
        need = n_sel - count(lambda k, _: k > thr)

        def idx_bit(t, jmax):
            bit = lax.shift_left(jnp.int32(1), jnp.int32(idx_bits - 1) - t)
            cand = jmax | bit
            cnt = count(lambda k, pos: (k == thr) & (pos < cand))
            return jnp.where(cnt < need, cand, jmax)

        jmax_ref[...] = lax.fori_loop(0, idx_bits, idx_bit, jnp.zeros((1, tq), jnp.int32))

    jmax = jmax_ref[...]

    def bias_block(j, _):
        k = key_ref[j]
        pos = kloc + j * tk
        sel = ((k > thr) | ((k == thr) & (pos <= jmax))) & (pos < klimit)
        bias_ref[j] = jnp.where(sel, 0.0, NEG)
        return 0

    lax.fori_loop(0, nkb, bias_block, 0)

    m_ref[...] = jnp.full(m_ref.shape, NEG, F32)
    l_ref[...] = jnp.zeros(l_ref.shape, F32)
    acc_ref[...] = jnp.zeros(acc_ref.shape, F32)

    def att_block(j, _):
        ks = pl.multiple_of(j * tk, tk)
        s = _dot(dk2_ref[0, pl.ds(ks, tk), :], q5_ref[...])
        s = s + jnp.concatenate([bias_ref[j]] * HEADS_DSA, axis=1)
        m_old = m_ref[...]
        m_new = jnp.maximum(m_old, jnp.max(s, axis=0, keepdims=True))
        alpha = jnp.exp(m_old - m_new)
        p = jnp.exp(s - m_new)
        m_ref[...] = m_new
        l_ref[...] = alpha * l_ref[...] + jnp.sum(p, axis=0, keepdims=True)
        acc_ref[...] = alpha * acc_ref[...] + _dot(dvt_ref[0, j], p.astype(BF16))
        return 0

    lax.fori_loop(0, nkb, att_block, 0)
    out = acc_ref[...] / l_ref[...]
    o_ref[0] = jnp.concatenate([out[:, h * tq:(h + 1) * tq].T for h in range(HEADS_DSA)],
                               axis=1).astype(o_ref.dtype)


def _dsa(dqt, iqt, iwt, dk2, ik2, dvt):
    b, s, _ = dk2.shape
    tq, tk = TQ_DSA, TK_DSA
    n_sel = min(DSA_TOPK_MAX, s // 4)
    assert n_sel <= tk and s % tk == 0
    idx_bits = max(1, (s - 1).bit_length())
    return pl.pallas_call(
        functools.partial(_dsa_kernel, n_sel=n_sel, idx_bits=idx_bits),
        grid=(b, s // tq),
        in_specs=[pl.BlockSpec((1, 1, W_DSA, tq), lambda bi, qi: (bi, qi, 0, 0)),
                  pl.BlockSpec((1, 1, W_IDX, tq), lambda bi, qi: (bi, qi, 0, 0)),
                  pl.BlockSpec((1, 1, IDX_HEADS, tq), lambda bi, qi: (bi, qi, 0, 0)),
                  pl.BlockSpec((1, s, LANE), lambda bi, qi: (bi, 0, 0)),
                  pl.BlockSpec((1, s, LANE), lambda bi, qi: (bi, 0, 0)),
                  pl.BlockSpec((1, s // tk, HEAD_DIM, tk), lambda bi, qi: (bi, 0, 0, 0))],
        out_specs=pl.BlockSpec((1, tq, W_DSA), lambda bi, qi: (bi, qi, 0)),
        out_shape=jax.ShapeDtypeStruct((b, s, W_DSA), BF16),
        scratch_shapes=[pltpu.VMEM((s // tk, tk, tq), jnp.int32),
                        pltpu.VMEM((s // tk, tk, tq), jnp.int16),
                        pltpu.VMEM((s // tk, tk, tq), jnp.int16),
                        pltpu.VMEM((s // tk, tk, tq), F32),
                        pltpu.VMEM((LANE, IDX_HEADS * tq), BF16),
                        pltpu.VMEM((LANE, HEADS_DSA * tq), BF16),
                        pltpu.VMEM((1, tq), jnp.int32),
                        pltpu.VMEM((1, HEADS_DSA * tq), F32),
                        pltpu.VMEM((1, HEADS_DSA * tq), F32),
                        pltpu.VMEM((HEAD_DIM, HEADS_DSA * tq), F32)],
        compiler_params=pltpu.CompilerParams(
            dimension_semantics=("arbitrary", "arbitrary"), vmem_limit_bytes=VMEM_LIMIT),
        name="dsa",
    )(dqt, iqt, iwt, dk2, ik2, dvt)


def _layer_norm(y, g, b):
    mu = jnp.mean(y, axis=1, keepdims=True)
    d = y - mu
    var = jnp.mean(d * d, axis=1, keepdims=True)
    return d * lax.rsqrt(var + LN_EPS) * g + b


def _merge_kernel(x_ref, of_ref, os_ref, od_ref, wg_ref, wuf_ref, wus_ref, wud_ref, wo_ref,
                  g_ref, b_ref, o_ref):
    x = x_ref[...]
    xb = x.astype(BF16)
    merged = None
    for n, (br_ref, wu_ref) in enumerate(((of_ref, wuf_ref), (os_ref, wus_ref), (od_ref, wud_ref))):
        gate = jax.nn.sigmoid(_dot(xb, wg_ref[:, n * D_MODEL:(n + 1) * D_MODEL]))
        term = gate * _dot(br_ref[...], wu_ref[...])
        merged = term if merged is None else merged + term
    y = DEEPNORM_ALPHA * x + _dot(merged.astype(BF16), wo_ref[...])
    o_ref[...] = _layer_norm(y, g_ref[...], b_ref[...])


def _const_spec(shape):
    return pl.BlockSpec(shape, lambda i: (0,) * len(shape), pipeline_mode=pl.Buffered(1))


def _merge(x2, o_fox, o_sb, o_dsa, wg, wuf, wus, wud, wo, g, bvec):
    t = x2.shape[0]
    tm = min(TM_MLP, t)

    def rows(width):
        return pl.BlockSpec((tm, width), lambda i: (i, 0))

    return pl.pallas_call(
        _merge_kernel,
        grid=(t // tm,),
        in_specs=[rows(D_MODEL), rows(W_FOX), rows(W_SB), rows(W_DSA),
                  _const_spec(wg.shape), _const_spec(wuf.shape), _const_spec(wus.shape),
                  _const_spec(wud.shape), _const_spec(wo.shape),
                  _const_spec(g.shape), _const_spec(bvec.shape)],
        out_specs=rows(D_MODEL),
        out_shape=jax.ShapeDtypeStruct((t, D_MODEL), F32),
        compiler_params=pltpu.CompilerParams(
            dimension_semantics=("arbitrary",), vmem_limit_bytes=VMEM_LIMIT),
        name="merge",
    )(x2, o_fox, o_sb, o_dsa, wg, wuf, wus, wud, wo, g, bvec)


FF_CHUNK = 1024


def _ffn_kernel(x_ref, p_ref, w1_ref, w2_ref, wpg_ref, wp_ref, g_ref, b_ref, o_ref):
    x = x_ref[...]
    xb = x.astype(BF16)
    acc = None
    for c in range(D_FF // FF_CHUNK):
        h = jnp.maximum(_dot(xb, w1_ref[:, c * FF_CHUNK:(c + 1) * FF_CHUNK]), 0.0)
        term = _dot((h * h).astype(BF16), w2_ref[c * FF_CHUNK:(c + 1) * FF_CHUNK, :])
        acc = term if acc is None else acc + term
    ple = jax.nn.sigmoid(_dot(xb, wpg_ref[...])) * _dot(p_ref[...].astype(BF16), wp_ref[...])
    y = DEEPNORM_ALPHA * x + acc + ple
    o_ref[...] = _layer_norm(y, g_ref[...], b_ref[...])


def _ffn(x2, p2, w1, w2, wpg, wp, g, bvec):
    t = x2.shape[0]
    tm = min(TM_MLP, t)

    def rows(width):
        return pl.BlockSpec((tm, width), lambda i: (i, 0))

    return pl.pallas_call(
        _ffn_kernel,
        grid=(t // tm,),
        in_specs=[rows(D_MODEL), rows(PLE_DIM),
                  _const_spec(w1.shape), _const_spec(w2.shape), _const_spec(wpg.shape),
                  _const_spec(wp.shape), _const_spec(g.shape), _const_spec(bvec.shape)],
        out_specs=rows(D_MODEL),
        out_shape=jax.ShapeDtypeStruct((t, D_MODEL), F32),
        compiler_params=pltpu.CompilerParams(
            dimension_semantics=("arbitrary",), vmem_limit_bytes=VMEM_LIMIT),
        name="ffn",
    )(x2, p2, w1, w2, wpg, wp, g, bvec)


def _pack_w_in(w):
    d = w.shape[0]
    offs = {}
    o = 0
    for name, width in (("fq", W_FOX), ("fk", W_FOX), ("fv", W_FOX), ("ff", HEADS_FOX),
                        ("sq", W_SB), ("sk", W_SB), ("sv", W_SB),
                        ("dq", W_DSA), ("dk", HEAD_DIM), ("dv", HEAD_DIM),
                        ("iq", W_IDX), ("ik", HEAD_DIM), ("iw", IDX_HEADS),
                        ("g", N_BRANCH * D_MODEL)):
        offs[name] = w[:, o:o + width]
        o += width

    def z(n):
        return jnp.zeros((d, n), w.dtype)

    packed = jnp.concatenate(
        [offs["fk"], offs["sk"], z(W_SB_PAD - W_SB), offs["dk"], offs["dk"], offs["ik"], offs["ik"],
         offs["ff"], z(IW_LANE - HEADS_FOX), offs["iw"], z(LANE - IW_LANE - IDX_HEADS)], axis=1)
    assert packed.shape[1] == C_PACK
    packed_t = jnp.concatenate([offs["fq"], offs["sq"], offs["dq"], offs["iq"],
                                offs["fv"], offs["sv"], offs["dv"]], axis=1).T
    assert packed_t.shape[0] == R_PACK
    return packed.astype(BF16), packed_t.astype(BF16), offs["g"].astype(BF16)


def _rope_tables(s):
    inv = ROPE_THETA ** (-jnp.arange(HALF, dtype=F32) / HALF)
    ang = jnp.arange(s, dtype=F32)[:, None] * inv[None, :]
    cos, sin = jnp.cos(ang), jnp.sin(ang)
    row_major = jnp.concatenate([cos, cos, cos, cos, -sin, sin, -sin, sin], axis=1)
    transposed = jnp.concatenate([cos, cos, -sin, sin], axis=1).T
    return row_major, transposed


def _placement():
    rows_q, cols_q, rows_k, cols_k = [], [], [], []
    for h in range(HEADS_FOX):
        for t in range(N_CTERM):
            rows_q.append(t * LANE + h), cols_q.append(CSTRIDE * h + t)
            rows_k.append(t * LANE + h), cols_k.append(LANE + CSTRIDE * h + N_CTERM + t)
    place = jnp.zeros((N_CTERM * LANE, 2 * LANE), F32)
    place = place.at[jnp.array(rows_q), jnp.array(cols_q)].set(1.0)
    place = place.at[jnp.array(rows_k), jnp.array(cols_k)].set(-1.0)
    lane = jnp.arange(2 * LANE)
    head_lane = (lane % LANE) % CSTRIDE
    in_heads = (lane % LANE) < CSTRIDE * HEADS_FOX
    ones_q = (lane < LANE) & in_heads & (head_lane >= N_CTERM) & (head_lane < 2 * N_CTERM)
    ones_k = (lane >= LANE) & in_heads & (head_lane < N_CTERM)
    ones_row = jnp.where(ones_q | ones_k, 1.0, 0.0).astype(F32)[None, :]
    return place.astype(BF16), ones_row


def kernel(x, p, w_in, b_forget, w_up_fox, w_up_sb, w_up_dsa, w_out, ln1_g, ln1_b,
           w_ff_in, w_ff_out, w_ple, w_ple_gate, ln2_g, ln2_b):
    b, s, d = x.shape
    depth = w_in.shape[0]
    rope_tab, rope_tab_t = _rope_tables(s)
    place, ones_row = _placement()
    w_in_rows = w_in.reshape(depth * d, w_in.shape[2])
    for i in range(depth):
        w_pack, wt_pack, w_gate = _pack_w_in(w_in_rows[i * d:(i + 1) * d])
        bias_row = jnp.zeros((1, LANE), F32).at[0, :HEADS_FOX].set(b_forget[i].astype(F32))
        (fqt, fqct, fk, fkc, fvt, sqt, sk, svt, dqt, iqt, dk2, ik2, dvt, iwt) = _proj(
            x, w_pack, wt_pack, bias_row, rope_tab, rope_tab_t, place, ones_row)
        o_fox = _fox(fqt, fqct, fk, fkc, fvt)
        o_sb = _sb(sqt, sk, svt)
        o_dsa = _dsa(dqt, iqt, iwt, dk2, ik2, dvt)
        x1 = _merge(x.reshape(b * s, d), o_fox.reshape(b * s, W_FOX), o_sb.reshape(b * s, W_SB),
                    o_dsa.reshape(b * s, W_DSA), w_gate,
                    w_up_fox[i].astype(BF16), w_up_sb[i].astype(BF16), w_up_dsa[i].astype(BF16),
                    w_out[i].astype(BF16), ln1_g[i][None, :], ln1_b[i][None, :])
        x2 = _ffn(x1, p[i].reshape(b * s, PLE_DIM), w_ff_in[i].astype(BF16), w_ff_out[i].astype(BF16),
                  w_ple_gate[i].astype(BF16), w_ple[i].astype(BF16), ln2_g[i][None, :], ln2_b[i][None, :])
        x = x2.reshape(b, s, d)
    return x
```

```python
import functools

import jax
import jax.numpy as jnp
from jax import lax
from jax.experimental import pallas as pl
from jax.experimental.pallas import tpu as pltpu

D_MODEL = 1024
HEAD_DIM = 64
HALF = HEAD_DIM // 2
HEADS_FOX = 6
HEADS_SB = 5
HEADS_DSA = 5
IDX_HEADS = 8
CHUNK = 64
DSA_TOPK_MAX = 256
D_FF = 4 * D_MODEL
PLE_DIM = 256
ROPE_THETA = 10000.0
LN_EPS = 1e-5
N_BRANCH = 3
NEG = -1e30
DEPTH = 2
DEEPNORM_ALPHA = (2 * DEPTH) ** 0.25
W_FOX = HEADS_FOX * HEAD_DIM
W_SB = HEADS_SB * HEAD_DIM
W_DSA = HEADS_DSA * HEAD_DIM
W_IDX = IDX_HEADS * HEAD_DIM
SCALE = HEAD_DIM ** -0.5

LANE = 128
SUBLANE = 8
PACKED_ROWS = 16
W_SB_PAD = 384
IW_LANE = 8
N_CTERM = 3
VMEM_LIMIT = 56 * 1024 * 1024

TM_PROJ = 512
TKC = 256
TQ_DSA = 256
TK_DSA = TKC
TM_MLP = 512

BF16 = jnp.bfloat16
F32 = jnp.float32


def _log_sigmoid(x):
    return jnp.minimum(x, 0.0) - jnp.log(1.0 + jnp.exp(-jnp.abs(x)))


def _dot(a, b):
    return jnp.dot(a, b, preferred_element_type=F32)


def _dot_nt(a, b):
    return lax.dot_general(a, b, (((1,), (1,)), ((), ())), preferred_element_type=F32)


def _split3(v):
    hi = v.astype(BF16)
    r = v - hi.astype(F32)
    mid = r.astype(BF16)
    lo = (r - mid.astype(F32)).astype(BF16)
    return hi, mid, lo


def _pair_rows(qt_h, h):
    z = jnp.zeros_like(qt_h)
    return jnp.concatenate([z, qt_h] if h % 2 else [qt_h, z], axis=0)


OFF_FK = 0
OFF_SK = 384
OFF_DK2 = 768
OFF_IK2 = 896
OFF_SM = 1024
C_PACK = 1152
OFF_TFQ, OFF_TSQ, OFF_TDQ, OFF_TIQ = 0, 384, 704, 1024
OFF_TFV, OFF_TSV, OFF_TDV = 1536, 1920, 2240
R_PACK = 2304
CSTRIDE = PACKED_ROWS


def _swap_halves(h):
    lane = lax.broadcasted_iota(jnp.int32, h.shape, 1)
    first = (lane % HEAD_DIM) < HALF
    return jnp.where(first, pltpu.roll(h, LANE - HALF, 1), pltpu.roll(h, HALF, 1))


def _rope_rows(h, cos, sin):
    return h * cos + _swap_halves(h) * sin


def _rope_t(ht, cos_t, sin_t):
    parts = []
    for h in range(ht.shape[0] // HEAD_DIM):
        x = ht[h * HEAD_DIM:(h + 1) * HEAD_DIM]
        swapped = jnp.concatenate([x[HALF:], x[:HALF]], axis=0)
        parts.append(x * cos_t + swapped * sin_t)
    return jnp.concatenate(parts, axis=0)


def _proj_kernel(x_ref, w_ref, wt_ref, bias_ref, tab_ref, tabt_ref, place_ref, ones_ref,
                 fqt_ref, fqct_ref, fk_ref, fkc_ref, fvt_ref, sqt_ref, sk_ref, svt_ref,
                 dqt_ref, iqt_ref, dk2_ref, ik2_ref, dvt_ref, iwt_ref, carry_ref):
    tm = x_ref.shape[1]
    xb = x_ref[0].astype(BF16)

    def seg(off, width):
        return _dot(xb, w_ref[:, off:off + width])

    def seg_t(off, nrows):
        return _dot_nt(wt_ref[off:off + nrows, :], xb)

    def put_chunks(ref, val, chunk):
        for r in range(tm // chunk):
            ref[0, r] = val[:, r * chunk:(r + 1) * chunk].astype(ref.dtype)

    fk_ref[0] = seg(OFF_FK, W_FOX).astype(BF16)
    sk_ref[0] = seg(OFF_SK, W_SB_PAD)[:, :W_SB].astype(BF16)
    cos, sin = tab_ref[:, 0:LANE], tab_ref[:, LANE:2 * LANE]
    dk2_ref[0] = _rope_rows(seg(OFF_DK2, LANE), cos, sin).astype(BF16)
    ik2_ref[0] = _rope_rows(seg(OFF_IK2, LANE), cos, sin).astype(BF16)

    cos_t, sin_t = tabt_ref[0:HEAD_DIM, :], tabt_ref[HEAD_DIM:2 * HEAD_DIM, :]
    put_chunks(fqt_ref, seg_t(OFF_TFQ, W_FOX) * SCALE, TKC)
    put_chunks(sqt_ref, seg_t(OFF_TSQ, W_SB) * SCALE, TKC)
    put_chunks(dqt_ref, _rope_t(seg_t(OFF_TDQ, W_DSA), cos_t, sin_t) * SCALE, TQ_DSA)
    put_chunks(iqt_ref, _rope_t(seg_t(OFF_TIQ, W_IDX), cos_t, sin_t), TQ_DSA)
    put_chunks(fvt_ref, seg_t(OFF_TFV, W_FOX), TKC)
    put_chunks(svt_ref, seg_t(OFF_TSV, W_SB), TKC)
    put_chunks(dvt_ref, seg_t(OFF_TDV, HEAD_DIM), TKC)

    small = seg(OFF_SM, LANE)
    logf = _log_sigmoid(small + bias_ref[...])

    @pl.when(pl.program_id(1) == 0)
    def _():
        carry_ref[...] = jnp.zeros_like(carry_ref)

    row = lax.broadcasted_iota(jnp.int32, (tm, tm), 0)
    col = lax.broadcasted_iota(jnp.int32, (tm, tm), 1)
    tri = jnp.where(col <= row, 1.0, 0.0).astype(BF16)
    c = _dot(tri, jnp.concatenate(_split3(logf), axis=1))
    c = c[:, 0:LANE] + c[:, LANE:2 * LANE] + c[:, 2 * LANE:3 * LANE] + carry_ref[...]
    carry_ref[...] = c[tm - 1:tm, :]
    placed = _dot(jnp.concatenate(_split3(c), axis=1), place_ref[...]) + ones_ref[...]
    put_chunks(fqct_ref, placed[:, 0:LANE].T, TKC)
    fkc_ref[0] = placed[:, LANE:2 * LANE].astype(BF16)
    put_chunks(iwt_ref, small.T[IW_LANE:IW_LANE + IDX_HEADS], TQ_DSA)


def _proj(x, w_pack, wt_pack, bias_row, rope_tab, rope_tab_t, place, ones_row):
    b, s, d = x.shape
    tm = min(TM_PROJ, s)
    grid = (b, s // tm)

    def rows(width, dtype=BF16):
        return (jax.ShapeDtypeStruct((b, s, width), dtype),
                pl.BlockSpec((1, tm, width), lambda bi, si: (bi, si, 0)))

    def chunks(nrows, chunk, dtype=BF16):
        return (jax.ShapeDtypeStruct((b, s // chunk, nrows, chunk), dtype),
                pl.BlockSpec((1, tm // chunk, nrows, chunk), lambda bi, si: (bi, si, 0, 0)))

    def const(shape):
        return pl.BlockSpec(shape, lambda bi, si: (0,) * len(shape))

    outs = [chunks(W_FOX, TKC), chunks(LANE, TKC), rows(W_FOX), rows(LANE), chunks(W_FOX, TKC),
            chunks(W_SB, TKC), rows(W_SB), chunks(W_SB, TKC),
            chunks(W_DSA, TQ_DSA), chunks(W_IDX, TQ_DSA), rows(LANE), rows(LANE), chunks(HEAD_DIM, TKC),
            chunks(IDX_HEADS, TQ_DSA, F32)]
    return pl.pallas_call(
        _proj_kernel,
        grid=grid,
        in_specs=[pl.BlockSpec((1, tm, d), lambda bi, si: (bi, si, 0)),
                  const((d, C_PACK)), const((R_PACK, d)), const((1, LANE)),
                  pl.BlockSpec((tm, 2 * LANE), lambda bi, si: (si, 0)),
                  pl.BlockSpec((2 * HEAD_DIM, tm), lambda bi, si: (0, si)),
                  const(place.shape), const(ones_row.shape)],
        out_specs=[o[1] for o in outs],
        out_shape=[o[0] for o in outs],
        scratch_shapes=[pltpu.VMEM((1, LANE), F32)],
        compiler_params=pltpu.CompilerParams(
            dimension_semantics=("arbitrary", "arbitrary"), vmem_limit_bytes=VMEM_LIMIT),
        name="proj",
    )(x, w_pack, wt_pack, bias_row, rope_tab, rope_tab_t, place, ones_row)


def _fox_kernel(qt_ref, qct_ref, k_ref, kc_ref, vt_ref, o_ref, qcat_ref, m_ref, l_ref, acc_ref):
    tq = tk = TKC
    i = pl.program_id(1)
    krow = lax.broadcasted_iota(jnp.int32, (tk, tq), 0)
    qcol = lax.broadcasted_iota(jnp.int32, (tk, tq), 1)
    causal = krow <= qcol

    for h in range(HEADS_FOX):
        gate_rows = jnp.concatenate(
            ([jnp.zeros((CSTRIDE * h, tq), BF16)] if h else [])
            + [qct_ref[0, 0, CSTRIDE * h:CSTRIDE * (h + 1), :],
               jnp.zeros((LANE - CSTRIDE * (h + 1), tq), BF16)], axis=0)
        qcat_ref[h] = jnp.concatenate(
            [_pair_rows(qt_ref[0, 0, h * HEAD_DIM:(h + 1) * HEAD_DIM, :], h), gate_rows], axis=0)
    m_ref[...] = jnp.full(m_ref.shape, NEG, F32)
    l_ref[...] = jnp.zeros(l_ref.shape, F32)
    acc_ref[...] = jnp.zeros(acc_ref.shape, F32)

    def blocks(js, masked):
        scores = []
        for j in js:
            ks = pl.multiple_of(j * tk, tk)
            kc = kc_ref[0, pl.ds(ks, tk), :]
            per_head = []
            for g in range(HEADS_FOX // 2):
                kcat = jnp.concatenate([k_ref[0, pl.ds(ks, tk), g * LANE:(g + 1) * LANE], kc], axis=1)
                per_head += [_dot(kcat, qcat_ref[h]) for h in (2 * g, 2 * g + 1)]
            scores.append(per_head)
        vts = [vt_ref[0, j] for j in js]
        for h in range(HEADS_FOX):
            ss = [jnp.where(causal, sc[h], NEG) if masked else sc[h] for sc in scores]
            m_old = m_ref[h]
            m_new = m_old
            for s in ss:
                m_new = jnp.maximum(m_new, jnp.max(s, axis=0, keepdims=True))
            alpha = jnp.exp(m_old - m_new)
            ps = [jnp.exp(s - m_new) for s in ss]
            m_ref[h] = m_new
            l_ref[h] = alpha * l_ref[h] + sum(jnp.sum(p, axis=0, keepdims=True) for p in ps)
            acc_ref[h] = alpha * acc_ref[h] + sum(
                _dot(vt[h * HEAD_DIM:(h + 1) * HEAD_DIM, :], p.astype(BF16)) for vt, p in zip(vts, ps))

    blocks((i,), True)

    def body(t, _):
        blocks((2 * t, 2 * t + 1), False)
        return 0

    lax.fori_loop(0, lax.shift_right_logical(i, 1), body, 0)

    @pl.when((i & 1) == 1)
    def _():
        blocks((i - 1,), False)

    o_ref[0] = jnp.concatenate([(acc_ref[h] / l_ref[h]).T for h in range(HEADS_FOX)],
                               axis=1).astype(o_ref.dtype)


def _fox(fqt, fqct, fk, fkc, fvt):
    b, s, w = fk.shape
    tq = TKC
    return pl.pallas_call(
        _fox_kernel,
        grid=(b, s // tq),
        in_specs=[pl.BlockSpec((1, 1, w, tq), lambda bi, qi: (bi, qi, 0, 0)),
                  pl.BlockSpec((1, 1, LANE, tq), lambda bi, qi: (bi, qi, 0, 0)),
                  pl.BlockSpec((1, s, w), lambda bi, qi: (bi, 0, 0)),
                  pl.BlockSpec((1, s, LANE), lambda bi, qi: (bi, 0, 0)),
                  pl.BlockSpec((1, s // TKC, w, TKC), lambda bi, qi: (bi, 0, 0, 0))],
        out_specs=pl.BlockSpec((1, tq, w), lambda bi, qi: (bi, qi, 0)),
        out_shape=jax.ShapeDtypeStruct((b, s, w), BF16),
        scratch_shapes=[pltpu.VMEM((HEADS_FOX, 2 * LANE, tq), BF16),
                        pltpu.VMEM((HEADS_FOX, 1, tq), F32),
                        pltpu.VMEM((HEADS_FOX, 1, tq), F32),
                        pltpu.VMEM((HEADS_FOX, HEAD_DIM, tq), F32)],
        compiler_params=pltpu.CompilerParams(
            dimension_semantics=("arbitrary", "arbitrary"), vmem_limit_bytes=VMEM_LIMIT),
        name="fox",
    )(fqt, fqct, fk, fkc, fvt)


SB_DEAD_LOG = -110.0

def _sb_kernel(qt_ref, k_ref, vt_ref, o_ref, qm_ref, later_w_ref, run_ref, acc_ref):
    tq = tk = TKC
    i = pl.program_id(1)
    krow = lax.broadcasted_iota(jnp.int32, (tk, tq), 0)
    qcol = lax.broadcasted_iota(jnp.int32, (tk, tq), 1)
    strict = krow < qcol
    r = lax.broadcasted_iota(jnp.int32, (tk, 2 * tk), 0)
    c = lax.broadcasted_iota(jnp.int32, (tk, 2 * tk), 1) % tk
    later_w_ref[...] = jnp.where(c > r, 1.0, 0.0).astype(BF16)
    for h in range(HEADS_SB):
        qm_ref[h] = _pair_rows(qt_ref[0, 0, h * HEAD_DIM:(h + 1) * HEAD_DIM, :], h)
    run_ref[...] = jnp.zeros(run_ref.shape, F32)
    acc_ref[...] = jnp.zeros(acc_ref.shape, F32)

    def block(j, masked):
        ks = pl.multiple_of(j * tk, tk)
        vt = vt_ref[0, j]
        zs = []
        for h in range(HEADS_SB):
            g = h // 2
            width = min(LANE, W_SB - g * LANE)
            kp = k_ref[0, pl.ds(ks, tk), g * LANE:g * LANE + width]
            zs.append(_dot(kp, qm_ref[h][:width, :]))
        lns = []
        for h in range(HEADS_SB):
            ln = _log_sigmoid(-zs[h])
            lns.append(jnp.where(strict, ln, 0.0) if masked else ln)
        laters = []
        for h in range(HEADS_SB):
            ln_hi = lns[h].astype(BF16)
            ln_lo = (lns[h] - ln_hi.astype(F32)).astype(BF16)
            laters.append(_dot(later_w_ref[...], jnp.concatenate([ln_hi, ln_lo], axis=0)))
        for h in range(HEADS_SB):
            run = run_ref[h]
            a = jnp.exp(zs[h] + lns[h] + laters[h] + run)
            if masked:
                a = jnp.where(strict, a, 0.0)
            acc_ref[h] = acc_ref[h] + _dot(vt[h * HEAD_DIM:(h + 1) * HEAD_DIM, :], a.astype(BF16))
            run_ref[h] = run + laters[h][0:1, :] + lns[h][0:1, :]

    def alive():
        return jnp.max(run_ref[...]) > SB_DEAD_LOG

    block(i, True)

    def body(carry):
        t, _ = carry
        block(i - 1 - t, False)
        return t + 1, alive()

    lax.while_loop(lambda carry: (carry[0] < i) & carry[1], body, (jnp.int32(0), alive()))
    o_ref[0] = jnp.concatenate([acc_ref[h].T for h in range(HEADS_SB)], axis=1).astype(o_ref.dtype)


def _sb(sqt, sk, svt):
    b, s, w = sk.shape
    tq = TKC
    return pl.pallas_call(
        _sb_kernel,
        grid=(b, s // tq),
        in_specs=[pl.BlockSpec((1, 1, w, tq), lambda bi, qi: (bi, qi, 0, 0)),
                  pl.BlockSpec((1, s, w), lambda bi, qi: (bi, 0, 0)),
                  pl.BlockSpec((1, s // TKC, w, TKC), lambda bi, qi: (bi, 0, 0, 0))],
        out_specs=pl.BlockSpec((1, tq, w), lambda bi, qi: (bi, qi, 0)),
        out_shape=jax.ShapeDtypeStruct((b, s, w), BF16),
        scratch_shapes=[pltpu.VMEM((HEADS_SB, LANE, tq), BF16),
                        pltpu.VMEM((TKC, 2 * TKC), BF16),
                        pltpu.VMEM((HEADS_SB, 1, tq), F32),
                        pltpu.VMEM((HEADS_SB, HEAD_DIM, tq), F32)],
        compiler_params=pltpu.CompilerParams(
            dimension_semantics=("arbitrary", "arbitrary"), vmem_limit_bytes=VMEM_LIMIT),
        name="sb",
    )(sqt, sk, svt)


NEG_KEY = -0x7149F2CB


def _sortable(score):
    bits = lax.bitcast_convert_type(score, jnp.int32)
    return jnp.where(bits < 0, bits ^ jnp.int32(0x7FFFFFFF), bits)


def _sum_sublane_groups(x):
    parts = [x[r * SUBLANE:(r + 1) * SUBLANE] for r in range(x.shape[0] // SUBLANE)]
    while len(parts) > 1:
        parts = [parts[k] + parts[k + 1] for k in range(0, len(parts) - 1, 2)] + (
            [parts[-1]] if len(parts) % 2 else [])
    return parts[0]


def _pair_tree_sum(parts):
    while len(parts) > 1:
        parts = [parts[k] + parts[k + 1] for k in range(0, len(parts), 2)]
    return parts[0]


def _count_ge16(ref, npair, cand):
    tk, tq = ref.shape[1], ref.shape[2]
    cand16 = jnp.broadcast_to(cand, (PACKED_ROWS, tq)).astype(jnp.int16)
    one, zero = jnp.ones((PACKED_ROWS, tq), jnp.int16), jnp.zeros((PACKED_ROWS, tq), jnp.int16)

    def body(t, acc):
        parts = []
        for u in range(2):
            blk = ref[2 * t + u]
            parts += [jnp.where(blk[r * PACKED_ROWS:(r + 1) * PACKED_ROWS] >= cand16, one, zero)
                      for r in range(tk // PACKED_ROWS)]
        return acc + _pair_tree_sum(parts)

    acc = lax.fori_loop(0, npair, body, zero)
    return jnp.sum(acc.astype(F32), axis=0, keepdims=True)


def _search16(ref, npair, n_sel):
    tq = ref.shape[2]

    def bit_step(t, thr_u):
        bit = lax.shift_left(jnp.int32(1), jnp.int32(15) - t)
        cand_u = thr_u | bit
        cnt = _count_ge16(ref, npair, cand_u - 32768)
        return jnp.where(cnt >= n_sel, cand_u, thr_u)

    return lax.fori_loop(0, 16, bit_step, jnp.zeros((1, tq), jnp.int32)) - 32768


def _dsa_kernel(dqt_ref, iqt_ref, iwt_ref, dk2_ref, ik2_ref, dvt_ref, o_ref,
                key_ref, hi_ref, lo_ref, bias_ref, iq8_ref, q5_ref, jmax_ref, m_ref, l_ref, acc_ref,
                *, n_sel, idx_bits):
    tq, tk = TQ_DSA, TK_DSA
    i = pl.program_id(1)
    nkb = (i * tq + tq - 1) // tk + 1
    npair = (nkb + 1) // 2
    kloc = lax.broadcasted_iota(jnp.int32, (tk, tq), 0)
    qpos = i * tq + lax.broadcasted_iota(jnp.int32, (tk, tq), 1)
    klimit = (qpos // CHUNK + 1) * CHUNK
    for h in range(IDX_HEADS):
        iq8_ref[:, h * tq:(h + 1) * tq] = _pair_rows(iqt_ref[0, 0, h * HEAD_DIM:(h + 1) * HEAD_DIM, :], h)
    for h in range(HEADS_DSA):
        q5_ref[:, h * tq:(h + 1) * tq] = _pair_rows(dqt_ref[0, 0, h * HEAD_DIM:(h + 1) * HEAD_DIM, :], h)
    iwt = iwt_ref[0, 0]

    def score_pair(t, _):
        blocks = (2 * t, 2 * t + 1)
        logits = [_dot(ik2_ref[0, pl.ds(pl.multiple_of(j * tk, tk), tk), :], iq8_ref[...])
                  for j in blocks]
        for j, a in zip(blocks, logits):
            sc = jnp.zeros((tk, tq), F32)
            for h in range(IDX_HEADS):
                sc = sc + iwt[h:h + 1, :] * jnp.maximum(a[:, h * tq:(h + 1) * tq], 0.0)
            sc = sc + 0.0
            sc = jnp.where(kloc + j * tk < klimit, sc, NEG)
            key = _sortable(sc)
            key_ref[j] = key
            hi_ref[j] = lax.shift_right_arithmetic(key, jnp.int32(16)).astype(jnp.int16)
            lo_ref[j] = ((key & jnp.int32(0xFFFF)) - 32768).astype(jnp.int16)
        return 0

    lax.fori_loop(0, npair, score_pair, 0)

    thr_hi = _search16(hi_ref, npair, n_sel)
    thr_hi16 = jnp.broadcast_to(thr_hi, (PACKED_ROWS, tq)).astype(jnp.int16)

    def low_pair(t, _):
        for j in (2 * t, 2 * t + 1):
            for r in range(tk // PACKED_ROWS):
                rows = slice(r * PACKED_ROWS, (r + 1) * PACKED_ROWS)
                hi = hi_ref[j, rows, :]
                pinned = jnp.where(hi > thr_hi16, jnp.int16(32767), jnp.int16(-32768))
                lo_ref[j, rows, :] = jnp.where(hi == thr_hi16, lo_ref[j, rows, :], pinned)
        return 0

    lax.fori_loop(0, npair, low_pair, 0)
    thr = thr_hi * 65536 + (_search16(lo_ref, npair, n_sel) + 32768)

    def count(pred):
        def body(t, acc):
            for j in (2 * t, 2 * t + 1):
                hit = jnp.where(pred(key_ref[j], kloc + j * tk), 1.0, 0.0)
                acc = acc + _sum_sublane_groups(hit)
            return acc
        acc = lax.fori_loop(0, npair, body, jnp.zeros((SUBLANE, tq), F32))
        return jnp.sum(acc, axis=0, keepdims=True)

    cnt_ge = count(lambda k, _: k >= thr)
    jmax_ref[...] = jnp.full((1, tq), 2 ** idx_bits - 1, jnp.int32)
    tied = (cnt_ge > n_sel) & (thr > NEG_KEY)

    @pl.when(jnp.max(jnp.where(tied, 1.0, 0.0)) > 0.0)
    def _():
        need = n_sel - count(lambda k, _: k > thr)

        def idx_bit(t, jmax):
            bit = lax.shift_left(jnp.int32(1), jnp.int32(idx_bits - 1) - t)
            cand = jmax | bit
            cnt = count(lambda k, pos: (k == thr) & (pos < cand))
            return jnp.where(cnt < need, cand, jmax)

        jmax_ref[...] = lax.fori_loop(0, idx_bits, idx_bit, jnp.zeros((1, tq), jnp.int32))

    jmax = jmax_ref[...]

    def bias_pair(t, _):
        for j in (2 * t, 2 * t + 1):
            k = key_ref[j]
            pos = kloc + j * tk
            sel = ((k > thr) | ((k == thr) & (pos <= jmax))) & (pos < klimit)
            bias_ref[j] = jnp.where(sel, 0.0, NEG)
        return 0

    lax.fori_loop(0, npair, bias_pair, 0)

    m_ref[...] = jnp.full(m_ref.shape, NEG, F32)
    l_ref[...] = jnp.zeros(l_ref.shape, F32)
    acc_ref[...] = jnp.zeros(acc_ref.shape, F32)

    def att_pair(t, _):
        blocks = (2 * t, 2 * t + 1)
        scores = [_dot(dk2_ref[0, pl.ds(pl.multiple_of(j * tk, tk), tk), :], q5_ref[...])
                  + jnp.concatenate([bias_ref[j]] * HEADS_DSA, axis=1)
                  for j in blocks]
        m_old = m_ref[...]
        m_new = jnp.maximum(m_old, jnp.maximum(jnp.max(scores[0], axis=0, keepdims=True),
                                               jnp.max(scores[1], axis=0, keepdims=True)))
        alpha = jnp.exp(m_old - m_new)
        m_ref[...] = m_new
        p = [jnp.exp(s - m_new) for s in scores]
        l_ref[...] = alpha * l_ref[...] + (jnp.sum(p[0], axis=0, keepdims=True)
                                           + jnp.sum(p[1], axis=0, keepdims=True))
        acc_ref[...] = alpha * acc_ref[...] + (_dot(dvt_ref[0, blocks[0]], p[0].astype(BF16))
                                               + _dot(dvt_ref[0, blocks[1]], p[1].astype(BF16)))
        return 0

    lax.fori_loop(0, npair, att_pair, 0)
    out = acc_ref[...] / l_ref[...]
    o_ref[0] = jnp.concatenate([out[:, h * tq:(h + 1) * tq].T for h in range(HEADS_DSA)],
                               axis=1).astype(o_ref.dtype)


def _dsa(dqt, iqt, iwt, dk2, ik2, dvt):
    b, s, _ = dk2.shape
    tq, tk = TQ_DSA, TK_DSA
    n_sel = min(DSA_TOPK_MAX, s // 4)
    assert n_sel <= tk and s % (2 * tk) == 0 and tk % tq == 0
    idx_bits = max(1, (s - 1).bit_length())
    return pl.pallas_call(
        functools.partial(_dsa_kernel, n_sel=n_sel, idx_bits=idx_bits),
        grid=(b, s // tq),
        in_specs=[pl.BlockSpec((1, 1, W_DSA, tq), lambda bi, qi: (bi, qi, 0, 0)),
                  pl.BlockSpec((1, 1, W_IDX, tq), lambda bi, qi: (bi, qi, 0, 0)),
                  pl.BlockSpec((1, 1, IDX_HEADS, tq), lambda bi, qi: (bi, qi, 0, 0)),
                  pl.BlockSpec((1, s, LANE), lambda bi, qi: (bi, 0, 0)),
                  pl.BlockSpec((1, s, LANE), lambda bi, qi: (bi, 0, 0)),
                  pl.BlockSpec((1, s // tk, HEAD_DIM, tk), lambda bi, qi: (bi, 0, 0, 0))],
        out_specs=pl.BlockSpec((1, tq, W_DSA), lambda bi, qi: (bi, qi, 0)),
        out_shape=jax.ShapeDtypeStruct((b, s, W_DSA), BF16),
        scratch_shapes=[pltpu.VMEM((s // tk, tk, tq), jnp.int32),
                        pltpu.VMEM((s // tk, tk, tq), jnp.int16),
                        pltpu.VMEM((s // tk, tk, tq), jnp.int16),
                        pltpu.VMEM((s // tk, tk, tq), F32),
                        pltpu.VMEM((LANE, IDX_HEADS * tq), BF16),
                        pltpu.VMEM((LANE, HEADS_DSA * tq), BF16),
                        pltpu.VMEM((1, tq), jnp.int32),
                        pltpu.VMEM((1, HEADS_DSA * tq), F32),
                        pltpu.VMEM((1, HEADS_DSA * tq), F32),
                        pltpu.VMEM((HEAD_DIM, HEADS_DSA * tq), F32)],
        compiler_params=pltpu.CompilerParams(
            dimension_semantics=("arbitrary", "arbitrary"), vmem_limit_bytes=VMEM_LIMIT),
        name="dsa",
    )(dqt, iqt, iwt, dk2, ik2, dvt)


def _layer_norm(y, g, b):
    mu = jnp.mean(y, axis=1, keepdims=True)
    d = y - mu
    var = jnp.mean(d * d, axis=1, keepdims=True)
    return d * lax.rsqrt(var + LN_EPS) * g + b


def _merge_kernel(x_ref, of_ref, os_ref, od_ref, wg_ref, wuf_ref, wus_ref, wud_ref, wo_ref,
                  g_ref, b_ref, o_ref):
    x = x_ref[...]
    xb = x.astype(BF16)
    merged = None
    for n, (br_ref, wu_ref) in enumerate(((of_ref, wuf_ref), (os_ref, wus_ref), (od_ref, wud_ref))):
        gate = jax.nn.sigmoid(_dot(xb, wg_ref[:, n * D_MODEL:(n + 1) * D_MODEL]))
        term = gate * _dot(br_ref[...], wu_ref[...])
        merged = term if merged is None else merged + term
    y = DEEPNORM_ALPHA * x + _dot(merged.astype(BF16), wo_ref[...])
    o_ref[...] = _layer_norm(y, g_ref[...], b_ref[...])


def _const_spec(shape):
    return pl.BlockSpec(shape, lambda i: (0,) * len(shape), pipeline_mode=pl.Buffered(1))


def _merge(x2, o_fox, o_sb, o_dsa, wg, wuf, wus, wud, wo, g, bvec):
    t = x2.shape[0]
    tm = min(TM_MLP, t)

    def rows(width):
        return pl.BlockSpec((tm, width), lambda i: (i, 0))

    return pl.pallas_call(
        _merge_kernel,
        grid=(t // tm,),
        in_specs=[rows(D_MODEL), rows(W_FOX), rows(W_SB), rows(W_DSA),
                  _const_spec(wg.shape), _const_spec(wuf.shape), _const_spec(wus.shape),
                  _const_spec(wud.shape), _const_spec(wo.shape),
                  _const_spec(g.shape), _const_spec(bvec.shape)],
        out_specs=rows(D_MODEL),
        out_shape=jax.ShapeDtypeStruct((t, D_MODEL), F32),
        compiler_params=pltpu.CompilerParams(
            dimension_semantics=("arbitrary",), vmem_limit_bytes=VMEM_LIMIT),
        name="merge",
    )(x2, o_fox, o_sb, o_dsa, wg, wuf, wus, wud, wo, g, bvec)


FF_CHUNK = 1024


def _ffn_kernel(x_ref, p_ref, w1_ref, w2_ref, wpg_ref, wp_ref, g_ref, b_ref, o_ref):
    x = x_ref[...]
    xb = x.astype(BF16)
    acc = None
    for c in range(D_FF // FF_CHUNK):
        h = jnp.maximum(_dot(xb, w1_ref[:, c * FF_CHUNK:(c + 1) * FF_CHUNK]), 0.0)
        term = _dot((h * h).astype(BF16), w2_ref[c * FF_CHUNK:(c + 1) * FF_CHUNK, :])
        acc = term if acc is None else acc + term
    ple = jax.nn.sigmoid(_dot(xb, wpg_ref[...])) * _dot(p_ref[...].astype(BF16), wp_ref[...])
    y = DEEPNORM_ALPHA * x + acc + ple
    o_ref[...] = _layer_norm(y, g_ref[...], b_ref[...])


def _ffn(x2, p2, w1, w2, wpg, wp, g, bvec):
    t = x2.shape[0]
    tm = min(TM_MLP, t)

    def rows(width):
        return pl.BlockSpec((tm, width), lambda i: (i, 0))

    return pl.pallas_call(
        _ffn_kernel,
        grid=(t // tm,),
        in_specs=[rows(D_MODEL), rows(PLE_DIM),
                  _const_spec(w1.shape), _const_spec(w2.shape), _const_spec(wpg.shape),
                  _const_spec(wp.shape), _const_spec(g.shape), _const_spec(bvec.shape)],
        out_specs=rows(D_MODEL),
        out_shape=jax.ShapeDtypeStruct((t, D_MODEL), F32),
        compiler_params=pltpu.CompilerParams(
            dimension_semantics=("arbitrary",), vmem_limit_bytes=VMEM_LIMIT),
        name="ffn",
    )(x2, p2, w1, w2, wpg, wp, g, bvec)


def _pack_w_in(w):
    d = w.shape[0]
    offs = {}
    o = 0
    for name, width in (("fq", W_FOX), ("fk", W_FOX), ("fv", W_FOX), ("ff", HEADS_FOX),
                        ("sq", W_SB), ("sk", W_SB), ("sv", W_SB),
                        ("dq", W_DSA), ("dk", HEAD_DIM), ("dv", HEAD_DIM),
                        ("iq", W_IDX), ("ik", HEAD_DIM), ("iw", IDX_HEADS),
                        ("g", N_BRANCH * D_MODEL)):
        offs[name] = w[:, o:o + width]
        o += width

    def z(n):
        return jnp.zeros((d, n), w.dtype)

    packed = jnp.concatenate(
        [offs["fk"], offs["sk"], z(W_SB_PAD - W_SB), offs["dk"], offs["dk"], offs["ik"], offs["ik"],
         offs["ff"], z(IW_LANE - HEADS_FOX), offs["iw"], z(LANE - IW_LANE - IDX_HEADS)], axis=1)
    assert packed.shape[1] == C_PACK
    packed_t = jnp.concatenate([offs["fq"], offs["sq"], offs["dq"], offs["iq"],
                                offs["fv"], offs["sv"], offs["dv"]], axis=1).T
    assert packed_t.shape[0] == R_PACK
    return packed.astype(BF16), packed_t.astype(BF16), offs["g"].astype(BF16)


def _rope_tables(s):
    inv = ROPE_THETA ** (-jnp.arange(HALF, dtype=F32) / HALF)
    ang = jnp.arange(s, dtype=F32)[:, None] * inv[None, :]
    cos, sin = jnp.cos(ang), jnp.sin(ang)
    row_major = jnp.concatenate([cos, cos, cos, cos, -sin, sin, -sin, sin], axis=1)
    transposed = jnp.concatenate([cos, cos, -sin, sin], axis=1).T
    return row_major, transposed


def _placement():
    rows_q, cols_q, rows_k, cols_k = [], [], [], []
    for h in range(HEADS_FOX):
        for t in range(N_CTERM):
            rows_q.append(t * LANE + h), cols_q.append(CSTRIDE * h + t)
            rows_k.append(t * LANE + h), cols_k.append(LANE + CSTRIDE * h + N_CTERM + t)
    place = jnp.zeros((N_CTERM * LANE, 2 * LANE), F32)
    place = place.at[jnp.array(rows_q), jnp.array(cols_q)].set(1.0)
    place = place.at[jnp.array(rows_k), jnp.array(cols_k)].set(-1.0)
    lane = jnp.arange(2 * LANE)
    head_lane = (lane % LANE) % CSTRIDE
    in_heads = (lane % LANE) < CSTRIDE * HEADS_FOX
    ones_q = (lane < LANE) & in_heads & (head_lane >= N_CTERM) & (head_lane < 2 * N_CTERM)
    ones_k = (lane >= LANE) & in_heads & (head_lane < N_CTERM)
    ones_row = jnp.where(ones_q | ones_k, 1.0, 0.0).astype(F32)[None, :]
    return place.astype(BF16), ones_row


def kernel(x, p, w_in, b_forget, w_up_fox, w_up_sb, w_up_dsa, w_out, ln1_g, ln1_b,
           w_ff_in, w_ff_out, w_ple, w_ple_gate, ln2_g, ln2_b):
    b, s, d = x.shape
    depth = w_in.shape[0]
    rope_tab, rope_tab_t = _rope_tables(s)
    place, ones_row = _placement()
    w_in_rows = w_in.reshape(depth * d, w_in.shape[2])
    for i in range(depth):
        w_pack, wt_pack, w_gate = _pack_w_in(w_in_rows[i * d:(i + 1) * d])
        bias_row = jnp.zeros((1, LANE), F32).at[0, :HEADS_FOX].set(b_forget[i].astype(F32))
        (fqt, fqct, fk, fkc, fvt, sqt, sk, svt, dqt, iqt, dk2, ik2, dvt, iwt) = _proj(
            x, w_pack, wt_pack, bias_row, rope_tab, rope_tab_t, place, ones_row)
        o_fox = _fox(fqt, fqct, fk, fkc, fvt)
        o_sb = _sb(sqt, sk, svt)
        o_dsa = _dsa(dqt, iqt, iwt, dk2, ik2, dvt)
        x1 = _merge(x.reshape(b * s, d), o_fox.reshape(b * s, W_FOX), o_sb.reshape(b * s, W_SB),
                    o_dsa.reshape(b * s, W_DSA), w_gate,
                    w_up_fox[i].astype(BF16), w_up_sb[i].astype(BF16), w_up_dsa[i].astype(BF16),
                    w_out[i].astype(BF16), ln1_g[i][None, :], ln1_b[i][None, :])
        x2 = _ffn(x1, p[i].reshape(b * s, PLE_DIM), w_ff_in[i].astype(BF16), w_ff_out[i].astype(BF16),
                  w_ple_gate[i].astype(BF16), w_ple[i].astype(BF16), ln2_g[i][None, :], ln2_b[i][None, :])
        x = x2.reshape(b, s, d)
    return x
```

```python
import functools

import jax
import jax.numpy as jnp
from jax import lax
from jax.experimental import pallas as pl
from jax.experimental.pallas import tpu as pltpu

D_MODEL = 1024
HEAD_DIM = 64
HALF = HEAD_DIM // 2
HEADS_FOX = 6
HEADS_SB = 5
HEADS_DSA = 5
IDX_HEADS = 8
CHUNK = 64
DSA_TOPK_MAX = 256
D_FF = 4 * D_MODEL
PLE_DIM = 256
ROPE_THETA = 10000.0
LN_EPS = 1e-5
N_BRANCH = 3
NEG = -1e30
DEPTH = 2
DEEPNORM_ALPHA = (2 * DEPTH) ** 0.25
W_FOX = HEADS_FOX * HEAD_DIM
W_SB = HEADS_SB * HEAD_DIM
W_DSA = HEADS_DSA * HEAD_DIM
W_IDX = IDX_HEADS * HEAD_DIM
SCALE = HEAD_DIM ** -0.5

LANE = 128
SUBLANE = 8
PACKED_ROWS = 16
W_SB_PAD = 384
IW_LANE = 8
N_CTERM = 3
VMEM_LIMIT = 56 * 1024 * 1024

TM_PROJ = 512
TKC = 256
TQ_DSA = 256
TK_DSA = TKC
TM_MLP = 512

BF16 = jnp.bfloat16
F32 = jnp.float32


def _log_sigmoid(x):
    return jnp.minimum(x, 0.0) - jnp.log(1.0 + jnp.exp(-jnp.abs(x)))


def _dot(a, b):
    return jnp.dot(a, b, preferred_element_type=F32)


def _dot_nt(a, b):
    return lax.dot_general(a, b, (((1,), (1,)), ((), ())), preferred_element_type=F32)


def _split3(v):
    hi = v.astype(BF16)
    r = v - hi.astype(F32)
    mid = r.astype(BF16)
    lo = (r - mid.astype(F32)).astype(BF16)
    return hi, mid, lo


def _pair_rows(qt_h, h):
    z = jnp.zeros_like(qt_h)
    return jnp.concatenate([z, qt_h] if h % 2 else [qt_h, z], axis=0)


OFF_FK = 0
OFF_SK = 384
OFF_DK2 = 768
OFF_IK2 = 896
OFF_SM = 1024
C_PACK = 1152
OFF_TFQ, OFF_TSQ, OFF_TDQ, OFF_TIQ = 0, 384, 704, 1024
OFF_TFV, OFF_TSV, OFF_TDV = 1536, 1920, 2240
R_PACK = 2304
CSTRIDE = PACKED_ROWS


def _swap_halves(h):
    lane = lax.broadcasted_iota(jnp.int32, h.shape, 1)
    first = (lane % HEAD_DIM) < HALF
    return jnp.where(first, pltpu.roll(h, LANE - HALF, 1), pltpu.roll(h, HALF, 1))


def _rope_rows(h, cos, sin):
    return h * cos + _swap_halves(h) * sin


def _rope_t(ht, cos_t, sin_t):
    parts = []
    for h in range(ht.shape[0] // HEAD_DIM):
        x = ht[h * HEAD_DIM:(h + 1) * HEAD_DIM]
        swapped = jnp.concatenate([x[HALF:], x[:HALF]], axis=0)
        parts.append(x * cos_t + swapped * sin_t)
    return jnp.concatenate(parts, axis=0)


def _proj_kernel(x_ref, w_ref, wt_ref, bias_ref, tab_ref, tabt_ref, place_ref, ones_ref,
                 fqt_ref, fqct_ref, fk_ref, fkc_ref, fvt_ref, sqt_ref, sk_ref, svt_ref,
                 dqt_ref, iqt_ref, dk2_ref, ik2_ref, dvt_ref, iwt_ref, carry_ref):
    tm = x_ref.shape[1]
    xb = x_ref[0].astype(BF16)

    def seg(off, width):
        return _dot(xb, w_ref[:, off:off + width])

    def seg_t(off, nrows):
        return _dot_nt(wt_ref[off:off + nrows, :], xb)

    def put_chunks(ref, val, chunk):
        for r in range(tm // chunk):
            ref[0, r] = val[:, r * chunk:(r + 1) * chunk].astype(ref.dtype)

    fk_ref[0] = seg(OFF_FK, W_FOX).astype(BF16)
    sk_ref[0] = seg(OFF_SK, W_SB_PAD)[:, :W_SB].astype(BF16)
    cos, sin = tab_ref[:, 0:LANE], tab_ref[:, LANE:2 * LANE]
    dk2_ref[0] = _rope_rows(seg(OFF_DK2, LANE), cos, sin).astype(BF16)
    ik2_ref[0] = _rope_rows(seg(OFF_IK2, LANE), cos, sin).astype(BF16)

    cos_t, sin_t = tabt_ref[0:HEAD_DIM, :], tabt_ref[HEAD_DIM:2 * HEAD_DIM, :]
    put_chunks(fqt_ref, seg_t(OFF_TFQ, W_FOX) * SCALE, TKC)
    put_chunks(sqt_ref, seg_t(OFF_TSQ, W_SB) * SCALE, TKC)
    put_chunks(dqt_ref, _rope_t(seg_t(OFF_TDQ, W_DSA), cos_t, sin_t) * SCALE, TQ_DSA)
    put_chunks(iqt_ref, _rope_t(seg_t(OFF_TIQ, W_IDX), cos_t, sin_t), TQ_DSA)
    put_chunks(fvt_ref, seg_t(OFF_TFV, W_FOX), TKC)
    put_chunks(svt_ref, seg_t(OFF_TSV, W_SB), TKC)
    put_chunks(dvt_ref, seg_t(OFF_TDV, HEAD_DIM), TKC)

    small = seg(OFF_SM, LANE)
    logf = _log_sigmoid(small + bias_ref[...])

    @pl.when(pl.program_id(1) == 0)
    def _():
        carry_ref[...] = jnp.zeros_like(carry_ref)

    row = lax.broadcasted_iota(jnp.int32, (tm, tm), 0)
    col = lax.broadcasted_iota(jnp.int32, (tm, tm), 1)
    tri = jnp.where(col <= row, 1.0, 0.0).astype(BF16)
    c = _dot(tri, jnp.concatenate(_split3(logf), axis=1))
    c = c[:, 0:LANE] + c[:, LANE:2 * LANE] + c[:, 2 * LANE:3 * LANE] + carry_ref[...]
    carry_ref[...] = c[tm - 1:tm, :]
    placed = _dot(jnp.concatenate(_split3(c), axis=1), place_ref[...]) + ones_ref[...]
    put_chunks(fqct_ref, placed[:, 0:LANE].T, TKC)
    fkc_ref[0] = placed[:, LANE:2 * LANE].astype(BF16)
    put_chunks(iwt_ref, small.T[IW_LANE:IW_LANE + IDX_HEADS], TQ_DSA)


def _proj(x, w_pack, wt_pack, bias_row, rope_tab, rope_tab_t, place, ones_row):
    b, s, d = x.shape
    tm = min(TM_PROJ, s)
    grid = (b, s // tm)

    def rows(width, dtype=BF16):
        return (jax.ShapeDtypeStruct((b, s, width), dtype),
                pl.BlockSpec((1, tm, width), lambda bi, si: (bi, si, 0)))

    def chunks(nrows, chunk, dtype=BF16):
        return (jax.ShapeDtypeStruct((b, s // chunk, nrows, chunk), dtype),
                pl.BlockSpec((1, tm // chunk, nrows, chunk), lambda bi, si: (bi, si, 0, 0)))

    def const(shape):
        return pl.BlockSpec(shape, lambda bi, si: (0,) * len(shape))

    outs = [chunks(W_FOX, TKC), chunks(LANE, TKC), rows(W_FOX), rows(LANE), chunks(W_FOX, TKC),
            chunks(W_SB, TKC), rows(W_SB), chunks(W_SB, TKC),
            chunks(W_DSA, TQ_DSA), chunks(W_IDX, TQ_DSA), rows(LANE), rows(LANE), chunks(HEAD_DIM, TKC),
            chunks(IDX_HEADS, TQ_DSA, F32)]
    return pl.pallas_call(
        _proj_kernel,
        grid=grid,
        in_specs=[pl.BlockSpec((1, tm, d), lambda bi, si: (bi, si, 0)),
                  const((d, C_PACK)), const((R_PACK, d)), const((1, LANE)),
                  pl.BlockSpec((tm, 2 * LANE), lambda bi, si: (si, 0)),
                  pl.BlockSpec((2 * HEAD_DIM, tm), lambda bi, si: (0, si)),
                  const(place.shape), const(ones_row.shape)],
        out_specs=[o[1] for o in outs],
        out_shape=[o[0] for o in outs],
        scratch_shapes=[pltpu.VMEM((1, LANE), F32)],
        compiler_params=pltpu.CompilerParams(
            dimension_semantics=("arbitrary", "arbitrary"), vmem_limit_bytes=VMEM_LIMIT),
        name="proj",
    )(x, w_pack, wt_pack, bias_row, rope_tab, rope_tab_t, place, ones_row)


def _fox_kernel(qt_ref, qct_ref, k_ref, kc_ref, vt_ref, o_ref, qcat_ref, m_ref, l_ref, acc_ref):
    tq = tk = TKC
    i = pl.program_id(1)
    krow = lax.broadcasted_iota(jnp.int32, (tk, tq), 0)
    qcol = lax.broadcasted_iota(jnp.int32, (tk, tq), 1)
    causal = krow <= qcol

    for h in range(HEADS_FOX):
        gate_rows = jnp.concatenate(
            ([jnp.zeros((CSTRIDE * h, tq), BF16)] if h else [])
            + [qct_ref[0, 0, CSTRIDE * h:CSTRIDE * (h + 1), :],
               jnp.zeros((LANE - CSTRIDE * (h + 1), tq), BF16)], axis=0)
        qcat_ref[h] = jnp.concatenate(
            [_pair_rows(qt_ref[0, 0, h * HEAD_DIM:(h + 1) * HEAD_DIM, :], h), gate_rows], axis=0)
    m_ref[...] = jnp.full(m_ref.shape, NEG, F32)
    l_ref[...] = jnp.zeros(l_ref.shape, F32)
    acc_ref[...] = jnp.zeros(acc_ref.shape, F32)

    def blocks(js, masked):
        starts = [pl.multiple_of(j * tk, tk) for j in js]
        kcs = [kc_ref[0, pl.ds(ks, tk), :] for ks in starts]
        scores = []
        for g in range(HEADS_FOX // 2):
            kcats = [jnp.concatenate([k_ref[0, pl.ds(ks, tk), g * LANE:(g + 1) * LANE], kc], axis=1)
                     for ks, kc in zip(starts, kcs)]
            for h in (2 * g, 2 * g + 1):
                scores.append([_dot(kcat, qcat_ref[h]) for kcat in kcats])
        vts = [vt_ref[0, j] for j in js]
        for h in range(HEADS_FOX):
            ss = [jnp.where(causal, s, NEG) if masked else s for s in scores[h]]
            m_old = m_ref[h]
            m_new = m_old
            for s in ss:
                m_new = jnp.maximum(m_new, jnp.max(s, axis=0, keepdims=True))
            alpha = jnp.exp(m_old - m_new)
            ps = [jnp.exp(s - m_new) for s in ss]
            m_ref[h] = m_new
            l_ref[h] = alpha * l_ref[h] + sum(jnp.sum(p, axis=0, keepdims=True) for p in ps)
            acc_ref[h] = alpha * acc_ref[h] + sum(
                _dot(vt[h * HEAD_DIM:(h + 1) * HEAD_DIM, :], p.astype(BF16)) for vt, p in zip(vts, ps))

    blocks((i,), True)

    def body(t, _):
        blocks((2 * t, 2 * t + 1), False)
        return 0

    lax.fori_loop(0, lax.shift_right_logical(i, 1), body, 0)

    @pl.when((i & 1) == 1)
    def _():
        blocks((i - 1,), False)

    o_ref[0] = jnp.concatenate([(acc_ref[h] / l_ref[h]).T for h in range(HEADS_FOX)],
                               axis=1).astype(o_ref.dtype)


def _fox(fqt, fqct, fk, fkc, fvt):
    b, s, w = fk.shape
    tq = TKC
    return pl.pallas_call(
        _fox_kernel,
        grid=(b, s // tq),
        in_specs=[pl.BlockSpec((1, 1, w, tq), lambda bi, qi: (bi, qi, 0, 0)),
                  pl.BlockSpec((1, 1, LANE, tq), lambda bi, qi: (bi, qi, 0, 0)),
                  pl.BlockSpec((1, s, w), lambda bi, qi: (bi, 0, 0)),
                  pl.BlockSpec((1, s, LANE), lambda bi, qi: (bi, 0, 0)),
                  pl.BlockSpec((1, s // TKC, w, TKC), lambda bi, qi: (bi, 0, 0, 0))],
        out_specs=pl.BlockSpec((1, tq, w), lambda bi, qi: (bi, qi, 0)),
        out_shape=jax.ShapeDtypeStruct((b, s, w), BF16),
        scratch_shapes=[pltpu.VMEM((HEADS_FOX, 2 * LANE, tq), BF16),
                        pltpu.VMEM((HEADS_FOX, 1, tq), F32),
                        pltpu.VMEM((HEADS_FOX, 1, tq), F32),
                        pltpu.VMEM((HEADS_FOX, HEAD_DIM, tq), F32)],
        compiler_params=pltpu.CompilerParams(
            dimension_semantics=("arbitrary", "arbitrary"), vmem_limit_bytes=VMEM_LIMIT),
        name="fox",
    )(fqt, fqct, fk, fkc, fvt)


SB_DEAD_LOG = -110.0

def _sb_kernel(qt_ref, k_ref, vt_ref, o_ref, qm_ref, later_w_ref, run_ref, acc_ref):
    tq = tk = TKC
    i = pl.program_id(1)
    krow = lax.broadcasted_iota(jnp.int32, (tk, tq), 0)
    qcol = lax.broadcasted_iota(jnp.int32, (tk, tq), 1)
    strict = krow < qcol
    r = lax.broadcasted_iota(jnp.int32, (tk, 2 * tk), 0)
    c = lax.broadcasted_iota(jnp.int32, (tk, 2 * tk), 1) % tk
    later_w_ref[...] = jnp.where(c > r, -1.0, 0.0).astype(BF16)
    for h in range(HEADS_SB):
        qm_ref[h] = _pair_rows(qt_ref[0, 0, h * HEAD_DIM:(h + 1) * HEAD_DIM, :], h)
    run_ref[...] = jnp.zeros(run_ref.shape, F32)
    acc_ref[...] = jnp.zeros(acc_ref.shape, F32)

    def block(j, masked):
        ks = pl.multiple_of(j * tk, tk)
        vt = vt_ref[0, j]
        zs = []
        for h in range(HEADS_SB):
            g = h // 2
            width = min(LANE, W_SB - g * LANE)
            kp = k_ref[0, pl.ds(ks, tk), g * LANE:g * LANE + width]
            zs.append(_dot(kp, qm_ref[h][:width, :]))
        sps = []
        for h in range(HEADS_SB):
            z = zs[h]
            sp = jnp.maximum(z, 0.0) + jnp.log(1.0 + jnp.exp(-jnp.abs(z)))
            sps.append(jnp.where(strict, sp, 0.0) if masked else sp)
        laters = []
        for h in range(HEADS_SB):
            sp_hi = sps[h].astype(BF16)
            sp_lo = (sps[h] - sp_hi.astype(F32)).astype(BF16)
            laters.append(_dot(later_w_ref[...], jnp.concatenate([sp_hi, sp_lo], axis=0)))
        for h in range(HEADS_SB):
            run = run_ref[h]
            a = jnp.exp(zs[h] - sps[h] + laters[h] + run)
            if masked:
                a = jnp.where(strict, a, 0.0)
            acc_ref[h] = acc_ref[h] + _dot(vt[h * HEAD_DIM:(h + 1) * HEAD_DIM, :], a.astype(BF16))
            run_ref[h] = run + laters[h][0:1, :] - sps[h][0:1, :]

    def alive():
        return jnp.max(run_ref[...]) > SB_DEAD_LOG

    block(i, True)

    def body(carry):
        t, _ = carry
        block(i - 1 - t, False)
        return t + 1, alive()

    lax.while_loop(lambda carry: (carry[0] < i) & carry[1], body, (jnp.int32(0), alive()))
    o_ref[0] = jnp.concatenate([acc_ref[h].T for h in range(HEADS_SB)], axis=1).astype(o_ref.dtype)


def _sb(sqt, sk, svt):
    b, s, w = sk.shape
    tq = TKC
    return pl.pallas_call(
        _sb_kernel,
        grid=(b, s // tq),
        in_specs=[pl.BlockSpec((1, 1, w, tq), lambda bi, qi: (bi, qi, 0, 0)),
                  pl.BlockSpec((1, s, w), lambda bi, qi: (bi, 0, 0)),
                  pl.BlockSpec((1, s // TKC, w, TKC), lambda bi, qi: (bi, 0, 0, 0))],
        out_specs=pl.BlockSpec((1, tq, w), lambda bi, qi: (bi, qi, 0)),
        out_shape=jax.ShapeDtypeStruct((b, s, w), BF16),
        scratch_shapes=[pltpu.VMEM((HEADS_SB, LANE, tq), BF16),
                        pltpu.VMEM((TKC, 2 * TKC), BF16),
                        pltpu.VMEM((HEADS_SB, 1, tq), F32),
                        pltpu.VMEM((HEADS_SB, HEAD_DIM, tq), F32)],
        compiler_params=pltpu.CompilerParams(
            dimension_semantics=("arbitrary", "arbitrary"), vmem_limit_bytes=VMEM_LIMIT),
        name="sb",
    )(sqt, sk, svt)


NEG_KEY = -0x7149F2CB


def _sortable(score):
    bits = lax.bitcast_convert_type(score, jnp.int32)
    return jnp.where(bits < 0, bits ^ jnp.int32(0x7FFFFFFF), bits)


def _sum_sublane_groups(x):
    parts = [x[r * SUBLANE:(r + 1) * SUBLANE] for r in range(x.shape[0] // SUBLANE)]
    while len(parts) > 1:
        parts = [parts[k] + parts[k + 1] for k in range(0, len(parts) - 1, 2)] + (
            [parts[-1]] if len(parts) % 2 else [])
    return parts[0]


def _pair_tree_sum(parts):
    while len(parts) > 1:
        parts = [parts[k] + parts[k + 1] for k in range(0, len(parts), 2)]
    return parts[0]


def _count_ge16(ref, npair, cand):
    tk, tq = ref.shape[1], ref.shape[2]
    cand16 = jnp.broadcast_to(cand, (PACKED_ROWS, tq)).astype(jnp.int16)
    one, zero = jnp.ones((PACKED_ROWS, tq), jnp.int16), jnp.zeros((PACKED_ROWS, tq), jnp.int16)

    def body(t, acc):
        parts = []
        for u in range(2):
            blk = ref[2 * t + u]
            parts += [jnp.where(blk[r * PACKED_ROWS:(r + 1) * PACKED_ROWS] >= cand16, one, zero)
                      for r in range(tk // PACKED_ROWS)]
        return acc + _pair_tree_sum(parts)

    acc = lax.fori_loop(0, npair, body, zero)
    return jnp.sum(acc.astype(F32), axis=0, keepdims=True)


def _search16(ref, npair, n_sel):
    tq = ref.shape[2]

    def bit_step(t, thr_u):
        bit = lax.shift_left(jnp.int32(1), jnp.int32(15) - t)
        cand_u = thr_u | bit
        cnt = _count_ge16(ref, npair, cand_u - 32768)
        return jnp.where(cnt >= n_sel, cand_u, thr_u)

    return lax.fori_loop(0, 16, bit_step, jnp.zeros((1, tq), jnp.int32)) - 32768


def _dsa_kernel(dqt_ref, iqt_ref, iwt_ref, dk2_ref, ik2_ref, dvt_ref, o_ref,
                key_ref, hi_ref, lo_ref, bias_ref, iq8_ref, q5_ref, jmax_ref, m_ref, l_ref, acc_ref,
                *, n_sel, idx_bits):
    tq, tk = TQ_DSA, TK_DSA
    i = pl.program_id(1)
    nkb = (i * tq + tq - 1) // tk + 1
    npair = (nkb + 1) // 2
    kloc = lax.broadcasted_iota(jnp.int32, (tk, tq), 0)
    qpos = i * tq + lax.broadcasted_iota(jnp.int32, (tk, tq), 1)
    klimit = (qpos // CHUNK + 1) * CHUNK
    for h in range(IDX_HEADS):
        iq8_ref[:, h * tq:(h + 1) * tq] = _pair_rows(iqt_ref[0, 0, h * HEAD_DIM:(h + 1) * HEAD_DIM, :], h)
    for h in range(HEADS_DSA):
        q5_ref[:, h * tq:(h + 1) * tq] = _pair_rows(dqt_ref[0, 0, h * HEAD_DIM:(h + 1) * HEAD_DIM, :], h)
    iwt = iwt_ref[0, 0]

    def score_pair(t, _):
        blocks = (2 * t, 2 * t + 1)
        logits = [_dot(ik2_ref[0, pl.ds(pl.multiple_of(j * tk, tk), tk), :], iq8_ref[...])
                  for j in blocks]
        for j, a in zip(blocks, logits):
            sc = jnp.zeros((tk, tq), F32)
            for h in range(IDX_HEADS):
                sc = sc + iwt[h:h + 1, :] * jnp.maximum(a[:, h * tq:(h + 1) * tq], 0.0)
            sc = sc + 0.0
            sc = jnp.where(kloc + j * tk < klimit, sc, NEG)
            key = _sortable(sc)
            key_ref[j] = key
            hi_ref[j] = lax.shift_right_arithmetic(key, jnp.int32(16)).astype(jnp.int16)
            lo_ref[j] = ((key & jnp.int32(0xFFFF)) - 32768).astype(jnp.int16)
        return 0

    lax.fori_loop(0, npair, score_pair, 0)

    thr_hi = _search16(hi_ref, npair, n_sel)
    thr_hi16 = jnp.broadcast_to(thr_hi, (PACKED_ROWS, tq)).astype(jnp.int16)

    def low_pair(t, _):
        for j in (2 * t, 2 * t + 1):
            for r in range(tk // PACKED_ROWS):
                rows = slice(r * PACKED_ROWS, (r + 1) * PACKED_ROWS)
                hi = hi_ref[j, rows, :]
                pinned = jnp.where(hi > thr_hi16, jnp.int16(32767), jnp.int16(-32768))
                lo_ref[j, rows, :] = jnp.where(hi == thr_hi16, lo_ref[j, rows, :], pinned)
        return 0

    lax.fori_loop(0, npair, low_pair, 0)
    thr = thr_hi * 65536 + (_search16(lo_ref, npair, n_sel) + 32768)

    def count(pred):
        def body(t, acc):
            for j in (2 * t, 2 * t + 1):
                hit = jnp.where(pred(key_ref[j], kloc + j * tk), 1.0, 0.0)
                acc = acc + _sum_sublane_groups(hit)
            return acc
        acc = lax.fori_loop(0, npair, body, jnp.zeros((SUBLANE, tq), F32))
        return jnp.sum(acc, axis=0, keepdims=True)

    cnt_ge = count(lambda k, _: k >= thr)
    jmax_ref[...] = jnp.full((1, tq), 2 ** idx_bits - 1, jnp.int32)
    tied = (cnt_ge > n_sel) & (thr > NEG_KEY)

    @pl.when(jnp.max(jnp.where(tied, 1.0, 0.0)) > 0.0)
    def _():
        need = n_sel - count(lambda k, _: k > thr)

        def tied_pos_pair(t, _):
            for j in (2 * t, 2 * t + 1):
                lo_ref[j] = jnp.where(key_ref[j] == thr, kloc + j * tk, 32767).astype(jnp.int16)
            return 0

        lax.fori_loop(0, npair, tied_pos_pair, 0)
        stored = (2 * npair * tk).astype(F32)

        def idx_bit(t, jmax):
            bit = lax.shift_left(jnp.int32(1), jnp.int32(idx_bits - 1) - t)
            cand = jmax | bit
            cnt = stored - _count_ge16(lo_ref, npair, cand)
            return jnp.where(cnt < need, cand, jmax)

        jmax_ref[...] = lax.fori_loop(0, idx_bits, idx_bit, jnp.zeros((1, tq), jnp.int32))

    jmax = jmax_ref[...]

    def bias_pair(t, _):
        for j in (2 * t, 2 * t + 1):
            k = key_ref[j]
            pos = kloc + j * tk
            sel = ((k > thr) | ((k == thr) & (pos <= jmax))) & (pos < klimit)
            bias_ref[j] = jnp.where(sel, 0.0, NEG)
        return 0

    lax.fori_loop(0, npair, bias_pair, 0)

    m_ref[...] = jnp.full(m_ref.shape, NEG, F32)
    l_ref[...] = jnp.zeros(l_ref.shape, F32)
    acc_ref[...] = jnp.zeros(acc_ref.shape, F32)

    def att_pair(t, _):
        blocks = (2 * t, 2 * t + 1)
        scores = [_dot(dk2_ref[0, pl.ds(pl.multiple_of(j * tk, tk), tk), :], q5_ref[...])
                  + jnp.concatenate([bias_ref[j]] * HEADS_DSA, axis=1)
                  for j in blocks]
        m_old = m_ref[...]
        m_new = jnp.maximum(m_old, jnp.maximum(jnp.max(scores[0], axis=0, keepdims=True),
                                               jnp.max(scores[1], axis=0, keepdims=True)))
        alpha = jnp.exp(m_old - m_new)
        m_ref[...] = m_new
        p = [jnp.exp(s - m_new) for s in scores]
        l_ref[...] = alpha * l_ref[...] + (jnp.sum(p[0], axis=0, keepdims=True)
                                           + jnp.sum(p[1], axis=0, keepdims=True))
        acc_ref[...] = alpha * acc_ref[...] + (_dot(dvt_ref[0, blocks[0]], p[0].astype(BF16))
                                               + _dot(dvt_ref[0, blocks[1]], p[1].astype(BF16)))
        return 0

    lax.fori_loop(0, npair, att_pair, 0)
    out = acc_ref[...] / l_ref[...]
    o_ref[0] = jnp.concatenate([out[:, h * tq:(h + 1) * tq].T for h in range(HEADS_DSA)],
                               axis=1).astype(o_ref.dtype)


def _dsa(dqt, iqt, iwt, dk2, ik2, dvt):
    b, s, _ = dk2.shape
    tq, tk = TQ_DSA, TK_DSA
    n_sel = min(DSA_TOPK_MAX, s // 4)
    assert n_sel <= tk and s % (2 * tk) == 0 and tk % tq == 0 and s < 2 ** 15
    idx_bits = max(1, (s - 1).bit_length())
    return pl.pallas_call(
        functools.partial(_dsa_kernel, n_sel=n_sel, idx_bits=idx_bits),
        grid=(b, s // tq),
        in_specs=[pl.BlockSpec((1, 1, W_DSA, tq), lambda bi, qi: (bi, qi, 0, 0)),
                  pl.BlockSpec((1, 1, W_IDX, tq), lambda bi, qi: (bi, qi, 0, 0)),
                  pl.BlockSpec((1, 1, IDX_HEADS, tq), lambda bi, qi: (bi, qi, 0, 0)),
                  pl.BlockSpec((1, s, LANE), lambda bi, qi: (bi, 0, 0)),
                  pl.BlockSpec((1, s, LANE), lambda bi, qi: (bi, 0, 0)),
                  pl.BlockSpec((1, s // tk, HEAD_DIM, tk), lambda bi, qi: (bi, 0, 0, 0))],
        out_specs=pl.BlockSpec((1, tq, W_DSA), lambda bi, qi: (bi, qi, 0)),
        out_shape=jax.ShapeDtypeStruct((b, s, W_DSA), BF16),
        scratch_shapes=[pltpu.VMEM((s // tk, tk, tq), jnp.int32),
                        pltpu.VMEM((s // tk, tk, tq), jnp.int16),
                        pltpu.VMEM((s // tk, tk, tq), jnp.int16),
                        pltpu.VMEM((s // tk, tk, tq), F32),
                        pltpu.VMEM((LANE, IDX_HEADS * tq), BF16),
                        pltpu.VMEM((LANE, HEADS_DSA * tq), BF16),
                        pltpu.VMEM((1, tq), jnp.int32),
                        pltpu.VMEM((1, HEADS_DSA * tq), F32),
                        pltpu.VMEM((1, HEADS_DSA * tq), F32),
                        pltpu.VMEM((HEAD_DIM, HEADS_DSA * tq), F32)],
        compiler_params=pltpu.CompilerParams(
            dimension_semantics=("arbitrary", "arbitrary"), vmem_limit_bytes=VMEM_LIMIT),
        name="dsa",
    )(dqt, iqt, iwt, dk2, ik2, dvt)


def _layer_norm(y, g, b):
    mu = jnp.mean(y, axis=1, keepdims=True)
    d = y - mu
    var = jnp.mean(d * d, axis=1, keepdims=True)
    return d * lax.rsqrt(var + LN_EPS) * g + b


def _merge_kernel(x_ref, of_ref, os_ref, od_ref, wg_ref, wuf_ref, wus_ref, wud_ref, wo_ref,
                  g_ref, b_ref, o_ref):
    x = x_ref[...]
    xb = x.astype(BF16)
    merged = None
    for n, (br_ref, wu_ref) in enumerate(((of_ref, wuf_ref), (os_ref, wus_ref), (od_ref, wud_ref))):
        gate = jax.nn.sigmoid(_dot(xb, wg_ref[:, n * D_MODEL:(n + 1) * D_MODEL]))
        term = gate * _dot(br_ref[...], wu_ref[...])
        merged = term if merged is None else merged + term
    y = DEEPNORM_ALPHA * x + _dot(merged.astype(BF16), wo_ref[...])
    o_ref[...] = _layer_norm(y, g_ref[...], b_ref[...])


def _const_spec(shape):
    return pl.BlockSpec(shape, lambda i: (0,) * len(shape), pipeline_mode=pl.Buffered(1))


def _merge(x2, o_fox, o_sb, o_dsa, wg, wuf, wus, wud, wo, g, bvec):
    t = x2.shape[0]
    tm = min(TM_MLP, t)

    def rows(width):
        return pl.BlockSpec((tm, width), lambda i: (i, 0))

    return pl.pallas_call(
        _merge_kernel,
        grid=(t // tm,),
        in_specs=[rows(D_MODEL), rows(W_FOX), rows(W_SB), rows(W_DSA),
                  _const_spec(wg.shape), _const_spec(wuf.shape), _const_spec(wus.shape),
                  _const_spec(wud.shape), _const_spec(wo.shape),
                  _const_spec(g.shape), _const_spec(bvec.shape)],
        out_specs=rows(D_MODEL),
        out_shape=jax.ShapeDtypeStruct((t, D_MODEL), F32),
        compiler_params=pltpu.CompilerParams(
            dimension_semantics=("arbitrary",), vmem_limit_bytes=VMEM_LIMIT),
        name="merge",
    )(x2, o_fox, o_sb, o_dsa, wg, wuf, wus, wud, wo, g, bvec)


FF_CHUNK = 1024


def _ffn_kernel(x_ref, p_ref, w1_ref, w2_ref, wpg_ref, wp_ref, g_ref, b_ref, o_ref):
    x = x_ref[...]
    xb = x.astype(BF16)
    acc = None
    for c in range(D_FF // FF_CHUNK):
        h = jnp.maximum(_dot(xb, w1_ref[:, c * FF_CHUNK:(c + 1) * FF_CHUNK]), 0.0)
        term = _dot((h * h).astype(BF16), w2_ref[c * FF_CHUNK:(c + 1) * FF_CHUNK, :])
        acc = term if acc is None else acc + term
    ple = jax.nn.sigmoid(_dot(xb, wpg_ref[...])) * _dot(p_ref[...].astype(BF16), wp_ref[...])
    y = DEEPNORM_ALPHA * x + acc + ple
    o_ref[...] = _layer_norm(y, g_ref[...], b_ref[...])


def _ffn(x2, p2, w1, w2, wpg, wp, g, bvec):
    t = x2.shape[0]
    tm = min(TM_MLP, t)

    def rows(width):
        return pl.BlockSpec((tm, width), lambda i: (i, 0))

    return pl.pallas_call(
        _ffn_kernel,
        grid=(t // tm,),
        in_specs=[rows(D_MODEL), rows(PLE_DIM),
                  _const_spec(w1.shape), _const_spec(w2.shape), _const_spec(wpg.shape),
                  _const_spec(wp.shape), _const_spec(g.shape), _const_spec(bvec.shape)],
        out_specs=rows(D_MODEL),
        out_shape=jax.ShapeDtypeStruct((t, D_MODEL), F32),
        compiler_params=pltpu.CompilerParams(
            dimension_semantics=("arbitrary",), vmem_limit_bytes=VMEM_LIMIT),
        name="ffn",
    )(x2, p2, w1, w2, wpg, wp, g, bvec)


def _pack_w_in(w):
    d = w.shape[0]
    offs = {}
    o = 0
    for name, width in (("fq", W_FOX), ("fk", W_FOX), ("fv", W_FOX), ("ff", HEADS_FOX),
                        ("sq", W_SB), ("sk", W_SB), ("sv", W_SB),
                        ("dq", W_DSA), ("dk", HEAD_DIM), ("dv", HEAD_DIM),
                        ("iq", W_IDX), ("ik", HEAD_DIM), ("iw", IDX_HEADS),
                        ("g", N_BRANCH * D_MODEL)):
        offs[name] = w[:, o:o + width]
        o += width

    def z(n):
        return jnp.zeros((d, n), w.dtype)

    packed = jnp.concatenate(
        [offs["fk"], offs["sk"], z(W_SB_PAD - W_SB), offs["dk"], offs["dk"], offs["ik"], offs["ik"],
         offs["ff"], z(IW_LANE - HEADS_FOX), offs["iw"], z(LANE - IW_LANE - IDX_HEADS)], axis=1)
    assert packed.shape[1] == C_PACK
    packed_t = jnp.concatenate([offs["fq"], offs["sq"], offs["dq"], offs["iq"],
                                offs["fv"], offs["sv"], offs["dv"]], axis=1).T
    assert packed_t.shape[0] == R_PACK
    return packed.astype(BF16), packed_t.astype(BF16), offs["g"].astype(BF16)


def _rope_tables(s):
    inv = ROPE_THETA ** (-jnp.arange(HALF, dtype=F32) / HALF)
    ang = jnp.arange(s, dtype=F32)[:, None] * inv[None, :]
    cos, sin = jnp.cos(ang), jnp.sin(ang)
    row_major = jnp.concatenate([cos, cos, cos, cos, -sin, sin, -sin, sin], axis=1)
    transposed = jnp.concatenate([cos, cos, -sin, sin], axis=1).T
    return row_major, transposed


def _placement():
    rows_q, cols_q, rows_k, cols_k = [], [], [], []
    for h in range(HEADS_FOX):
        for t in range(N_CTERM):
            rows_q.append(t * LANE + h), cols_q.append(CSTRIDE * h + t)
            rows_k.append(t * LANE + h), cols_k.append(LANE + CSTRIDE * h + N_CTERM + t)
    place = jnp.zeros((N_CTERM * LANE, 2 * LANE), F32)
    place = place.at[jnp.array(rows_q), jnp.array(cols_q)].set(1.0)
    place = place.at[jnp.array(rows_k), jnp.array(cols_k)].set(-1.0)
    lane = jnp.arange(2 * LANE)
    head_lane = (lane % LANE) % CSTRIDE
    in_heads = (lane % LANE) < CSTRIDE * HEADS_FOX
    ones_q = (lane < LANE) & in_heads & (head_lane >= N_CTERM) & (head_lane < 2 * N_CTERM)
    ones_k = (lane >= LANE) & in_heads & (head_lane < N_CTERM)
    ones_row = jnp.where(ones_q | ones_k, 1.0, 0.0).astype(F32)[None, :]
    return place.astype(BF16), ones_row


def kernel(x, p, w_in, b_forget, w_up_fox, w_up_sb, w_up_dsa, w_out, ln1_g, ln1_b,
           w_ff_in, w_ff_out, w_ple, w_ple_gate, ln2_g, ln2_b):
    b, s, d = x.shape
    depth = w_in.shape[0]
    rope_tab, rope_tab_t = _rope_tables(s)
    place, ones_row = _placement()
    w_in_rows = w_in.reshape(depth * d, w_in.shape[2])
    for i in range(depth):
        w_pack, wt_pack, w_gate = _pack_w_in(w_in_rows[i * d:(i + 1) * d])
        bias_row = jnp.zeros((1, LANE), F32).at[0, :HEADS_FOX].set(b_forget[i].astype(F32))
        (fqt, fqct, fk, fkc, fvt, sqt, sk, svt, dqt, iqt, dk2, ik2, dvt, iwt) = _proj(
            x, w_pack, wt_pack, bias_row, rope_tab, rope_tab_t, place, ones_row)
        o_fox = _fox(fqt, fqct, fk, fkc, fvt)
        o_sb = _sb(sqt, sk, svt)
        o_dsa = _dsa(dqt, iqt, iwt, dk2, ik2, dvt)
        x1 = _merge(x.reshape(b * s, d), o_fox.reshape(b * s, W_FOX), o_sb.reshape(b * s, W_SB),
                    o_dsa.reshape(b * s, W_DSA), w_gate,
                    w_up_fox[i].astype(BF16), w_up_sb[i].astype(BF16), w_up_dsa[i].astype(BF16),
                    w_out[i].astype(BF16), ln1_g[i][None, :], ln1_b[i][None, :])
        x2 = _ffn(x1, p[i].reshape(b * s, PLE_DIM), w_ff_in[i].astype(BF16), w_ff_out[i].astype(BF16),
                  w_ple_gate[i].astype(BF16), w_ple[i].astype(BF16), ln2_g[i][None, :], ln2_b[i][None, :])
        x = x2.reshape(b, s, d)
    return x
```

```python
import functools

import jax
import jax.numpy as jnp
from jax import lax
from jax.experimental import pallas as pl
from jax.experimental.pallas import tpu as pltpu

D_MODEL = 1024
HEAD_DIM = 64
HALF = HEAD_DIM // 2
HEADS_FOX = 6
HEADS_SB = 5
HEADS_DSA = 5
IDX_HEADS = 8
CHUNK = 64
DSA_TOPK_MAX = 256
D_FF = 4 * D_MODEL
PLE_DIM = 256
ROPE_THETA = 10000.0
LN_EPS = 1e-5
N_BRANCH = 3
NEG = -1e30
DEPTH = 2
DEEPNORM_ALPHA = (2 * DEPTH) ** 0.25
W_FOX = HEADS_FOX * HEAD_DIM
W_SB = HEADS_SB * HEAD_DIM
W_DSA = HEADS_DSA * HEAD_DIM
W_IDX = IDX_HEADS * HEAD_DIM
SCALE = HEAD_DIM ** -0.5

LANE = 128
SUBLANE = 8
PACKED_ROWS = 16
W_SB_PAD = 384
IW_LANE = 8
N_CTERM = 3
VMEM_LIMIT = 56 * 1024 * 1024

TM_PROJ = 512
TKC = 256
TQ_DSA = 256
TK_DSA = TKC
TM_MLP = 1024
MLP_SPLIT = 2

BF16 = jnp.bfloat16
F32 = jnp.float32


def _log_sigmoid(x):
    return jnp.minimum(x, 0.0) - jnp.log(1.0 + jnp.exp(-jnp.abs(x)))


def _dot(a, b):
    return jnp.dot(a, b, preferred_element_type=F32)


def _dot_nt(a, b):
    return lax.dot_general(a, b, (((1,), (1,)), ((), ())), preferred_element_type=F32)


def _split3(v):
    hi = v.astype(BF16)
    r = v - hi.astype(F32)
    mid = r.astype(BF16)
    lo = (r - mid.astype(F32)).astype(BF16)
    return hi, mid, lo


def _pair_rows(qt_h, h):
    z = jnp.zeros_like(qt_h)
    return jnp.concatenate([z, qt_h] if h % 2 else [qt_h, z], axis=0)


OFF_FK = 0
OFF_SK = 384
OFF_DK2 = 768
OFF_IK2 = 896
OFF_SM = 1024
C_PACK = 1152
OFF_TFQ, OFF_TSQ, OFF_TDQ, OFF_TIQ = 0, 384, 704, 1024
OFF_TFV, OFF_TSV, OFF_TDV = 1536, 1920, 2240
R_PACK = 2304
CSTRIDE = PACKED_ROWS


def _swap_halves(h):
    lane = lax.broadcasted_iota(jnp.int32, h.shape, 1)
    first = (lane % HEAD_DIM) < HALF
    return jnp.where(first, pltpu.roll(h, LANE - HALF, 1), pltpu.roll(h, HALF, 1))


def _rope_rows(h, cos, sin):
    return h * cos + _swap_halves(h) * sin


def _rope_t(ht, cos_t, sin_t):
    parts = []
    for h in range(ht.shape[0] // HEAD_DIM):
        x = ht[h * HEAD_DIM:(h + 1) * HEAD_DIM]
        swapped = jnp.concatenate([x[HALF:], x[:HALF]], axis=0)
        parts.append(x * cos_t + swapped * sin_t)
    return jnp.concatenate(parts, axis=0)


def _proj_kernel(x_ref, w_ref, wt_ref, bias_ref, tab_ref, tabt_ref, place_ref, ones_ref,
                 fqt_ref, fqct_ref, fk_ref, fkc_ref, fvt_ref, sqt_ref, sk_ref, svt_ref,
                 dqt_ref, iqt_ref, dk2_ref, ik2_ref, dvt_ref, iwt_ref, carry_ref):
    tm = x_ref.shape[1]

    @pl.when(pl.program_id(1) == 0)
    def _():
        carry_ref[...] = jnp.zeros_like(carry_ref)

    xb = x_ref[0].astype(BF16)

    def seg(off, width):
        return _dot(xb, w_ref[:, off:off + width])

    def seg_t(off, nrows):
        return _dot_nt(wt_ref[off:off + nrows, :], xb)

    def put_chunks(ref, val, chunk):
        for r in range(tm // chunk):
            ref[0, r] = val[:, r * chunk:(r + 1) * chunk].astype(ref.dtype)

    small = seg(OFF_SM, LANE)
    logf = _log_sigmoid(small + bias_ref[...])

    row = lax.broadcasted_iota(jnp.int32, (tm, tm), 0)
    col = lax.broadcasted_iota(jnp.int32, (tm, tm), 1)
    tri = jnp.where(col <= row, 1.0, 0.0).astype(BF16)
    c = _dot(tri, jnp.concatenate(_split3(logf), axis=1))
    c = c[:, 0:LANE] + c[:, LANE:2 * LANE] + c[:, 2 * LANE:3 * LANE] + carry_ref[...]
    carry_ref[...] = c[tm - 1:tm, :]
    placed = _dot(jnp.concatenate(_split3(c), axis=1), place_ref[...]) + ones_ref[...]
    put_chunks(fqct_ref, placed[:, 0:LANE].T, TKC)
    fkc_ref[0] = placed[:, LANE:2 * LANE].astype(BF16)
    put_chunks(iwt_ref, small.T[IW_LANE:IW_LANE + IDX_HEADS], TQ_DSA)

    fk_ref[0] = seg(OFF_FK, W_FOX).astype(BF16)
    sk_ref[0] = seg(OFF_SK, W_SB_PAD)[:, :W_SB].astype(BF16)
    cos, sin = tab_ref[:, 0:LANE], tab_ref[:, LANE:2 * LANE]
    dk2_ref[0] = _rope_rows(seg(OFF_DK2, LANE), cos, sin).astype(BF16)
    ik2_ref[0] = _rope_rows(seg(OFF_IK2, LANE), cos, sin).astype(BF16)

    cos_t, sin_t = tabt_ref[0:HEAD_DIM, :], tabt_ref[HEAD_DIM:2 * HEAD_DIM, :]
    put_chunks(fqt_ref, seg_t(OFF_TFQ, W_FOX) * SCALE, TKC)
    put_chunks(sqt_ref, seg_t(OFF_TSQ, W_SB) * SCALE, TKC)
    put_chunks(dqt_ref, _rope_t(seg_t(OFF_TDQ, W_DSA), cos_t, sin_t) * SCALE, TQ_DSA)
    put_chunks(iqt_ref, _rope_t(seg_t(OFF_TIQ, W_IDX), cos_t, sin_t), TQ_DSA)
    put_chunks(fvt_ref, seg_t(OFF_TFV, W_FOX), TKC)
    put_chunks(svt_ref, seg_t(OFF_TSV, W_SB), TKC)
    put_chunks(dvt_ref, seg_t(OFF_TDV, HEAD_DIM), TKC)


def _proj(x, w_pack, wt_pack, bias_row, rope_tab, rope_tab_t, place, ones_row):
    b, s, d = x.shape
    tm = min(TM_PROJ, s)
    grid = (b, s // tm)

    def rows(width, dtype=BF16):
        return (jax.ShapeDtypeStruct((b, s, width), dtype),
                pl.BlockSpec((1, tm, width), lambda bi, si: (bi, si, 0)))

    def chunks(nrows, chunk, dtype=BF16):
        return (jax.ShapeDtypeStruct((b, s // chunk, nrows, chunk), dtype),
                pl.BlockSpec((1, tm // chunk, nrows, chunk), lambda bi, si: (bi, si, 0, 0)))

    def const(shape):
        return pl.BlockSpec(shape, lambda bi, si: (0,) * len(shape))

    outs = [chunks(W_FOX, TKC), chunks(LANE, TKC), rows(W_FOX), rows(LANE), chunks(W_FOX, TKC),
            chunks(W_SB, TKC), rows(W_SB), chunks(W_SB, TKC),
            chunks(W_DSA, TQ_DSA), chunks(W_IDX, TQ_DSA), rows(LANE), rows(LANE), chunks(HEAD_DIM, TKC),
            chunks(IDX_HEADS, TQ_DSA, F32)]
    return pl.pallas_call(
        _proj_kernel,
        grid=grid,
        in_specs=[pl.BlockSpec((1, tm, d), lambda bi, si: (bi, si, 0)),
                  const((d, C_PACK)), const((R_PACK, d)), const((1, LANE)),
                  pl.BlockSpec((tm, 2 * LANE), lambda bi, si: (si, 0)),
                  pl.BlockSpec((2 * HEAD_DIM, tm), lambda bi, si: (0, si)),
                  const(place.shape), const(ones_row.shape)],
        out_specs=[o[1] for o in outs],
        out_shape=[o[0] for o in outs],
        scratch_shapes=[pltpu.VMEM((1, LANE), F32)],
        compiler_params=pltpu.CompilerParams(
            dimension_semantics=("arbitrary", "arbitrary"), vmem_limit_bytes=VMEM_LIMIT),
        name="proj",
    )(x, w_pack, wt_pack, bias_row, rope_tab, rope_tab_t, place, ones_row)


def _fox_kernel(qt_ref, qct_ref, k_ref, kc_ref, vt_ref, o_ref, qcat_ref, m_ref, l_ref, acc_ref):
    tq = tk = TKC
    i = pl.program_id(1)
    krow = lax.broadcasted_iota(jnp.int32, (tk, tq), 0)
    qcol = lax.broadcasted_iota(jnp.int32, (tk, tq), 1)
    causal = krow <= qcol

    for h in range(HEADS_FOX):
        gate_rows = jnp.concatenate(
            ([jnp.zeros((CSTRIDE * h, tq), BF16)] if h else [])
            + [qct_ref[0, 0, CSTRIDE * h:CSTRIDE * (h + 1), :],
               jnp.zeros((LANE - CSTRIDE * (h + 1), tq), BF16)], axis=0)
        qcat_ref[h] = jnp.concatenate(
            [_pair_rows(qt_ref[0, 0, h * HEAD_DIM:(h + 1) * HEAD_DIM, :], h), gate_rows], axis=0)
    m_ref[...] = jnp.full(m_ref.shape, NEG, F32)
    l_ref[...] = jnp.zeros(l_ref.shape, F32)
    acc_ref[...] = jnp.zeros(acc_ref.shape, F32)

    def qk(js):
        kcs = [kc_ref[0, j * tk:(j + 1) * tk, :] for j in js]
        scores = []
        for g in range(HEADS_FOX // 2):
            kcats = [jnp.concatenate([k_ref[0, j * tk:(j + 1) * tk, g * LANE:(g + 1) * LANE], kc], axis=1)
                     for j, kc in zip(js, kcs)]
            for h in (2 * g, 2 * g + 1):
                scores.append([_dot(kcat, qcat_ref[h]) for kcat in kcats])
        return scores

    def softmax_pv(js, scores, masked):
        vts = [vt_ref[0, j] for j in js]
        for h in range(HEADS_FOX):
            ss = [jnp.where(causal, s, NEG) if masked else s for s in scores[h]]
            m_old = m_ref[h]
            m_new = m_old
            for s in ss:
                m_new = jnp.maximum(m_new, jnp.max(s, axis=0, keepdims=True))
            alpha = jnp.exp(m_old - m_new)
            ps = [jnp.exp(s - m_new) for s in ss]
            m_ref[h] = m_new
            l_ref[h] = alpha * l_ref[h] + sum(jnp.sum(p, axis=0, keepdims=True) for p in ps)
            acc_ref[h] = alpha * acc_ref[h] + sum(
                _dot(vt[h * HEAD_DIM:(h + 1) * HEAD_DIM, :], p.astype(BF16)) for vt, p in zip(vts, ps))

    for i_static in range(k_ref.shape[1] // tk):
        @pl.when(i == i_static)
        def _(i_static=i_static):
            groups = [((i_static,), True)] + [((j, j + 1), False) for j in range(0, i_static - 1, 2)]
            if i_static % 2:
                groups.append(((i_static - 1,), False))
            ahead = qk(groups[0][0])
            for n, (js, masked) in enumerate(groups):
                scores = ahead
                if n + 1 < len(groups):
                    ahead = qk(groups[n + 1][0])
                softmax_pv(js, scores, masked)

    o_ref[0] = jnp.concatenate([(acc_ref[h] / l_ref[h]).T for h in range(HEADS_FOX)],
                               axis=1).astype(o_ref.dtype)


def _fox(fqt, fqct, fk, fkc, fvt):
    b, s, w = fk.shape
    tq = TKC
    return pl.pallas_call(
        _fox_kernel,
        grid=(b, s // tq),
        in_specs=[pl.BlockSpec((1, 1, w, tq), lambda bi, qi: (bi, qi, 0, 0)),
                  pl.BlockSpec((1, 1, LANE, tq), lambda bi, qi: (bi, qi, 0, 0)),
                  pl.BlockSpec((1, s, w), lambda bi, qi: (bi, 0, 0)),
                  pl.BlockSpec((1, s, LANE), lambda bi, qi: (bi, 0, 0)),
                  pl.BlockSpec((1, s // TKC, w, TKC), lambda bi, qi: (bi, 0, 0, 0))],
        out_specs=pl.BlockSpec((1, tq, w), lambda bi, qi: (bi, qi, 0)),
        out_shape=jax.ShapeDtypeStruct((b, s, w), BF16),
        scratch_shapes=[pltpu.VMEM((HEADS_FOX, 2 * LANE, tq), BF16),
                        pltpu.VMEM((HEADS_FOX, 1, tq), F32),
                        pltpu.VMEM((HEADS_FOX, 1, tq), F32),
                        pltpu.VMEM((HEADS_FOX, HEAD_DIM, tq), F32)],
        compiler_params=pltpu.CompilerParams(
            dimension_semantics=("arbitrary", "arbitrary"), vmem_limit_bytes=VMEM_LIMIT),
        name="fox",
    )(fqt, fqct, fk, fkc, fvt)


SB_DEAD_LOG = -110.0

def _sb_kernel(qt_ref, k_ref, vt_ref, o_ref, qm_ref, later_w_ref, run_ref, acc_ref):
    tq = tk = TKC
    i = pl.program_id(1)
    krow = lax.broadcasted_iota(jnp.int32, (tk, tq), 0)
    qcol = lax.broadcasted_iota(jnp.int32, (tk, tq), 1)
    strict = krow < qcol
    r = lax.broadcasted_iota(jnp.int32, (tk, 2 * tk), 0)
    c = lax.broadcasted_iota(jnp.int32, (tk, 2 * tk), 1) % tk
    later_w_ref[...] = jnp.where(c > r, -1.0, 0.0).astype(BF16)
    for h in range(HEADS_SB):
        qm_ref[h] = _pair_rows(qt_ref[0, 0, h * HEAD_DIM:(h + 1) * HEAD_DIM, :], h)
    run_ref[...] = jnp.zeros(run_ref.shape, F32)
    acc_ref[...] = jnp.zeros(acc_ref.shape, F32)

    def block(j, masked):
        ks = pl.multiple_of(j * tk, tk)
        vt = vt_ref[0, j]
        zs = []
        for h in range(HEADS_SB):
            g = h // 2
            width = min(LANE, W_SB - g * LANE)
            kp = k_ref[0, pl.ds(ks, tk), g * LANE:g * LANE + width]
            zs.append(_dot(kp, qm_ref[h][:width, :]))
        sps = []
        for h in range(HEADS_SB):
            z = zs[h]
            sp = jnp.maximum(z, 0.0) + jnp.log(1.0 + jnp.exp(-jnp.abs(z)))
            sps.append(jnp.where(strict, sp, 0.0) if masked else sp)
        laters = []
        for h in range(HEADS_SB):
            sp_hi = sps[h].astype(BF16)
            sp_lo = (sps[h] - sp_hi.astype(F32)).astype(BF16)
            laters.append(_dot(later_w_ref[...], jnp.concatenate([sp_hi, sp_lo], axis=0)))
        for h in range(HEADS_SB):
            run = run_ref[h]
            a = jnp.exp(zs[h] - sps[h] + laters[h] + run)
            if masked:
                a = jnp.where(strict, a, 0.0)
            acc_ref[h] = acc_ref[h] + _dot(vt[h * HEAD_DIM:(h + 1) * HEAD_DIM, :], a.astype(BF16))
            run_ref[h] = run + laters[h][0:1, :] - sps[h][0:1, :]

    def alive():
        return jnp.max(run_ref[...]) > SB_DEAD_LOG

    block(i, True)

    def body(carry):
        t, _ = carry
        block(i - 1 - t, False)
        return t + 1, alive()

    lax.while_loop(lambda carry: (carry[0] < i) & carry[1], body, (jnp.int32(0), alive()))
    o_ref[0] = jnp.concatenate([acc_ref[h].T for h in range(HEADS_SB)], axis=1).astype(o_ref.dtype)


def _sb(sqt, sk, svt):
    b, s, w = sk.shape
    tq = TKC
    return pl.pallas_call(
        _sb_kernel,
        grid=(b, s // tq),
        in_specs=[pl.BlockSpec((1, 1, w, tq), lambda bi, qi: (bi, qi, 0, 0)),
                  pl.BlockSpec((1, s, w), lambda bi, qi: (bi, 0, 0)),
                  pl.BlockSpec((1, s // TKC, w, TKC), lambda bi, qi: (bi, 0, 0, 0))],
        out_specs=pl.BlockSpec((1, tq, w), lambda bi, qi: (bi, qi, 0)),
        out_shape=jax.ShapeDtypeStruct((b, s, w), BF16),
        scratch_shapes=[pltpu.VMEM((HEADS_SB, LANE, tq), BF16),
                        pltpu.VMEM((TKC, 2 * TKC), BF16),
                        pltpu.VMEM((HEADS_SB, 1, tq), F32),
                        pltpu.VMEM((HEADS_SB, HEAD_DIM, tq), F32)],
        compiler_params=pltpu.CompilerParams(
            dimension_semantics=("arbitrary", "arbitrary"), vmem_limit_bytes=VMEM_LIMIT),
        name="sb",
    )(sqt, sk, svt)


NEG_KEY = -0x7149F2CB


def _sortable(score):
    bits = lax.bitcast_convert_type(score, jnp.int32)
    return jnp.where(bits < 0, bits ^ jnp.int32(0x7FFFFFFF), bits)


def _sum_sublane_groups(x):
    parts = [x[r * SUBLANE:(r + 1) * SUBLANE] for r in range(x.shape[0] // SUBLANE)]
    while len(parts) > 1:
        parts = [parts[k] + parts[k + 1] for k in range(0, len(parts) - 1, 2)] + (
            [parts[-1]] if len(parts) % 2 else [])
    return parts[0]


def _tree_sum(parts):
    parts = list(parts)
    while len(parts) > 1:
        parts = [parts[k] + parts[k + 1] for k in range(0, len(parts) - 1, 2)] + (
            [parts[-1]] if len(parts) % 2 else [])
    return parts[0]


def _count_ge16(ref, nkb, cand):
    tk, tq = ref.shape[1], ref.shape[2]
    cand16 = jnp.broadcast_to(cand, (PACKED_ROWS, tq)).astype(jnp.int16)
    one, zero = jnp.ones((PACKED_ROWS, tq), jnp.int16), jnp.zeros((PACKED_ROWS, tq), jnp.int16)
    parts = []
    for j in range(nkb):
        blk = ref[j]
        parts += [jnp.where(blk[r * PACKED_ROWS:(r + 1) * PACKED_ROWS] >= cand16, one, zero)
                  for r in range(tk // PACKED_ROWS)]
    return jnp.sum(_tree_sum(parts).astype(F32), axis=0, keepdims=True)


def _search16(ref, nkb, n_sel):
    tq = ref.shape[2]

    def bit_step(t, thr_u):
        bit = lax.shift_left(jnp.int32(1), jnp.int32(15) - t)
        cand_u = thr_u | bit
        cnt = _count_ge16(ref, nkb, cand_u - 32768)
        return jnp.where(cnt >= n_sel, cand_u, thr_u)

    return lax.fori_loop(0, 16, bit_step, jnp.zeros((1, tq), jnp.int32)) - 32768


def _dsa_kernel(dqt_ref, iqt_ref, iwt_ref, dk2_ref, ik2_ref, dvt_ref, o_ref,
                key_ref, hi_ref, lo_ref, bias_ref, iq8_ref, q5_ref, jmax_ref, *, n_sel, idx_bits):
    tq, tk = TQ_DSA, TK_DSA
    i = pl.program_id(1)
    kloc = lax.broadcasted_iota(jnp.int32, (tk, tq), 0)
    qpos = i * tq + lax.broadcasted_iota(jnp.int32, (tk, tq), 1)
    klimit = (qpos // CHUNK + 1) * CHUNK
    for h in range(IDX_HEADS):
        iq8_ref[:, h * tq:(h + 1) * tq] = _pair_rows(iqt_ref[0, 0, h * HEAD_DIM:(h + 1) * HEAD_DIM, :], h)
    for h in range(HEADS_DSA):
        q5_ref[:, h * tq:(h + 1) * tq] = _pair_rows(dqt_ref[0, 0, h * HEAD_DIM:(h + 1) * HEAD_DIM, :], h)
    iwt = iwt_ref[0, 0]

    def select(nkb):
        for j in range(nkb):
            a = _dot(ik2_ref[0, j * tk:(j + 1) * tk, :], iq8_ref[...])
            sc = jnp.zeros((tk, tq), F32)
            for h in range(IDX_HEADS):
                sc = sc + iwt[h:h + 1, :] * jnp.maximum(a[:, h * tq:(h + 1) * tq], 0.0)
            sc = sc + 0.0
            sc = jnp.where(kloc + j * tk < klimit, sc, NEG)
            key = _sortable(sc)
            key_ref[j] = key
            hi_ref[j] = lax.shift_right_arithmetic(key, jnp.int32(16)).astype(jnp.int16)
            lo_ref[j] = ((key & jnp.int32(0xFFFF)) - 32768).astype(jnp.int16)

        thr_hi = _search16(hi_ref, nkb, n_sel)
        thr_hi16 = jnp.broadcast_to(thr_hi, (PACKED_ROWS, tq)).astype(jnp.int16)
        for j in range(nkb):
            for r in range(tk // PACKED_ROWS):
                rows = slice(r * PACKED_ROWS, (r + 1) * PACKED_ROWS)
                hi = hi_ref[j, rows, :]
                pinned = jnp.where(hi > thr_hi16, jnp.int16(32767), jnp.int16(-32768))
                lo_ref[j, rows, :] = jnp.where(hi == thr_hi16, lo_ref[j, rows, :], pinned)
        thr = thr_hi * 65536 + (_search16(lo_ref, nkb, n_sel) + 32768)

        def count(pred):
            parts = [_sum_sublane_groups(jnp.where(pred(key_ref[j]), 1.0, 0.0)) for j in range(nkb)]
            return jnp.sum(_tree_sum(parts), axis=0, keepdims=True)

        cnt_ge = count(lambda k: k >= thr)
        jmax_ref[...] = jnp.full((1, tq), 2 ** idx_bits - 1, jnp.int32)
        tied = (cnt_ge > n_sel) & (thr > NEG_KEY)

        @pl.when(jnp.max(jnp.where(tied, 1.0, 0.0)) > 0.0)
        def _():
            need = n_sel - count(lambda k: k > thr)
            for j in range(nkb):
                lo_ref[j] = jnp.where(key_ref[j] == thr, kloc + j * tk, 32767).astype(jnp.int16)
            stored = float(nkb * tk)

            def idx_bit(t, jmax):
                bit = lax.shift_left(jnp.int32(1), jnp.int32(idx_bits - 1) - t)
                cand = jmax | bit
                cnt = stored - _count_ge16(lo_ref, nkb, cand)
                return jnp.where(cnt < need, cand, jmax)

            jmax_ref[...] = lax.fori_loop(0, idx_bits, idx_bit, jnp.zeros((1, tq), jnp.int32))

        jmax = jmax_ref[...]
        for j in range(nkb):
            k = key_ref[j]
            pos = kloc + j * tk
            sel = ((k > thr) | ((k == thr) & (pos <= jmax))) & (pos < klimit)
            bias_ref[j] = jnp.where(sel, 0.0, NEG)

    def attend(nkb):
        def qk(j):
            return _dot(dk2_ref[0, j * tk:(j + 1) * tk, :], q5_ref[...])

        m = jnp.full((1, HEADS_DSA * tq), NEG, F32)
        l = jnp.zeros((1, HEADS_DSA * tq), F32)
        acc = jnp.zeros((HEAD_DIM, HEADS_DSA * tq), F32)
        ahead = qk(0)
        for j in range(nkb):
            s = ahead + jnp.concatenate([bias_ref[j]] * HEADS_DSA, axis=1)
            if j + 1 < nkb:
                ahead = qk(j + 1)
            m_new = jnp.maximum(m, jnp.max(s, axis=0, keepdims=True))
            alpha = jnp.exp(m - m_new)
            p = jnp.exp(s - m_new)
            m = m_new
            l = alpha * l + jnp.sum(p, axis=0, keepdims=True)
            acc = alpha * acc + _dot(dvt_ref[0, j], p.astype(BF16))
        out = acc / l
        o_ref[0] = jnp.concatenate([out[:, h * tq:(h + 1) * tq].T for h in range(HEADS_DSA)],
                                   axis=1).astype(o_ref.dtype)

    for i_static in range(dk2_ref.shape[1] // tq):
        @pl.when(i == i_static)
        def _(i_static=i_static):
            nkb = (i_static * tq + tq - 1) // tk + 1
            if nkb * tk <= n_sel:
                for j in range(nkb):
                    bias_ref[j] = jnp.where(kloc + j * tk < klimit, 0.0, NEG)
            else:
                select(nkb)
            attend(nkb)


def _dsa(dqt, iqt, iwt, dk2, ik2, dvt):
    b, s, _ = dk2.shape
    tq, tk = TQ_DSA, TK_DSA
    n_sel = min(DSA_TOPK_MAX, s // 4)
    assert n_sel <= tk and s % tk == 0 and tk % tq == 0 and s < 2 ** 15
    idx_bits = max(1, (s - 1).bit_length())
    return pl.pallas_call(
        functools.partial(_dsa_kernel, n_sel=n_sel, idx_bits=idx_bits),
        grid=(b, s // tq),
        in_specs=[pl.BlockSpec((1, 1, W_DSA, tq), lambda bi, qi: (bi, qi, 0, 0)),
                  pl.BlockSpec((1, 1, W_IDX, tq), lambda bi, qi: (bi, qi, 0, 0)),
                  pl.BlockSpec((1, 1, IDX_HEADS, tq), lambda bi, qi: (bi, qi, 0, 0)),
                  pl.BlockSpec((1, s, LANE), lambda bi, qi: (bi, 0, 0)),
                  pl.BlockSpec((1, s, LANE), lambda bi, qi: (bi, 0, 0)),
                  pl.BlockSpec((1, s // tk, HEAD_DIM, tk), lambda bi, qi: (bi, 0, 0, 0))],
        out_specs=pl.BlockSpec((1, tq, W_DSA), lambda bi, qi: (bi, qi, 0)),
        out_shape=jax.ShapeDtypeStruct((b, s, W_DSA), BF16),
        scratch_shapes=[pltpu.VMEM((s // tk, tk, tq), jnp.int32),
                        pltpu.VMEM((s // tk, tk, tq), jnp.int16),
                        pltpu.VMEM((s // tk, tk, tq), jnp.int16),
                        pltpu.VMEM((s // tk, tk, tq), F32),
                        pltpu.VMEM((LANE, IDX_HEADS * tq), BF16),
                        pltpu.VMEM((LANE, HEADS_DSA * tq), BF16),
                        pltpu.VMEM((1, tq), jnp.int32)],
        compiler_params=pltpu.CompilerParams(
            dimension_semantics=("arbitrary", "arbitrary"), vmem_limit_bytes=VMEM_LIMIT),
        name="dsa",
    )(dqt, iqt, iwt, dk2, ik2, dvt)


def _layer_norm(y, g, b):
    mu = jnp.mean(y, axis=1, keepdims=True)
    d = y - mu
    var = jnp.mean(d * d, axis=1, keepdims=True)
    return d * lax.rsqrt(var + LN_EPS) * g + b


def _merge_kernel(x_ref, of_ref, os_ref, od_ref, wg_ref, wuf_ref, wus_ref, wud_ref, wo_ref,
                  g_ref, b_ref, o_ref):
    half = x_ref.shape[0] // MLP_SPLIT
    for r in range(MLP_SPLIT):
        rows = slice(r * half, (r + 1) * half)
        x = x_ref[rows, :]
        xb = x.astype(BF16)
        merged = None
        for n, (br_ref, wu_ref) in enumerate(((of_ref, wuf_ref), (os_ref, wus_ref), (od_ref, wud_ref))):
            gate = jax.nn.sigmoid(_dot(xb, wg_ref[:, n * D_MODEL:(n + 1) * D_MODEL]))
            term = gate * _dot(br_ref[rows, :], wu_ref[...])
            merged = term if merged is None else merged + term
        y = DEEPNORM_ALPHA * x + _dot(merged.astype(BF16), wo_ref[...])
        o_ref[rows, :] = _layer_norm(y, g_ref[...], b_ref[...])


def _const_spec(shape):
    return pl.BlockSpec(shape, lambda i: (0,) * len(shape), pipeline_mode=pl.Buffered(1))


def _merge(x2, o_fox, o_sb, o_dsa, wg, wuf, wus, wud, wo, g, bvec):
    t = x2.shape[0]
    tm = min(TM_MLP, t)

    def rows(width):
        return pl.BlockSpec((tm, width), lambda i: (i, 0))

    return pl.pallas_call(
        _merge_kernel,
        grid=(t // tm,),
        in_specs=[rows(D_MODEL), rows(W_FOX), rows(W_SB), rows(W_DSA),
                  _const_spec(wg.shape), _const_spec(wuf.shape), _const_spec(wus.shape),
                  _const_spec(wud.shape), _const_spec(wo.shape),
                  _const_spec(g.shape), _const_spec(bvec.shape)],
        out_specs=rows(D_MODEL),
        out_shape=jax.ShapeDtypeStruct((t, D_MODEL), F32),
        compiler_params=pltpu.CompilerParams(
            dimension_semantics=("arbitrary",), vmem_limit_bytes=VMEM_LIMIT),
        name="merge",
    )(x2, o_fox, o_sb, o_dsa, wg, wuf, wus, wud, wo, g, bvec)


FF_CHUNK = 1024


def _ffn_kernel(x_ref, p_ref, w1_ref, w2_ref, wpg_ref, wp_ref, g_ref, b_ref, o_ref):
    half = x_ref.shape[0] // MLP_SPLIT
    for r in range(MLP_SPLIT):
        rows = slice(r * half, (r + 1) * half)
        x = x_ref[rows, :]
        xb = x.astype(BF16)
        acc = None
        for c in range(D_FF // FF_CHUNK):
            h = jnp.maximum(_dot(xb, w1_ref[:, c * FF_CHUNK:(c + 1) * FF_CHUNK]), 0.0)
            term = _dot((h * h).astype(BF16), w2_ref[c * FF_CHUNK:(c + 1) * FF_CHUNK, :])
            acc = term if acc is None else acc + term
        ple = jax.nn.sigmoid(_dot(xb, wpg_ref[...])) * _dot(p_ref[rows, :].astype(BF16), wp_ref[...])
        y = DEEPNORM_ALPHA * x + acc + ple
        o_ref[rows, :] = _layer_norm(y, g_ref[...], b_ref[...])


def _ffn(x2, p2, w1, w2, wpg, wp, g, bvec):
    t = x2.shape[0]
    tm = min(TM_MLP, t)

    def rows(width):
        return pl.BlockSpec((tm, width), lambda i: (i, 0))

    return pl.pallas_call(
        _ffn_kernel,
        grid=(t // tm,),
        in_specs=[rows(D_MODEL), rows(PLE_DIM),
                  _const_spec(w1.shape), _const_spec(w2.shape), _const_spec(wpg.shape),
                  _const_spec(wp.shape), _const_spec(g.shape), _const_spec(bvec.shape)],
        out_specs=rows(D_MODEL),
        out_shape=jax.ShapeDtypeStruct((t, D_MODEL), F32),
        compiler_params=pltpu.CompilerParams(
            dimension_semantics=("arbitrary",), vmem_limit_bytes=VMEM_LIMIT),
        name="ffn",
    )(x2, p2, w1, w2, wpg, wp, g, bvec)


def _pack_w_in(w):
    d = w.shape[0]
    offs = {}
    o = 0
    for name, width in (("fq", W_FOX), ("fk", W_FOX), ("fv", W_FOX), ("ff", HEADS_FOX),
                        ("sq", W_SB), ("sk", W_SB), ("sv", W_SB),
                        ("dq", W_DSA), ("dk", HEAD_DIM), ("dv", HEAD_DIM),
                        ("iq", W_IDX), ("ik", HEAD_DIM), ("iw", IDX_HEADS),
                        ("g", N_BRANCH * D_MODEL)):
        offs[name] = w[:, o:o + width]
        o += width

    def z(n):
        return jnp.zeros((d, n), w.dtype)

    packed = jnp.concatenate(
        [offs["fk"], offs["sk"], z(W_SB_PAD - W_SB), offs["dk"], offs["dk"], offs["ik"], offs["ik"],
         offs["ff"], z(IW_LANE - HEADS_FOX), offs["iw"], z(LANE - IW_LANE - IDX_HEADS)], axis=1)
    assert packed.shape[1] == C_PACK
    packed_t = jnp.concatenate([offs["fq"], offs["sq"], offs["dq"], offs["iq"],
                                offs["fv"], offs["sv"], offs["dv"]], axis=1).T
    assert packed_t.shape[0] == R_PACK
    return packed.astype(BF16), packed_t.astype(BF16), offs["g"].astype(BF16)


def _rope_tables(s):
    inv = ROPE_THETA ** (-jnp.arange(HALF, dtype=F32) / HALF)
    ang = jnp.arange(s, dtype=F32)[:, None] * inv[None, :]
    cos, sin = jnp.cos(ang), jnp.sin(ang)
    row_major = jnp.concatenate([cos, cos, cos, cos, -sin, sin, -sin, sin], axis=1)
    transposed = jnp.concatenate([cos, cos, -sin, sin], axis=1).T
    return row_major, transposed


def _placement():
    rows_q, cols_q, rows_k, cols_k = [], [], [], []
    for h in range(HEADS_FOX):
        for t in range(N_CTERM):
            rows_q.append(t * LANE + h), cols_q.append(CSTRIDE * h + t)
            rows_k.append(t * LANE + h), cols_k.append(LANE + CSTRIDE * h + N_CTERM + t)
    place = jnp.zeros((N_CTERM * LANE, 2 * LANE), F32)
    place = place.at[jnp.array(rows_q), jnp.array(cols_q)].set(1.0)
    place = place.at[jnp.array(rows_k), jnp.array(cols_k)].set(-1.0)
    lane = jnp.arange(2 * LANE)
    head_lane = (lane % LANE) % CSTRIDE
    in_heads = (lane % LANE) < CSTRIDE * HEADS_FOX
    ones_q = (lane < LANE) & in_heads & (head_lane >= N_CTERM) & (head_lane < 2 * N_CTERM)
    ones_k = (lane >= LANE) & in_heads & (head_lane < N_CTERM)
    ones_row = jnp.where(ones_q | ones_k, 1.0, 0.0).astype(F32)[None, :]
    return place.astype(BF16), ones_row


def kernel(x, p, w_in, b_forget, w_up_fox, w_up_sb, w_up_dsa, w_out, ln1_g, ln1_b,
           w_ff_in, w_ff_out, w_ple, w_ple_gate, ln2_g, ln2_b):
    b, s, d = x.shape
    depth = w_in.shape[0]
    rope_tab, rope_tab_t = _rope_tables(s)
    place, ones_row = _placement()
    w_in_rows = w_in.reshape(depth * d, w_in.shape[2])
    for i in range(depth):
        w_pack, wt_pack, w_gate = _pack_w_in(w_in_rows[i * d:(i + 1) * d])
        bias_row = jnp.zeros((1, LANE), F32).at[0, :HEADS_FOX].set(b_forget[i].astype(F32))
        (fqt, fqct, fk, fkc, fvt, sqt, sk, svt, dqt, iqt, dk2, ik2, dvt, iwt) = _proj(
            x, w_pack, wt_pack, bias_row, rope_tab, rope_tab_t, place, ones_row)
        o_fox = _fox(fqt, fqct, fk, fkc, fvt)
        o_sb = _sb(sqt, sk, svt)
        o_dsa = _dsa(dqt, iqt, iwt, dk2, ik2, dvt)
        x1 = _merge(x.reshape(b * s, d), o_fox.reshape(b * s, W_FOX), o_sb.reshape(b * s, W_SB),
                    o_dsa.reshape(b * s, W_DSA), w_gate,
                    w_up_fox[i].astype(BF16), w_up_sb[i].astype(BF16), w_up_dsa[i].astype(BF16),
                    w_out[i].astype(BF16), ln1_g[i][None, :], ln1_b[i][None, :])
        x2 = _ffn(x1, p[i].reshape(b * s, PLE_DIM), w_ff_in[i].astype(BF16), w_ff_out[i].astype(BF16),
                  w_ple_gate[i].astype(BF16), w_ple[i].astype(BF16), ln2_g[i][None, :], ln2_b[i][None, :])
        x = x2.reshape(b, s, d)
    return x
```

```python
import functools

import jax
import jax.numpy as jnp
from jax import lax
from jax.experimental import pallas as pl
from jax.experimental.pallas import tpu as pltpu

D_MODEL = 1024
HEAD_DIM = 64
HALF = HEAD_DIM // 2
HEADS_FOX = 6
HEADS_SB = 5
HEADS_DSA = 5
IDX_HEADS = 8
CHUNK = 64
DSA_TOPK_MAX = 256
D_FF = 4 * D_MODEL
PLE_DIM = 256
ROPE_THETA = 10000.0
LN_EPS = 1e-5
N_BRANCH = 3
NEG = -1e30
DEPTH = 2
DEEPNORM_ALPHA = (2 * DEPTH) ** 0.25
W_FOX = HEADS_FOX * HEAD_DIM
W_SB = HEADS_SB * HEAD_DIM
W_DSA = HEADS_DSA * HEAD_DIM
W_IDX = IDX_HEADS * HEAD_DIM
SCALE = HEAD_DIM ** -0.5

LANE = 128
SUBLANE = 8
PACKED_ROWS = 16
W_SB_PAD = 384
IW_LANE = 8
N_CTERM = 3
VMEM_LIMIT = 56 * 1024 * 1024

TM_PROJ = 512
TKC = 256
TQ_DSA = 256
TK_DSA = TKC
TM_MLP = 1024
MLP_SPLIT = 2

BF16 = jnp.bfloat16
F32 = jnp.float32


def _log_sigmoid(x):
    return jnp.minimum(x, 0.0) - jnp.log(1.0 + jnp.exp(-jnp.abs(x)))


def _dot(a, b):
    return jnp.dot(a, b, preferred_element_type=F32)


def _dot_nt(a, b):
    return lax.dot_general(a, b, (((1,), (1,)), ((), ())), preferred_element_type=F32)


def _split3(v):
    hi = v.astype(BF16)
    r = v - hi.astype(F32)
    mid = r.astype(BF16)
    lo = (r - mid.astype(F32)).astype(BF16)
    return hi, mid, lo


def _pair_rows(qt_h, h):
    z = jnp.zeros_like(qt_h)
    return jnp.concatenate([z, qt_h] if h % 2 else [qt_h, z], axis=0)


OFF_FK = 0
OFF_SK = 384
OFF_DK2 = 768
OFF_IK2 = 896
OFF_SM = 1024
C_PACK = 1152
OFF_TFQ, OFF_TSQ, OFF_TDQ, OFF_TIQ = 0, 384, 704, 1024
OFF_TFV, OFF_TSV, OFF_TDV = 1536, 1920, 2240
R_PACK = 2304
CSTRIDE = PACKED_ROWS


def _swap_halves(h):
    lane = lax.broadcasted_iota(jnp.int32, h.shape, 1)
    first = (lane % HEAD_DIM) < HALF
    return jnp.where(first, pltpu.roll(h, LANE - HALF, 1), pltpu.roll(h, HALF, 1))


def _rope_rows(h, cos, sin):
    return h * cos + _swap_halves(h) * sin


def _rope_t(ht, cos_t, sin_t):
    parts = []
    for h in range(ht.shape[0] // HEAD_DIM):
        x = ht[h * HEAD_DIM:(h + 1) * HEAD_DIM]
        swapped = jnp.concatenate([x[HALF:], x[:HALF]], axis=0)
        parts.append(x * cos_t + swapped * sin_t)
    return jnp.concatenate(parts, axis=0)


def _proj_kernel(x_ref, w_ref, wt_ref, bias_ref, tab_ref, tabt_ref, place_ref, ones_ref,
                 fqt_ref, fqct_ref, fk_ref, fkc_ref, fvt_ref, sqt_ref, sk_ref, svt_ref,
                 dqt_ref, iqt_ref, dk2_ref, ik2_ref, dvt_ref, iwt_ref, carry_ref):
    tm = x_ref.shape[1]

    @pl.when(pl.program_id(1) == 0)
    def _():
        carry_ref[...] = jnp.zeros_like(carry_ref)

    xb = x_ref[0].astype(BF16)

    def seg(off, width):
        return _dot(xb, w_ref[:, off:off + width])

    def seg_t(off, nrows):
        return _dot_nt(wt_ref[off:off + nrows, :], xb)

    def put_chunks(ref, val, chunk):
        for r in range(tm // chunk):
            ref[0, r] = val[:, r * chunk:(r + 1) * chunk].astype(ref.dtype)

    small = seg(OFF_SM, LANE)
    logf = _log_sigmoid(small + bias_ref[...])

    row = lax.broadcasted_iota(jnp.int32, (tm, tm), 0)
    col = lax.broadcasted_iota(jnp.int32, (tm, tm), 1)
    tri = jnp.where(col <= row, 1.0, 0.0).astype(BF16)
    c = _dot(tri, jnp.concatenate(_split3(logf), axis=1))
    c = c[:, 0:LANE] + c[:, LANE:2 * LANE] + c[:, 2 * LANE:3 * LANE] + carry_ref[...]
    carry_ref[...] = c[tm - 1:tm, :]
    placed = _dot(jnp.concatenate(_split3(c), axis=1), place_ref[...]) + ones_ref[...]
    put_chunks(fqct_ref, placed[:, 0:LANE].T, TKC)
    fkc_ref[0] = placed[:, LANE:2 * LANE].astype(BF16)
    put_chunks(iwt_ref, small.T[IW_LANE:IW_LANE + IDX_HEADS], TQ_DSA)

    fk_ref[0] = seg(OFF_FK, W_FOX).astype(BF16)
    sk_ref[0] = seg(OFF_SK, W_SB_PAD)[:, :W_SB].astype(BF16)
    cos, sin = tab_ref[:, 0:LANE], tab_ref[:, LANE:2 * LANE]
    dk2_ref[0] = _rope_rows(seg(OFF_DK2, LANE), cos, sin).astype(BF16)
    ik2_ref[0] = _rope_rows(seg(OFF_IK2, LANE), cos, sin).astype(BF16)

    cos_t, sin_t = tabt_ref[0:HEAD_DIM, :], tabt_ref[HEAD_DIM:2 * HEAD_DIM, :]
    put_chunks(fqt_ref, seg_t(OFF_TFQ, W_FOX) * SCALE, TKC)
    put_chunks(sqt_ref, seg_t(OFF_TSQ, W_SB) * SCALE, TKC)
    put_chunks(dqt_ref, _rope_t(seg_t(OFF_TDQ, W_DSA), cos_t, sin_t) * SCALE, TQ_DSA)
    put_chunks(iqt_ref, _rope_t(seg_t(OFF_TIQ, W_IDX), cos_t, sin_t), TQ_DSA)
    put_chunks(fvt_ref, seg_t(OFF_TFV, W_FOX), TKC)
    put_chunks(svt_ref, seg_t(OFF_TSV, W_SB), TKC)
    put_chunks(dvt_ref, seg_t(OFF_TDV, HEAD_DIM), TKC)


def _proj(x, w_pack, wt_pack, bias_row, rope_tab, rope_tab_t, place, ones_row):
    b, s, d = x.shape
    tm = min(TM_PROJ, s)
    grid = (b, s // tm)

    def rows(width, dtype=BF16):
        return (jax.ShapeDtypeStruct((b, s, width), dtype),
                pl.BlockSpec((1, tm, width), lambda bi, si: (bi, si, 0)))

    def chunks(nrows, chunk, dtype=BF16):
        return (jax.ShapeDtypeStruct((b, s // chunk, nrows, chunk), dtype),
                pl.BlockSpec((1, tm // chunk, nrows, chunk), lambda bi, si: (bi, si, 0, 0)))

    def const(shape):
        return pl.BlockSpec(shape, lambda bi, si: (0,) * len(shape))

    outs = [chunks(W_FOX, TKC), chunks(LANE, TKC), rows(W_FOX), rows(LANE), chunks(W_FOX, TKC),
            chunks(W_SB, TKC), rows(W_SB), chunks(W_SB, TKC),
            chunks(W_DSA, TQ_DSA), chunks(W_IDX, TQ_DSA), rows(LANE), rows(LANE), chunks(HEAD_DIM, TKC),
            chunks(IDX_HEADS, TQ_DSA, F32)]
    return pl.pallas_call(
        _proj_kernel,
        grid=grid,
        in_specs=[pl.BlockSpec((1, tm, d), lambda bi, si: (bi, si, 0)),
                  const((d, C_PACK)), const((R_PACK, d)), const((1, LANE)),
                  pl.BlockSpec((tm, 2 * LANE), lambda bi, si: (si, 0)),
                  pl.BlockSpec((2 * HEAD_DIM, tm), lambda bi, si: (0, si)),
                  const(place.shape), const(ones_row.shape)],
        out_specs=[o[1] for o in outs],
        out_shape=[o[0] for o in outs],
        scratch_shapes=[pltpu.VMEM((1, LANE), F32)],
        compiler_params=pltpu.CompilerParams(
            dimension_semantics=("arbitrary", "arbitrary"), vmem_limit_bytes=VMEM_LIMIT),
        name="proj",
    )(x, w_pack, wt_pack, bias_row, rope_tab, rope_tab_t, place, ones_row)


def _fox_kernel(qt_ref, qct_ref, k_ref, kc_ref, vt_ref, o_ref, qcat_ref, m_ref, l_ref, acc_ref):
    tq = tk = TKC
    i = pl.program_id(1)
    krow = lax.broadcasted_iota(jnp.int32, (tk, tq), 0)
    qcol = lax.broadcasted_iota(jnp.int32, (tk, tq), 1)
    causal = krow <= qcol

    for h in range(HEADS_FOX):
        gate_rows = jnp.concatenate(
            ([jnp.zeros((CSTRIDE * h, tq), BF16)] if h else [])
            + [qct_ref[0, 0, CSTRIDE * h:CSTRIDE * (h + 1), :],
               jnp.zeros((LANE - CSTRIDE * (h + 1), tq), BF16)], axis=0)
        qcat_ref[h] = jnp.concatenate(
            [_pair_rows(qt_ref[0, 0, h * HEAD_DIM:(h + 1) * HEAD_DIM, :], h), gate_rows], axis=0)
    m_ref[...] = jnp.full(m_ref.shape, NEG, F32)
    l_ref[...] = jnp.zeros(l_ref.shape, F32)
    acc_ref[...] = jnp.zeros(acc_ref.shape, F32)

    def qk(js):
        kcs = [kc_ref[0, j * tk:(j + 1) * tk, :] for j in js]
        scores = []
        for g in range(HEADS_FOX // 2):
            kcats = [jnp.concatenate([k_ref[0, j * tk:(j + 1) * tk, g * LANE:(g + 1) * LANE], kc], axis=1)
                     for j, kc in zip(js, kcs)]
            for h in (2 * g, 2 * g + 1):
                scores.append([_dot(kcat, qcat_ref[h]) for kcat in kcats])
        return scores

    def softmax_pv(js, scores, masked):
        vts = [vt_ref[0, j] for j in js]
        for h in range(HEADS_FOX):
            ss = [jnp.where(causal, s, NEG) if masked else s for s in scores[h]]
            m_old = m_ref[h]
            m_new = m_old
            for s in ss:
                m_new = jnp.maximum(m_new, jnp.max(s, axis=0, keepdims=True))
            alpha = jnp.exp(m_old - m_new)
            ps = [jnp.exp(s - m_new) for s in ss]
            m_ref[h] = m_new
            l_ref[h] = alpha * l_ref[h] + sum(jnp.sum(p, axis=0, keepdims=True) for p in ps)
            acc_ref[h] = alpha * acc_ref[h] + sum(
                _dot(vt[h * HEAD_DIM:(h + 1) * HEAD_DIM, :], p.astype(BF16)) for vt, p in zip(vts, ps))

    for i_static in range(k_ref.shape[1] // tk):
        @pl.when(i == i_static)
        def _(i_static=i_static):
            groups = [((i_static,), True)] + [((j, j + 1), False) for j in range(0, i_static - 1, 2)]
            if i_static % 2:
                groups.append(((i_static - 1,), False))
            ahead = qk(groups[0][0])
            for n, (js, masked) in enumerate(groups):
                scores = ahead
                if n + 1 < len(groups):
                    ahead = qk(groups[n + 1][0])
                softmax_pv(js, scores, masked)

    o_ref[0] = jnp.concatenate([(acc_ref[h] / l_ref[h]).T for h in range(HEADS_FOX)],
                               axis=1).astype(o_ref.dtype)


def _fox(fqt, fqct, fk, fkc, fvt):
    b, s, w = fk.shape
    tq = TKC
    return pl.pallas_call(
        _fox_kernel,
        grid=(b, s // tq),
        in_specs=[pl.BlockSpec((1, 1, w, tq), lambda bi, qi: (bi, qi, 0, 0)),
                  pl.BlockSpec((1, 1, LANE, tq), lambda bi, qi: (bi, qi, 0, 0)),
                  pl.BlockSpec((1, s, w), lambda bi, qi: (bi, 0, 0)),
                  pl.BlockSpec((1, s, LANE), lambda bi, qi: (bi, 0, 0)),
                  pl.BlockSpec((1, s // TKC, w, TKC), lambda bi, qi: (bi, 0, 0, 0))],
        out_specs=pl.BlockSpec((1, tq, w), lambda bi, qi: (bi, qi, 0)),
        out_shape=jax.ShapeDtypeStruct((b, s, w), BF16),
        scratch_shapes=[pltpu.VMEM((HEADS_FOX, 2 * LANE, tq), BF16),
                        pltpu.VMEM((HEADS_FOX, 1, tq), F32),
                        pltpu.VMEM((HEADS_FOX, 1, tq), F32),
                        pltpu.VMEM((HEADS_FOX, HEAD_DIM, tq), F32)],
        compiler_params=pltpu.CompilerParams(
            dimension_semantics=("arbitrary", "arbitrary"), vmem_limit_bytes=VMEM_LIMIT),
        name="fox",
    )(fqt, fqct, fk, fkc, fvt)


SB_DEAD_LOG = -110.0

def _sb_kernel(qt_ref, k_ref, vt_ref, o_ref, qm_ref, later_w_ref, run_ref, acc_ref):
    tq = tk = TKC
    i = pl.program_id(1)
    krow = lax.broadcasted_iota(jnp.int32, (tk, tq), 0)
    qcol = lax.broadcasted_iota(jnp.int32, (tk, tq), 1)
    strict = krow < qcol
    r = lax.broadcasted_iota(jnp.int32, (tk, 2 * tk), 0)
    c = lax.broadcasted_iota(jnp.int32, (tk, 2 * tk), 1) % tk
    later_w_ref[...] = jnp.where(c > r, -1.0, 0.0).astype(BF16)
    for h in range(HEADS_SB):
        qm_ref[h] = _pair_rows(qt_ref[0, 0, h * HEAD_DIM:(h + 1) * HEAD_DIM, :], h)
    run_ref[...] = jnp.zeros(run_ref.shape, F32)
    acc_ref[...] = jnp.zeros(acc_ref.shape, F32)

    def block(j, masked):
        ks = pl.multiple_of(j * tk, tk)
        vt = vt_ref[0, j]
        zs = []
        for h in range(HEADS_SB):
            g = h // 2
            width = min(LANE, W_SB - g * LANE)
            kp = k_ref[0, pl.ds(ks, tk), g * LANE:g * LANE + width]
            zs.append(_dot(kp, qm_ref[h][:width, :]))
        sps = []
        for h in range(HEADS_SB):
            z = zs[h]
            sp = jnp.maximum(z, 0.0) + jnp.log(1.0 + jnp.exp(-jnp.abs(z)))
            sps.append(jnp.where(strict, sp, 0.0) if masked else sp)
        laters = []
        for h in range(HEADS_SB):
            sp_hi = sps[h].astype(BF16)
            sp_lo = (sps[h] - sp_hi.astype(F32)).astype(BF16)
            laters.append(_dot(later_w_ref[...], jnp.concatenate([sp_hi, sp_lo], axis=0)))
        for h in range(HEADS_SB):
            run = run_ref[h]
            a = jnp.exp(zs[h] - sps[h] + laters[h] + run)
            if masked:
                a = jnp.where(strict, a, 0.0)
            acc_ref[h] = acc_ref[h] + _dot(vt[h * HEAD_DIM:(h + 1) * HEAD_DIM, :], a.astype(BF16))
            run_ref[h] = run + laters[h][0:1, :] - sps[h][0:1, :]

    def alive():
        return jnp.max(run_ref[...]) > SB_DEAD_LOG

    block(i, True)

    def body(carry):
        t, _ = carry
        block(i - 1 - t, False)
        return t + 1, alive()

    lax.while_loop(lambda carry: (carry[0] < i) & carry[1], body, (jnp.int32(0), alive()))
    o_ref[0] = jnp.concatenate([acc_ref[h].T for h in range(HEADS_SB)], axis=1).astype(o_ref.dtype)


def _sb(sqt, sk, svt):
    b, s, w = sk.shape
    tq = TKC
    return pl.pallas_call(
        _sb_kernel,
        grid=(b, s // tq),
        in_specs=[pl.BlockSpec((1, 1, w, tq), lambda bi, qi: (bi, qi, 0, 0)),
                  pl.BlockSpec((1, s, w), lambda bi, qi: (bi, 0, 0)),
                  pl.BlockSpec((1, s // TKC, w, TKC), lambda bi, qi: (bi, 0, 0, 0))],
        out_specs=pl.BlockSpec((1, tq, w), lambda bi, qi: (bi, qi, 0)),
        out_shape=jax.ShapeDtypeStruct((b, s, w), BF16),
        scratch_shapes=[pltpu.VMEM((HEADS_SB, LANE, tq), BF16),
                        pltpu.VMEM((TKC, 2 * TKC), BF16),
                        pltpu.VMEM((HEADS_SB, 1, tq), F32),
                        pltpu.VMEM((HEADS_SB, HEAD_DIM, tq), F32)],
        compiler_params=pltpu.CompilerParams(
            dimension_semantics=("arbitrary", "arbitrary"), vmem_limit_bytes=VMEM_LIMIT),
        name="sb",
    )(sqt, sk, svt)


NEG_KEY = -0x7149F2CB


def _sortable(score):
    bits = lax.bitcast_convert_type(score, jnp.int32)
    return jnp.where(bits < 0, bits ^ jnp.int32(0x7FFFFFFF), bits)


def _sum_sublane_groups(x):
    parts = [x[r * SUBLANE:(r + 1) * SUBLANE] for r in range(x.shape[0] // SUBLANE)]
    while len(parts) > 1:
        parts = [parts[k] + parts[k + 1] for k in range(0, len(parts) - 1, 2)] + (
            [parts[-1]] if len(parts) % 2 else [])
    return parts[0]


def _tree_sum(parts):
    parts = list(parts)
    while len(parts) > 1:
        parts = [parts[k] + parts[k + 1] for k in range(0, len(parts) - 1, 2)] + (
            [parts[-1]] if len(parts) % 2 else [])
    return parts[0]


def _count_ge16(ref, nkb, cand):
    tk, tq = ref.shape[1], ref.shape[2]
    cand16 = jnp.broadcast_to(cand, (PACKED_ROWS, tq)).astype(jnp.int16)
    one, zero = jnp.ones((PACKED_ROWS, tq), jnp.int16), jnp.zeros((PACKED_ROWS, tq), jnp.int16)
    parts = []
    for j in range(nkb):
        blk = ref[j]
        parts += [jnp.where(blk[r * PACKED_ROWS:(r + 1) * PACKED_ROWS] >= cand16, one, zero)
                  for r in range(tk // PACKED_ROWS)]
    return jnp.sum(_tree_sum(parts).astype(F32), axis=0, keepdims=True)


def _search16(ref, nkb, n_sel):
    tq = ref.shape[2]

    def bit_step(t, thr_u):
        bit = lax.shift_left(jnp.int32(1), jnp.int32(15) - t)
        cand_u = thr_u | bit
        cnt = _count_ge16(ref, nkb, cand_u - 32768)
        return jnp.where(cnt >= n_sel, cand_u, thr_u)

    return lax.fori_loop(0, 16, bit_step, jnp.zeros((1, tq), jnp.int32)) - 32768


def _dsa_kernel(dqt_ref, iqt_ref, iwt_ref, dk2_ref, ik2_ref, dvt_ref, o_ref,
                key_ref, hi_ref, lo_ref, bias_ref, iq8_ref, q5_ref, jmax_ref, *, n_sel, idx_bits):
    tq, tk = TQ_DSA, TK_DSA
    i = pl.program_id(1)
    kloc = lax.broadcasted_iota(jnp.int32, (tk, tq), 0)
    qpos = i * tq + lax.broadcasted_iota(jnp.int32, (tk, tq), 1)
    klimit = (qpos // CHUNK + 1) * CHUNK
    for h in range(IDX_HEADS):
        iq8_ref[:, h * tq:(h + 1) * tq] = _pair_rows(iqt_ref[0, 0, h * HEAD_DIM:(h + 1) * HEAD_DIM, :], h)
    for h in range(HEADS_DSA):
        q5_ref[:, h * tq:(h + 1) * tq] = _pair_rows(dqt_ref[0, 0, h * HEAD_DIM:(h + 1) * HEAD_DIM, :], h)
    iwt = iwt_ref[0, 0]

    def select(nkb):
        for j in range(nkb):
            a = _dot(ik2_ref[0, j * tk:(j + 1) * tk, :], iq8_ref[...])
            sc = jnp.zeros((tk, tq), F32)
            for h in range(IDX_HEADS):
                sc = sc + iwt[h:h + 1, :] * jnp.maximum(a[:, h * tq:(h + 1) * tq], 0.0)
            sc = sc + 0.0
            sc = jnp.where(kloc + j * tk < klimit, sc, NEG)
            key = _sortable(sc)
            key_ref[j] = key
            hi_ref[j] = lax.shift_right_arithmetic(key, jnp.int32(16)).astype(jnp.int16)
            lo_ref[j] = ((key & jnp.int32(0xFFFF)) - 32768).astype(jnp.int16)

        thr_hi = _search16(hi_ref, nkb, n_sel)
        thr_hi16 = jnp.broadcast_to(thr_hi, (PACKED_ROWS, tq)).astype(jnp.int16)
        for j in range(nkb):
            for r in range(tk // PACKED_ROWS):
                rows = slice(r * PACKED_ROWS, (r + 1) * PACKED_ROWS)
                hi = hi_ref[j, rows, :]
                pinned = jnp.where(hi > thr_hi16, jnp.int16(32767), jnp.int16(-32768))
                lo_ref[j, rows, :] = jnp.where(hi == thr_hi16, lo_ref[j, rows, :], pinned)
        thr = thr_hi * 65536 + (_search16(lo_ref, nkb, n_sel) + 32768)

        def count(pred):
            parts = [_sum_sublane_groups(jnp.where(pred(key_ref[j]), 1.0, 0.0)) for j in range(nkb)]
            return jnp.sum(_tree_sum(parts), axis=0, keepdims=True)

        cnt_ge = count(lambda k: k >= thr)
        jmax_ref[...] = jnp.full((1, tq), 2 ** idx_bits - 1, jnp.int32)
        tied = (cnt_ge > n_sel) & (thr > NEG_KEY)

        @pl.when(jnp.max(jnp.where(tied, 1.0, 0.0)) > 0.0)
        def _():
            need = n_sel - count(lambda k: k > thr)
            for j in range(nkb):
                lo_ref[j] = jnp.where(key_ref[j] == thr, kloc + j * tk, 32767).astype(jnp.int16)
            stored = float(nkb * tk)

            def idx_bit(t, jmax):
                bit = lax.shift_left(jnp.int32(1), jnp.int32(idx_bits - 1) - t)
                cand = jmax | bit
                cnt = stored - _count_ge16(lo_ref, nkb, cand)
                return jnp.where(cnt < need, cand, jmax)

            jmax_ref[...] = lax.fori_loop(0, idx_bits, idx_bit, jnp.zeros((1, tq), jnp.int32))

        jmax = jmax_ref[...]
        for j in range(nkb):
            k = key_ref[j]
            pos = kloc + j * tk
            sel = ((k > thr) | ((k == thr) & (pos <= jmax))) & (pos < klimit)
            bias_ref[j] = jnp.where(sel, 0.0, NEG)

    def attend(nkb):
        def qk(j):
            return _dot(dk2_ref[0, j * tk:(j + 1) * tk, :], q5_ref[...])

        m = jnp.full((1, HEADS_DSA * tq), NEG, F32)
        l = jnp.zeros((1, HEADS_DSA * tq), F32)
        acc = jnp.zeros((HEAD_DIM, HEADS_DSA * tq), F32)
        ahead = qk(0)
        for j in range(nkb):
            s = ahead + jnp.concatenate([bias_ref[j]] * HEADS_DSA, axis=1)
            if j + 1 < nkb:
                ahead = qk(j + 1)
            m_new = jnp.maximum(m, jnp.max(s, axis=0, keepdims=True))
            alpha = jnp.exp(m - m_new)
            p = jnp.exp(s - m_new)
            m = m_new
            l = alpha * l + jnp.sum(p, axis=0, keepdims=True)
            acc = alpha * acc + _dot(dvt_ref[0, j], p.astype(BF16))
        out = acc / l
        o_ref[0] = jnp.concatenate([out[:, h * tq:(h + 1) * tq].T for h in range(HEADS_DSA)],
                                   axis=1).astype(o_ref.dtype)

    nkb = (i * tq + tq - 1) // tk + 1
    nkb_even = lax.shift_left(lax.shift_right_logical(nkb + 1, 1), 1)
    for nkb_static in range(2, dk2_ref.shape[1] // tk + 1, 2):
        @pl.when(nkb_even == nkb_static)
        def _(nkb_static=nkb_static):
            select(nkb_static)
            attend(nkb_static)


def _dsa(dqt, iqt, iwt, dk2, ik2, dvt):
    b, s, _ = dk2.shape
    tq, tk = TQ_DSA, TK_DSA
    n_sel = min(DSA_TOPK_MAX, s // 4)
    assert n_sel <= tk and s % (2 * tk) == 0 and tk % tq == 0 and s < 2 ** 15
    idx_bits = max(1, (s - 1).bit_length())
    return pl.pallas_call(
        functools.partial(_dsa_kernel, n_sel=n_sel, idx_bits=idx_bits),
        grid=(b, s // tq),
        in_specs=[pl.BlockSpec((1, 1, W_DSA, tq), lambda bi, qi: (bi, qi, 0, 0)),
                  pl.BlockSpec((1, 1, W_IDX, tq), lambda bi, qi: (bi, qi, 0, 0)),
                  pl.BlockSpec((1, 1, IDX_HEADS, tq), lambda bi, qi: (bi, qi, 0, 0)),
                  pl.BlockSpec((1, s, LANE), lambda bi, qi: (bi, 0, 0)),
                  pl.BlockSpec((1, s, LANE), lambda bi, qi: (bi, 0, 0)),
                  pl.BlockSpec((1, s // tk, HEAD_DIM, tk), lambda bi, qi: (bi, 0, 0, 0))],
        out_specs=pl.BlockSpec((1, tq, W_DSA), lambda bi, qi: (bi, qi, 0)),
        out_shape=jax.ShapeDtypeStruct((b, s, W_DSA), BF16),
        scratch_shapes=[pltpu.VMEM((s // tk, tk, tq), jnp.int32),
                        pltpu.VMEM((s // tk, tk, tq), jnp.int16),
                        pltpu.VMEM((s // tk, tk, tq), jnp.int16),
                        pltpu.VMEM((s // tk, tk, tq), F32),
                        pltpu.VMEM((LANE, IDX_HEADS * tq), BF16),
                        pltpu.VMEM((LANE, HEADS_DSA * tq), BF16),
                        pltpu.VMEM((1, tq), jnp.int32)],
        compiler_params=pltpu.CompilerParams(
            dimension_semantics=("arbitrary", "arbitrary"), vmem_limit_bytes=VMEM_LIMIT),
        name="dsa",
    )(dqt, iqt, iwt, dk2, ik2, dvt)


def _layer_norm(y, g, b):
    mu = jnp.mean(y, axis=1, keepdims=True)
    d = y - mu
    var = jnp.mean(d * d, axis=1, keepdims=True)
    return d * lax.rsqrt(var + LN_EPS) * g + b


def _merge_kernel(x_ref, of_ref, os_ref, od_ref, wg_ref, wuf_ref, wus_ref, wud_ref, wo_ref,
                  g_ref, b_ref, o_ref):
    half = x_ref.shape[0] // MLP_SPLIT
    for r in range(MLP_SPLIT):
        rows = slice(r * half, (r + 1) * half)
        x = x_ref[rows, :]
        xb = x.astype(BF16)
        merged = None
        for n, (br_ref, wu_ref) in enumerate(((of_ref, wuf_ref), (os_ref, wus_ref), (od_ref, wud_ref))):
            gate = jax.nn.sigmoid(_dot(xb, wg_ref[:, n * D_MODEL:(n + 1) * D_MODEL]))
            term = gate * _dot(br_ref[rows, :], wu_ref[...])
            merged = term if merged is None else merged + term
        y = DEEPNORM_ALPHA * x + _dot(merged.astype(BF16), wo_ref[...])
        o_ref[rows, :] = _layer_norm(y, g_ref[...], b_ref[...])


def _const_spec(shape):
    return pl.BlockSpec(shape, lambda i: (0,) * len(shape), pipeline_mode=pl.Buffered(1))


def _merge(x2, o_fox, o_sb, o_dsa, wg, wuf, wus, wud, wo, g, bvec):
    t = x2.shape[0]
    tm = min(TM_MLP, t)

    def rows(width):
        return pl.BlockSpec((tm, width), lambda i: (i, 0))

    return pl.pallas_call(
        _merge_kernel,
        grid=(t // tm,),
        in_specs=[rows(D_MODEL), rows(W_FOX), rows(W_SB), rows(W_DSA),
                  _const_spec(wg.shape), _const_spec(wuf.shape), _const_spec(wus.shape),
                  _const_spec(wud.shape), _const_spec(wo.shape),
                  _const_spec(g.shape), _const_spec(bvec.shape)],
        out_specs=rows(D_MODEL),
        out_shape=jax.ShapeDtypeStruct((t, D_MODEL), F32),
        compiler_params=pltpu.CompilerParams(
            dimension_semantics=("arbitrary",), vmem_limit_bytes=VMEM_LIMIT),
        name="merge",
    )(x2, o_fox, o_sb, o_dsa, wg, wuf, wus, wud, wo, g, bvec)


FF_CHUNK = 1024


def _ffn_kernel(x_ref, p_ref, w1_ref, w2_ref, wpg_ref, wp_ref, g_ref, b_ref, o_ref):
    half = x_ref.shape[0] // MLP_SPLIT
    for r in range(MLP_SPLIT):
        rows = slice(r * half, (r + 1) * half)
        x = x_ref[rows, :]
        xb = x.astype(BF16)
        acc = None
        for c in range(D_FF // FF_CHUNK):
            h = jnp.maximum(_dot(xb, w1_ref[:, c * FF_CHUNK:(c + 1) * FF_CHUNK]), 0.0)
            term = _dot((h * h).astype(BF16), w2_ref[c * FF_CHUNK:(c + 1) * FF_CHUNK, :])
            acc = term if acc is None else acc + term
        ple = jax.nn.sigmoid(_dot(xb, wpg_ref[...])) * _dot(p_ref[rows, :].astype(BF16), wp_ref[...])
        y = DEEPNORM_ALPHA * x + acc + ple
        o_ref[rows, :] = _layer_norm(y, g_ref[...], b_ref[...])


def _ffn(x2, p2, w1, w2, wpg, wp, g, bvec):
    t = x2.shape[0]
    tm = min(TM_MLP, t)

    def rows(width):
        return pl.BlockSpec((tm, width), lambda i: (i, 0))

    return pl.pallas_call(
        _ffn_kernel,
        grid=(t // tm,),
        in_specs=[rows(D_MODEL), rows(PLE_DIM),
                  _const_spec(w1.shape), _const_spec(w2.shape), _const_spec(wpg.shape),
                  _const_spec(wp.shape), _const_spec(g.shape), _const_spec(bvec.shape)],
        out_specs=rows(D_MODEL),
        out_shape=jax.ShapeDtypeStruct((t, D_MODEL), F32),
        compiler_params=pltpu.CompilerParams(
            dimension_semantics=("arbitrary",), vmem_limit_bytes=VMEM_LIMIT),
        name="ffn",
    )(x2, p2, w1, w2, wpg, wp, g, bvec)


def _pack_w_in(w):
    d = w.shape[0]
    offs = {}
    o = 0
    for name, width in (("fq", W_FOX), ("fk", W_FOX), ("fv", W_FOX), ("ff", HEADS_FOX),
                        ("sq", W_SB), ("sk", W_SB), ("sv", W_SB),
                        ("dq", W_DSA), ("dk", HEAD_DIM), ("dv", HEAD_DIM),
                        ("iq", W_IDX), ("ik", HEAD_DIM), ("iw", IDX_HEADS),
                        ("g", N_BRANCH * D_MODEL)):
        offs[name] = w[:, o:o + width]
        o += width

    def z(n):
        return jnp.zeros((d, n), w.dtype)

    packed = jnp.concatenate(
        [offs["fk"], offs["sk"], z(W_SB_PAD - W_SB), offs["dk"], offs["dk"], offs["ik"], offs["ik"],
         offs["ff"], z(IW_LANE - HEADS_FOX), offs["iw"], z(LANE - IW_LANE - IDX_HEADS)], axis=1)
    assert packed.shape[1] == C_PACK
    packed_t = jnp.concatenate([offs["fq"], offs["sq"], offs["dq"], offs["iq"],
                                offs["fv"], offs["sv"], offs["dv"]], axis=1).T
    assert packed_t.shape[0] == R_PACK
    return packed.astype(BF16), packed_t.astype(BF16), offs["g"].astype(BF16)


def _rope_tables(s):
    inv = ROPE_THETA ** (-jnp.arange(HALF, dtype=F32) / HALF)
    ang = jnp.arange(s, dtype=F32)[:, None] * inv[None, :]
    cos, sin = jnp.cos(ang), jnp.sin(ang)
    row_major = jnp.concatenate([cos, cos, cos, cos, -sin, sin, -sin, sin], axis=1)
    transposed = jnp.concatenate([cos, cos, -sin, sin], axis=1).T
    return row_major, transposed


def _placement():
    rows_q, cols_q, rows_k, cols_k = [], [], [], []
    for h in range(HEADS_FOX):
        for t in range(N_CTERM):
            rows_q.append(t * LANE + h), cols_q.append(CSTRIDE * h + t)
            rows_k.append(t * LANE + h), cols_k.append(LANE + CSTRIDE * h + N_CTERM + t)
    place = jnp.zeros((N_CTERM * LANE, 2 * LANE), F32)
    place = place.at[jnp.array(rows_q), jnp.array(cols_q)].set(1.0)
    place = place.at[jnp.array(rows_k), jnp.array(cols_k)].set(-1.0)
    lane = jnp.arange(2 * LANE)
    head_lane = (lane % LANE) % CSTRIDE
    in_heads = (lane % LANE) < CSTRIDE * HEADS_FOX
    ones_q = (lane < LANE) & in_heads & (head_lane >= N_CTERM) & (head_lane < 2 * N_CTERM)
    ones_k = (lane >= LANE) & in_heads & (head_lane < N_CTERM)
    ones_row = jnp.where(ones_q | ones_k, 1.0, 0.0).astype(F32)[None, :]
    return place.astype(BF16), ones_row


def kernel(x, p, w_in, b_forget, w_up_fox, w_up_sb, w_up_dsa, w_out, ln1_g, ln1_b,
           w_ff_in, w_ff_out, w_ple, w_ple_gate, ln2_g, ln2_b):
    b, s, d = x.shape
    depth = w_in.shape[0]
    rope_tab, rope_tab_t = _rope_tables(s)
    place, ones_row = _placement()
    w_in_rows = w_in.reshape(depth * d, w_in.shape[2])
    for i in range(depth):
        w_pack, wt_pack, w_gate = _pack_w_in(w_in_rows[i * d:(i + 1) * d])
        bias_row = jnp.zeros((1, LANE), F32).at[0, :HEADS_FOX].set(b_forget[i].astype(F32))
        (fqt, fqct, fk, fkc, fvt, sqt, sk, svt, dqt, iqt, dk2, ik2, dvt, iwt) = _proj(
            x, w_pack, wt_pack, bias_row, rope_tab, rope_tab_t, place, ones_row)
        o_fox = _fox(fqt, fqct, fk, fkc, fvt)
        o_sb = _sb(sqt, sk, svt)
        o_dsa = _dsa(dqt, iqt, iwt, dk2, ik2, dvt)
        x1 = _merge(x.reshape(b * s, d), o_fox.reshape(b * s, W_FOX), o_sb.reshape(b * s, W_SB),
                    o_dsa.reshape(b * s, W_DSA), w_gate,
                    w_up_fox[i].astype(BF16), w_up_sb[i].astype(BF16), w_up_dsa[i].astype(BF16),
                    w_out[i].astype(BF16), ln1_g[i][None, :], ln1_b[i][None, :])
        x2 = _ffn(x1, p[i].reshape(b * s, PLE_DIM), w_ff_in[i].astype(BF16), w_ff_out[i].astype(BF16),
                  w_ple_gate[i].astype(BF16), w_ple[i].astype(BF16), ln2_g[i][None, :], ln2_b[i][None, :])
        x = x2.reshape(b, s, d)
    return x
```

```python
import functools

import jax
import jax.numpy as jnp
from jax import lax
from jax.experimental import pallas as pl
from jax.experimental.pallas import tpu as pltpu

D_MODEL = 1024
HEAD_DIM = 64
HALF = HEAD_DIM // 2
HEADS_FOX = 6
HEADS_SB = 5
HEADS_DSA = 5
IDX_HEADS = 8
CHUNK = 64
DSA_TOPK_MAX = 256
D_FF = 4 * D_MODEL
PLE_DIM = 256
ROPE_THETA = 10000.0
LN_EPS = 1e-5
N_BRANCH = 3
NEG = -1e30
DEPTH = 2
DEEPNORM_ALPHA = (2 * DEPTH) ** 0.25
W_FOX = HEADS_FOX * HEAD_DIM
W_SB = HEADS_SB * HEAD_DIM
W_DSA = HEADS_DSA * HEAD_DIM
W_IDX = IDX_HEADS * HEAD_DIM
SCALE = HEAD_DIM ** -0.5

LANE = 128
SUBLANE = 8
PACKED_ROWS = 16
W_SB_PAD = 384
IW_LANE = 8
N_CTERM = 3
VMEM_LIMIT = 56 * 1024 * 1024

TM_PROJ = 512
TKC = 256
TQ_DSA = 256
TK_DSA = TKC
TM_MLP = 1024
MLP_SPLIT = 2

BF16 = jnp.bfloat16
F32 = jnp.float32


def _log_sigmoid(x):
    return jnp.minimum(x, 0.0) - jnp.log(1.0 + jnp.exp(-jnp.abs(x)))


def _dot(a, b):
    return jnp.dot(a, b, preferred_element_type=F32)


def _dot_nt(a, b):
    return lax.dot_general(a, b, (((1,), (1,)), ((), ())), preferred_element_type=F32)


def _split3(v):
    hi = v.astype(BF16)
    r = v - hi.astype(F32)
    mid = r.astype(BF16)
    lo = (r - mid.astype(F32)).astype(BF16)
    return hi, mid, lo


def _pair_rows(qt_h, h):
    z = jnp.zeros_like(qt_h)
    return jnp.concatenate([z, qt_h] if h % 2 else [qt_h, z], axis=0)


OFF_FK = 0
OFF_SK = 384
OFF_DK2 = 768
OFF_IK2 = 896
OFF_SM = 1024
C_PACK = 1152
OFF_TFQ, OFF_TSQ, OFF_TDQ, OFF_TIQ = 0, 384, 704, 1024
OFF_TFV, OFF_TSV, OFF_TDV = 1536, 1920, 2240
R_PACK = 2304
CSTRIDE = PACKED_ROWS


def _swap_halves(h):
    lane = lax.broadcasted_iota(jnp.int32, h.shape, 1)
    first = (lane % HEAD_DIM) < HALF
    return jnp.where(first, pltpu.roll(h, LANE - HALF, 1), pltpu.roll(h, HALF, 1))


def _rope_rows(h, cos, sin):
    return h * cos + _swap_halves(h) * sin


def _rope_t(ht, cos_t, sin_t):
    parts = []
    for h in range(ht.shape[0] // HEAD_DIM):
        x = ht[h * HEAD_DIM:(h + 1) * HEAD_DIM]
        swapped = jnp.concatenate([x[HALF:], x[:HALF]], axis=0)
        parts.append(x * cos_t + swapped * sin_t)
    return jnp.concatenate(parts, axis=0)


def _proj_kernel(x_ref, w_ref, wt_ref, bias_ref, tab_ref, tabt_ref, place_ref, ones_ref,
                 fqt_ref, fqct_ref, fk_ref, fkc_ref, fvt_ref, sqt_ref, sk_ref, svt_ref,
                 dqt_ref, iqt_ref, dk2_ref, ik2_ref, dvt_ref, iwt_ref, carry_ref):
    tm = x_ref.shape[1]

    @pl.when(pl.program_id(1) == 0)
    def _():
        carry_ref[...] = jnp.zeros_like(carry_ref)

    xb = x_ref[0].astype(BF16)

    def seg(off, width):
        return _dot(xb, w_ref[:, off:off + width])

    def seg_t(off, nrows):
        return _dot_nt(wt_ref[off:off + nrows, :], xb)

    def put_chunks(ref, val, chunk):
        for r in range(tm // chunk):
            ref[0, r] = val[:, r * chunk:(r + 1) * chunk].astype(ref.dtype)

    small = seg(OFF_SM, LANE)
    logf = _log_sigmoid(small + bias_ref[...])

    row = lax.broadcasted_iota(jnp.int32, (tm, tm), 0)
    col = lax.broadcasted_iota(jnp.int32, (tm, tm), 1)
    tri = jnp.where(col <= row, 1.0, 0.0).astype(BF16)
    c = _dot(tri, jnp.concatenate(_split3(logf), axis=1))
    c = c[:, 0:LANE] + c[:, LANE:2 * LANE] + c[:, 2 * LANE:3 * LANE] + carry_ref[...]
    carry_ref[...] = c[tm - 1:tm, :]
    placed = _dot(jnp.concatenate(_split3(c), axis=1), place_ref[...]) + ones_ref[...]
    put_chunks(fqct_ref, placed[:, 0:LANE].T, TKC)
    fkc_ref[0] = placed[:, LANE:2 * LANE].astype(BF16)
    put_chunks(iwt_ref, small.T[IW_LANE:IW_LANE + IDX_HEADS], TQ_DSA)

    fk_ref[0] = seg(OFF_FK, W_FOX).astype(BF16)
    sk_ref[0] = seg(OFF_SK, W_SB_PAD)[:, :W_SB].astype(BF16)
    cos, sin = tab_ref[:, 0:LANE], tab_ref[:, LANE:2 * LANE]
    dk2_ref[0] = _rope_rows(seg(OFF_DK2, LANE), cos, sin).astype(BF16)
    ik2_ref[0] = _rope_rows(seg(OFF_IK2, LANE), cos, sin).astype(BF16)

    cos_t, sin_t = tabt_ref[0:HEAD_DIM, :], tabt_ref[HEAD_DIM:2 * HEAD_DIM, :]
    put_chunks(fqt_ref, seg_t(OFF_TFQ, W_FOX) * SCALE, TKC)
    put_chunks(sqt_ref, seg_t(OFF_TSQ, W_SB) * SCALE, TKC)
    put_chunks(dqt_ref, _rope_t(seg_t(OFF_TDQ, W_DSA), cos_t, sin_t) * SCALE, TQ_DSA)
    put_chunks(iqt_ref, _rope_t(seg_t(OFF_TIQ, W_IDX), cos_t, sin_t), TQ_DSA)
    put_chunks(fvt_ref, seg_t(OFF_TFV, W_FOX), TKC)
    put_chunks(svt_ref, seg_t(OFF_TSV, W_SB), TKC)
    put_chunks(dvt_ref, seg_t(OFF_TDV, HEAD_DIM), TKC)


def _proj(x, w_pack, wt_pack, bias_row, rope_tab, rope_tab_t, place, ones_row):
    b, s, d = x.shape
    tm = min(TM_PROJ, s)
    grid = (b, s // tm)

    def rows(width, dtype=BF16):
        return (jax.ShapeDtypeStruct((b, s, width), dtype),
                pl.BlockSpec((1, tm, width), lambda bi, si: (bi, si, 0)))

    def chunks(nrows, chunk, dtype=BF16):
        return (jax.ShapeDtypeStruct((b, s // chunk, nrows, chunk), dtype),
                pl.BlockSpec((1, tm // chunk, nrows, chunk), lambda bi, si: (bi, si, 0, 0)))

    def const(shape):
        return pl.BlockSpec(shape, lambda bi, si: (0,) * len(shape))

    outs = [chunks(W_FOX, TKC), chunks(LANE, TKC), rows(W_FOX), rows(LANE), chunks(W_FOX, TKC),
            chunks(W_SB, TKC), rows(W_SB), chunks(W_SB, TKC),
            chunks(W_DSA, TQ_DSA), chunks(W_IDX, TQ_DSA), rows(LANE), rows(LANE), chunks(HEAD_DIM, TKC),
            chunks(IDX_HEADS, TQ_DSA, F32)]
    return pl.pallas_call(
        _proj_kernel,
        grid=grid,
        in_specs=[pl.BlockSpec((1, tm, d), lambda bi, si: (bi, si, 0)),
                  const((d, C_PACK)), const((R_PACK, d)), const((1, LANE)),
                  pl.BlockSpec((tm, 2 * LANE), lambda bi, si: (si, 0)),
                  pl.BlockSpec((2 * HEAD_DIM, tm), lambda bi, si: (0, si)),
                  const(place.shape), const(ones_row.shape)],
        out_specs=[o[1] for o in outs],
        out_shape=[o[0] for o in outs],
        scratch_shapes=[pltpu.VMEM((1, LANE), F32)],
        compiler_params=pltpu.CompilerParams(
            dimension_semantics=("arbitrary", "arbitrary"), vmem_limit_bytes=VMEM_LIMIT),
        name="proj",
    )(x, w_pack, wt_pack, bias_row, rope_tab, rope_tab_t, place, ones_row)


def _fox_kernel(qt_ref, qct_ref, k_ref, kc_ref, vt_ref, o_ref, qcat_ref, m_ref, l_ref, acc_ref):
    tq = tk = TKC
    i = pl.program_id(1)
    krow = lax.broadcasted_iota(jnp.int32, (tk, tq), 0)
    qcol = lax.broadcasted_iota(jnp.int32, (tk, tq), 1)
    causal = krow <= qcol

    for h in range(HEADS_FOX):
        gate_rows = jnp.concatenate(
            ([jnp.zeros((CSTRIDE * h, tq), BF16)] if h else [])
            + [qct_ref[0, 0, CSTRIDE * h:CSTRIDE * (h + 1), :],
               jnp.zeros((LANE - CSTRIDE * (h + 1), tq), BF16)], axis=0)
        qcat_ref[h] = jnp.concatenate(
            [_pair_rows(qt_ref[0, 0, h * HEAD_DIM:(h + 1) * HEAD_DIM, :], h), gate_rows], axis=0)
    m_ref[...] = jnp.full(m_ref.shape, NEG, F32)
    l_ref[...] = jnp.zeros(l_ref.shape, F32)
    acc_ref[...] = jnp.zeros(acc_ref.shape, F32)

    def qk(js):
        kcs = [kc_ref[0, j * tk:(j + 1) * tk, :] for j in js]
        scores = []
        for g in range(HEADS_FOX // 2):
            kcats = [jnp.concatenate([k_ref[0, j * tk:(j + 1) * tk, g * LANE:(g + 1) * LANE], kc], axis=1)
                     for j, kc in zip(js, kcs)]
            for h in (2 * g, 2 * g + 1):
                scores.append([_dot(kcat, qcat_ref[h]) for kcat in kcats])
        return scores

    def softmax_pv(js, scores, masked):
        vts = [vt_ref[0, j] for j in js]
        for h in range(HEADS_FOX):
            ss = [jnp.where(causal, s, NEG) if masked else s for s in scores[h]]
            m_old = m_ref[h]
            m_new = m_old
            for s in ss:
                m_new = jnp.maximum(m_new, jnp.max(s, axis=0, keepdims=True))
            alpha = jnp.exp(m_old - m_new)
            ps = [jnp.exp(s - m_new) for s in ss]
            m_ref[h] = m_new
            l_ref[h] = alpha * l_ref[h] + sum(jnp.sum(p, axis=0, keepdims=True) for p in ps)
            acc_ref[h] = alpha * acc_ref[h] + sum(
                _dot(vt[h * HEAD_DIM:(h + 1) * HEAD_DIM, :], p.astype(BF16)) for vt, p in zip(vts, ps))

    for i_static in range(k_ref.shape[1] // tk):
        @pl.when(i == i_static)
        def _(i_static=i_static):
            groups = [((i_static,), True)] + [((j, j + 1), False) for j in range(0, i_static - 1, 2)]
            if i_static % 2:
                groups.append(((i_static - 1,), False))
            ahead = qk(groups[0][0])
            for n, (js, masked) in enumerate(groups):
                scores = ahead
                if n + 1 < len(groups):
                    ahead = qk(groups[n + 1][0])
                softmax_pv(js, scores, masked)

    o_ref[0] = jnp.concatenate([(acc_ref[h] / l_ref[h]).T for h in range(HEADS_FOX)],
                               axis=1).astype(o_ref.dtype)


def _fox(fqt, fqct, fk, fkc, fvt):
    b, s, w = fk.shape
    tq = TKC
    return pl.pallas_call(
        _fox_kernel,
        grid=(b, s // tq),
        in_specs=[pl.BlockSpec((1, 1, w, tq), lambda bi, qi: (bi, qi, 0, 0)),
                  pl.BlockSpec((1, 1, LANE, tq), lambda bi, qi: (bi, qi, 0, 0)),
                  pl.BlockSpec((1, s, w), lambda bi, qi: (bi, 0, 0)),
                  pl.BlockSpec((1, s, LANE), lambda bi, qi: (bi, 0, 0)),
                  pl.BlockSpec((1, s // TKC, w, TKC), lambda bi, qi: (bi, 0, 0, 0))],
        out_specs=pl.BlockSpec((1, tq, w), lambda bi, qi: (bi, qi, 0)),
        out_shape=jax.ShapeDtypeStruct((b, s, w), BF16),
        scratch_shapes=[pltpu.VMEM((HEADS_FOX, 2 * LANE, tq), BF16),
                        pltpu.VMEM((HEADS_FOX, 1, tq), F32),
                        pltpu.VMEM((HEADS_FOX, 1, tq), F32),
                        pltpu.VMEM((HEADS_FOX, HEAD_DIM, tq), F32)],
        compiler_params=pltpu.CompilerParams(
            dimension_semantics=("arbitrary", "arbitrary"), vmem_limit_bytes=VMEM_LIMIT),
        name="fox",
    )(fqt, fqct, fk, fkc, fvt)


SB_DEAD_LOG = -110.0

def _sb_kernel(qt_ref, k_ref, vt_ref, o_ref, qm_ref, later_w_ref, run_ref, acc_ref):
    tq = tk = TKC
    i = pl.program_id(1)
    krow = lax.broadcasted_iota(jnp.int32, (tk, tq), 0)
    qcol = lax.broadcasted_iota(jnp.int32, (tk, tq), 1)
    strict = krow < qcol
    r = lax.broadcasted_iota(jnp.int32, (tk, 2 * tk), 0)
    c = lax.broadcasted_iota(jnp.int32, (tk, 2 * tk), 1) % tk
    later_w_ref[...] = jnp.where(c > r, -1.0, 0.0).astype(BF16)
    for h in range(HEADS_SB):
        qm_ref[h] = _pair_rows(qt_ref[0, 0, h * HEAD_DIM:(h + 1) * HEAD_DIM, :], h)
    run_ref[...] = jnp.zeros(run_ref.shape, F32)
    acc_ref[...] = jnp.zeros(acc_ref.shape, F32)

    def blocks(js, first_is_causal):
        starts = [pl.multiple_of(j * tk, tk) for j in js]
        zs, sps, laters = [], [], []
        for ks in starts:
            per_head = []
            for h in range(HEADS_SB):
                g = h // 2
                width = min(LANE, W_SB - g * LANE)
                kp = k_ref[0, pl.ds(ks, tk), g * LANE:g * LANE + width]
                per_head.append(_dot(kp, qm_ref[h][:width, :]))
            zs.append(per_head)
        for n in range(len(js)):
            per_head = []
            for z in zs[n]:
                sp = jnp.maximum(z, 0.0) + jnp.log(1.0 + jnp.exp(-jnp.abs(z)))
                per_head.append(jnp.where(strict, sp, 0.0) if (first_is_causal and n == 0) else sp)
            sps.append(per_head)
        for n in range(len(js)):
            per_head = []
            for sp in sps[n]:
                sp_hi = sp.astype(BF16)
                sp_lo = (sp - sp_hi.astype(F32)).astype(BF16)
                per_head.append(_dot(later_w_ref[...], jnp.concatenate([sp_hi, sp_lo], axis=0)))
            laters.append(per_head)
        vts = [vt_ref[0, j] for j in js]
        for h in range(HEADS_SB):
            run, acc = run_ref[h], acc_ref[h]
            for n in range(len(js)):
                a = jnp.exp(zs[n][h] - sps[n][h] + laters[n][h] + run)
                if first_is_causal and n == 0:
                    a = jnp.where(strict, a, 0.0)
                acc = acc + _dot(vts[n][h * HEAD_DIM:(h + 1) * HEAD_DIM, :], a.astype(BF16))
                run = run + laters[n][h][0:1, :] - sps[n][h][0:1, :]
            run_ref[h], acc_ref[h] = run, acc

    def alive():
        return jnp.max(run_ref[...]) > SB_DEAD_LOG

    @pl.when(i == 0)
    def _():
        blocks((i,), True)

    @pl.when(i > 0)
    def _():
        blocks((i, i - 1), True)

    def body(carry):
        t, _ = carry
        blocks((i - 1 - t,), False)
        return t + 1, alive()

    lax.while_loop(lambda carry: (carry[0] < i) & carry[1], body, (jnp.int32(1), alive()))
    o_ref[0] = jnp.concatenate([acc_ref[h].T for h in range(HEADS_SB)], axis=1).astype(o_ref.dtype)


def _sb(sqt, sk, svt):
    b, s, w = sk.shape
    tq = TKC
    return pl.pallas_call(
        _sb_kernel,
        grid=(b, s // tq),
        in_specs=[pl.BlockSpec((1, 1, w, tq), lambda bi, qi: (bi, qi, 0, 0)),
                  pl.BlockSpec((1, s, w), lambda bi, qi: (bi, 0, 0)),
                  pl.BlockSpec((1, s // TKC, w, TKC), lambda bi, qi: (bi, 0, 0, 0))],
        out_specs=pl.BlockSpec((1, tq, w), lambda bi, qi: (bi, qi, 0)),
        out_shape=jax.ShapeDtypeStruct((b, s, w), BF16),
        scratch_shapes=[pltpu.VMEM((HEADS_SB, LANE, tq), BF16),
                        pltpu.VMEM((TKC, 2 * TKC), BF16),
                        pltpu.VMEM((HEADS_SB, 1, tq), F32),
                        pltpu.VMEM((HEADS_SB, HEAD_DIM, tq), F32)],
        compiler_params=pltpu.CompilerParams(
            dimension_semantics=("arbitrary", "arbitrary"), vmem_limit_bytes=VMEM_LIMIT),
        name="sb",
    )(sqt, sk, svt)


NEG_KEY = -0x7149F2CB


def _sortable(score):
    bits = lax.bitcast_convert_type(score, jnp.int32)
    return jnp.where(bits < 0, bits ^ jnp.int32(0x7FFFFFFF), bits)


def _sum_sublane_groups(x):
    parts = [x[r * SUBLANE:(r + 1) * SUBLANE] for r in range(x.shape[0] // SUBLANE)]
    while len(parts) > 1:
        parts = [parts[k] + parts[k + 1] for k in range(0, len(parts) - 1, 2)] + (
            [parts[-1]] if len(parts) % 2 else [])
    return parts[0]


def _tree_sum(parts):
    parts = list(parts)
    while len(parts) > 1:
        parts = [parts[k] + parts[k + 1] for k in range(0, len(parts) - 1, 2)] + (
            [parts[-1]] if len(parts) % 2 else [])
    return parts[0]


def _count_ge16(ref, nkb, cand):
    tk, tq = ref.shape[1], ref.shape[2]
    cand16 = jnp.broadcast_to(cand, (PACKED_ROWS, tq)).astype(jnp.int16)
    one, zero = jnp.ones((PACKED_ROWS, tq), jnp.int16), jnp.zeros((PACKED_ROWS, tq), jnp.int16)
    parts = []
    for j in range(nkb):
        blk = ref[j]
        parts += [jnp.where(blk[r * PACKED_ROWS:(r + 1) * PACKED_ROWS] >= cand16, one, zero)
                  for r in range(tk // PACKED_ROWS)]
    return jnp.sum(_tree_sum(parts).astype(F32), axis=0, keepdims=True)


def _search16(ref, nkb, n_sel):
    tq = ref.shape[2]

    def bit_step(t, thr_u):
        bit = lax.shift_left(jnp.int32(1), jnp.int32(15) - t)
        cand_u = thr_u | bit
        cnt = _count_ge16(ref, nkb, cand_u - 32768)
        return jnp.where(cnt >= n_sel, cand_u, thr_u)

    return lax.fori_loop(0, 16, bit_step, jnp.zeros((1, tq), jnp.int32)) - 32768


def _dsa_kernel(dqt_ref, iqt_ref, iwt_ref, dk2_ref, ik2_ref, dvt_ref, o_ref,
                key_ref, hi_ref, lo_ref, bias_ref, iq8_ref, q5_ref, jmax_ref, *, n_sel, idx_bits):
    tq, tk = TQ_DSA, TK_DSA
    i = pl.program_id(1)
    kloc = lax.broadcasted_iota(jnp.int32, (tk, tq), 0)
    qpos = i * tq + lax.broadcasted_iota(jnp.int32, (tk, tq), 1)
    klimit = (qpos // CHUNK + 1) * CHUNK
    for h in range(IDX_HEADS):
        iq8_ref[:, h * tq:(h + 1) * tq] = _pair_rows(iqt_ref[0, 0, h * HEAD_DIM:(h + 1) * HEAD_DIM, :], h)
    for h in range(HEADS_DSA):
        q5_ref[:, h * tq:(h + 1) * tq] = _pair_rows(dqt_ref[0, 0, h * HEAD_DIM:(h + 1) * HEAD_DIM, :], h)
    iwt = iwt_ref[0, 0]

    def select(nkb):
        for j in range(nkb):
            a = _dot(ik2_ref[0, j * tk:(j + 1) * tk, :], iq8_ref[...])
            sc = jnp.zeros((tk, tq), F32)
            for h in range(IDX_HEADS):
                sc = sc + iwt[h:h + 1, :] * jnp.maximum(a[:, h * tq:(h + 1) * tq], 0.0)
            sc = sc + 0.0
            sc = jnp.where(kloc + j * tk < klimit, sc, NEG)
            key = _sortable(sc)
            key_ref[j] = key
            hi_ref[j] = lax.shift_right_arithmetic(key, jnp.int32(16)).astype(jnp.int16)
            lo_ref[j] = ((key & jnp.int32(0xFFFF)) - 32768).astype(jnp.int16)

        thr_hi = _search16(hi_ref, nkb, n_sel)
        thr_hi16 = jnp.broadcast_to(thr_hi, (PACKED_ROWS, tq)).astype(jnp.int16)
        for j in range(nkb):
            for r in range(tk // PACKED_ROWS):
                rows = slice(r * PACKED_ROWS, (r + 1) * PACKED_ROWS)
                hi = hi_ref[j, rows, :]
                pinned = jnp.where(hi > thr_hi16, jnp.int16(32767), jnp.int16(-32768))
                lo_ref[j, rows, :] = jnp.where(hi == thr_hi16, lo_ref[j, rows, :], pinned)
        thr = thr_hi * 65536 + (_search16(lo_ref, nkb, n_sel) + 32768)

        def count(pred):
            parts = [_sum_sublane_groups(jnp.where(pred(key_ref[j]), 1.0, 0.0)) for j in range(nkb)]
            return jnp.sum(_tree_sum(parts), axis=0, keepdims=True)

        cnt_ge = count(lambda k: k >= thr)
        jmax_ref[...] = jnp.full((1, tq), 2 ** idx_bits - 1, jnp.int32)
        tied = (cnt_ge > n_sel) & (thr > NEG_KEY)

        @pl.when(jnp.max(jnp.where(tied, 1.0, 0.0)) > 0.0)
        def _():
            need = n_sel - count(lambda k: k > thr)
            for j in range(nkb):
                lo_ref[j] = jnp.where(key_ref[j] == thr, kloc + j * tk, 32767).astype(jnp.int16)
            stored = float(nkb * tk)

            def idx_bit(t, jmax):
                bit = lax.shift_left(jnp.int32(1), jnp.int32(idx_bits - 1) - t)
                cand = jmax | bit
                cnt = stored - _count_ge16(lo_ref, nkb, cand)
                return jnp.where(cnt < need, cand, jmax)

            jmax_ref[...] = lax.fori_loop(0, idx_bits, idx_bit, jnp.zeros((1, tq), jnp.int32))

        jmax = jmax_ref[...]
        for j in range(nkb):
            k = key_ref[j]
            pos = kloc + j * tk
            sel = ((k > thr) | ((k == thr) & (pos <= jmax))) & (pos < klimit)
            bias_ref[j] = jnp.where(sel, 0.0, NEG)

    def attend(nkb):
        def qk(j):
            return _dot(dk2_ref[0, j * tk:(j + 1) * tk, :], q5_ref[...])

        m = jnp.full((1, HEADS_DSA * tq), NEG, F32)
        l = jnp.zeros((1, HEADS_DSA * tq), F32)
        acc = jnp.zeros((HEAD_DIM, HEADS_DSA * tq), F32)
        ahead = qk(0)
        for j in range(nkb):
            s = ahead + jnp.concatenate([bias_ref[j]] * HEADS_DSA, axis=1)
            if j + 1 < nkb:
                ahead = qk(j + 1)
            m_new = jnp.maximum(m, jnp.max(s, axis=0, keepdims=True))
            alpha = jnp.exp(m - m_new)
            p = jnp.exp(s - m_new)
            m = m_new
            l = alpha * l + jnp.sum(p, axis=0, keepdims=True)
            acc = alpha * acc + _dot(dvt_ref[0, j], p.astype(BF16))
        out = acc / l
        o_ref[0] = jnp.concatenate([out[:, h * tq:(h + 1) * tq].T for h in range(HEADS_DSA)],
                                   axis=1).astype(o_ref.dtype)

    nkb = (i * tq + tq - 1) // tk + 1
    nkb_even = lax.shift_left(lax.shift_right_logical(nkb + 1, 1), 1)
    for nkb_static in range(2, dk2_ref.shape[1] // tk + 1, 2):
        @pl.when(nkb_even == nkb_static)
        def _(nkb_static=nkb_static):
            select(nkb_static)
            attend(nkb_static)


def _dsa(dqt, iqt, iwt, dk2, ik2, dvt):
    b, s, _ = dk2.shape
    tq, tk = TQ_DSA, TK_DSA
    n_sel = min(DSA_TOPK_MAX, s // 4)
    assert n_sel <= tk and s % (2 * tk) == 0 and tk % tq == 0 and s < 2 ** 15
    idx_bits = max(1, (s - 1).bit_length())
    return pl.pallas_call(
        functools.partial(_dsa_kernel, n_sel=n_sel, idx_bits=idx_bits),
        grid=(b, s // tq),
        in_specs=[pl.BlockSpec((1, 1, W_DSA, tq), lambda bi, qi: (bi, qi, 0, 0)),
                  pl.BlockSpec((1, 1, W_IDX, tq), lambda bi, qi: (bi, qi, 0, 0)),
                  pl.BlockSpec((1, 1, IDX_HEADS, tq), lambda bi, qi: (bi, qi, 0, 0)),
                  pl.BlockSpec((1, s, LANE), lambda bi, qi: (bi, 0, 0)),
                  pl.BlockSpec((1, s, LANE), lambda bi, qi: (bi, 0, 0)),
                  pl.BlockSpec((1, s // tk, HEAD_DIM, tk), lambda bi, qi: (bi, 0, 0, 0))],
        out_specs=pl.BlockSpec((1, tq, W_DSA), lambda bi, qi: (bi, qi, 0)),
        out_shape=jax.ShapeDtypeStruct((b, s, W_DSA), BF16),
        scratch_shapes=[pltpu.VMEM((s // tk, tk, tq), jnp.int32),
                        pltpu.VMEM((s // tk, tk, tq), jnp.int16),
                        pltpu.VMEM((s // tk, tk, tq), jnp.int16),
                        pltpu.VMEM((s // tk, tk, tq), F32),
                        pltpu.VMEM((LANE, IDX_HEADS * tq), BF16),
                        pltpu.VMEM((LANE, HEADS_DSA * tq), BF16),
                        pltpu.VMEM((1, tq), jnp.int32)],
        compiler_params=pltpu.CompilerParams(
            dimension_semantics=("arbitrary", "arbitrary"), vmem_limit_bytes=VMEM_LIMIT),
        name="dsa",
    )(dqt, iqt, iwt, dk2, ik2, dvt)


def _layer_norm(y, g, b):
    mu = jnp.mean(y, axis=1, keepdims=True)
    d = y - mu
    var = jnp.mean(d * d, axis=1, keepdims=True)
    return d * lax.rsqrt(var + LN_EPS) * g + b


def _merge_kernel(x_ref, of_ref, os_ref, od_ref, wg_ref, wuf_ref, wus_ref, wud_ref, wo_ref,
                  g_ref, b_ref, o_ref):
    half = x_ref.shape[0] // MLP_SPLIT
    for r in range(MLP_SPLIT):
        rows = slice(r * half, (r + 1) * half)
        x = x_ref[rows, :]
        xb = x.astype(BF16)
        merged = None
        for n, (br_ref, wu_ref) in enumerate(((of_ref, wuf_ref), (os_ref, wus_ref), (od_ref, wud_ref))):
            gate = jax.nn.sigmoid(_dot(xb, wg_ref[:, n * D_MODEL:(n + 1) * D_MODEL]))
            term = gate * _dot(br_ref[rows, :], wu_ref[...])
            merged = term if merged is None else merged + term
        y = DEEPNORM_ALPHA * x + _dot(merged.astype(BF16), wo_ref[...])
        o_ref[rows, :] = _layer_norm(y, g_ref[...], b_ref[...])


def _const_spec(shape):
    return pl.BlockSpec(shape, lambda i: (0,) * len(shape), pipeline_mode=pl.Buffered(1))


def _merge(x2, o_fox, o_sb, o_dsa, wg, wuf, wus, wud, wo, g, bvec):
    t = x2.shape[0]
    tm = min(TM_MLP, t)

    def rows(width):
        return pl.BlockSpec((tm, width), lambda i: (i, 0))

    return pl.pallas_call(
        _merge_kernel,
        grid=(t // tm,),
        in_specs=[rows(D_MODEL), rows(W_FOX), rows(W_SB), rows(W_DSA),
                  _const_spec(wg.shape), _const_spec(wuf.shape), _const_spec(wus.shape),
                  _const_spec(wud.shape), _const_spec(wo.shape),
                  _const_spec(g.shape), _const_spec(bvec.shape)],
        out_specs=rows(D_MODEL),
        out_shape=jax.ShapeDtypeStruct((t, D_MODEL), F32),
        compiler_params=pltpu.CompilerParams(
            dimension_semantics=("arbitrary",), vmem_limit_bytes=VMEM_LIMIT),
        name="merge",
    )(x2, o_fox, o_sb, o_dsa, wg, wuf, wus, wud, wo, g, bvec)


FF_CHUNK = 1024


def _ffn_kernel(x_ref, p_ref, w1_ref, w2_ref, wpg_ref, wp_ref, g_ref, b_ref, o_ref):
    half = x_ref.shape[0] // MLP_SPLIT
    for r in range(MLP_SPLIT):
        rows = slice(r * half, (r + 1) * half)
        x = x_ref[rows, :]
        xb = x.astype(BF16)
        acc = None
        for c in range(D_FF // FF_CHUNK):
            h = jnp.maximum(_dot(xb, w1_ref[:, c * FF_CHUNK:(c + 1) * FF_CHUNK]), 0.0)
            term = _dot((h * h).astype(BF16), w2_ref[c * FF_CHUNK:(c + 1) * FF_CHUNK, :])
            acc = term if acc is None else acc + term
        ple = jax.nn.sigmoid(_dot(xb, wpg_ref[...])) * _dot(p_ref[0, rows, :].astype(BF16), wp_ref[...])
        y = DEEPNORM_ALPHA * x + acc + ple
        o_ref[rows, :] = _layer_norm(y, g_ref[...], b_ref[...])


def _ffn(x2, p3, layer, w1, w2, wpg, wp, g, bvec):
    t = x2.shape[0]
    tm = min(TM_MLP, t)

    def rows(width):
        return pl.BlockSpec((tm, width), lambda i: (i, 0))

    return pl.pallas_call(
        _ffn_kernel,
        grid=(t // tm,),
        in_specs=[rows(D_MODEL), pl.BlockSpec((1, tm, PLE_DIM), lambda i: (layer, i, 0)),
                  _const_spec(w1.shape), _const_spec(w2.shape), _const_spec(wpg.shape),
                  _const_spec(wp.shape), _const_spec(g.shape), _const_spec(bvec.shape)],
        out_specs=rows(D_MODEL),
        out_shape=jax.ShapeDtypeStruct((t, D_MODEL), F32),
        compiler_params=pltpu.CompilerParams(
            dimension_semantics=("arbitrary",), vmem_limit_bytes=VMEM_LIMIT),
        name="ffn",
    )(x2, p3, w1, w2, wpg, wp, g, bvec)


def _pack_w_in(w):
    d = w.shape[0]
    offs = {}
    o = 0
    for name, width in (("fq", W_FOX), ("fk", W_FOX), ("fv", W_FOX), ("ff", HEADS_FOX),
                        ("sq", W_SB), ("sk", W_SB), ("sv", W_SB),
                        ("dq", W_DSA), ("dk", HEAD_DIM), ("dv", HEAD_DIM),
                        ("iq", W_IDX), ("ik", HEAD_DIM), ("iw", IDX_HEADS),
                        ("g", N_BRANCH * D_MODEL)):
        offs[name] = w[:, o:o + width]
        o += width

    def z(n):
        return jnp.zeros((d, n), w.dtype)

    packed = jnp.concatenate(
        [offs["fk"], offs["sk"], z(W_SB_PAD - W_SB), offs["dk"], offs["dk"], offs["ik"], offs["ik"],
         offs["ff"], z(IW_LANE - HEADS_FOX), offs["iw"], z(LANE - IW_LANE - IDX_HEADS)], axis=1)
    assert packed.shape[1] == C_PACK
    packed_t = jnp.concatenate([offs["fq"], offs["sq"], offs["dq"], offs["iq"],
                                offs["fv"], offs["sv"], offs["dv"]], axis=1).T
    assert packed_t.shape[0] == R_PACK
    return packed.astype(BF16), packed_t.astype(BF16), offs["g"].astype(BF16)


def _rope_tables(s):
    inv = ROPE_THETA ** (-jnp.arange(HALF, dtype=F32) / HALF)
    ang = jnp.arange(s, dtype=F32)[:, None] * inv[None, :]
    cos, sin = jnp.cos(ang), jnp.sin(ang)
    row_major = jnp.concatenate([cos, cos, cos, cos, -sin, sin, -sin, sin], axis=1)
    transposed = jnp.concatenate([cos, cos, -sin, sin], axis=1).T
    return row_major, transposed


def _placement():
    rows_q, cols_q, rows_k, cols_k = [], [], [], []
    for h in range(HEADS_FOX):
        for t in range(N_CTERM):
            rows_q.append(t * LANE + h), cols_q.append(CSTRIDE * h + t)
            rows_k.append(t * LANE + h), cols_k.append(LANE + CSTRIDE * h + N_CTERM + t)
    place = jnp.zeros((N_CTERM * LANE, 2 * LANE), F32)
    place = place.at[jnp.array(rows_q), jnp.array(cols_q)].set(1.0)
    place = place.at[jnp.array(rows_k), jnp.array(cols_k)].set(-1.0)
    lane = jnp.arange(2 * LANE)
    head_lane = (lane % LANE) % CSTRIDE
    in_heads = (lane % LANE) < CSTRIDE * HEADS_FOX
    ones_q = (lane < LANE) & in_heads & (head_lane >= N_CTERM) & (head_lane < 2 * N_CTERM)
    ones_k = (lane >= LANE) & in_heads & (head_lane < N_CTERM)
    ones_row = jnp.where(ones_q | ones_k, 1.0, 0.0).astype(F32)[None, :]
    return place.astype(BF16), ones_row


def kernel(x, p, w_in, b_forget, w_up_fox, w_up_sb, w_up_dsa, w_out, ln1_g, ln1_b,
           w_ff_in, w_ff_out, w_ple, w_ple_gate, ln2_g, ln2_b):
    b, s, d = x.shape
    depth = w_in.shape[0]
    rope_tab, rope_tab_t = _rope_tables(s)
    place, ones_row = _placement()
    w_in_rows = w_in.reshape(depth * d, w_in.shape[2])
    for i in range(depth):
        w_pack, wt_pack, w_gate = _pack_w_in(w_in_rows[i * d:(i + 1) * d])
        bias_row = jnp.zeros((1, LANE), F32).at[0, :HEADS_FOX].set(b_forget[i].astype(F32))
        (fqt, fqct, fk, fkc, fvt, sqt, sk, svt, dqt, iqt, dk2, ik2, dvt, iwt) = _proj(
            x, w_pack, wt_pack, bias_row, rope_tab, rope_tab_t, place, ones_row)
        o_fox = _fox(fqt, fqct, fk, fkc, fvt)
        o_sb = _sb(sqt, sk, svt)
        o_dsa = _dsa(dqt, iqt, iwt, dk2, ik2, dvt)
        x1 = _merge(x.reshape(b * s, d), o_fox.reshape(b * s, W_FOX), o_sb.reshape(b * s, W_SB),
                    o_dsa.reshape(b * s, W_DSA), w_gate,
                    w_up_fox[i].astype(BF16), w_up_sb[i].astype(BF16), w_up_dsa[i].astype(BF16),
                    w_out[i].astype(BF16), ln1_g[i][None, :], ln1_b[i][None, :])
        x2 = _ffn(x1, p.reshape(depth, b * s, PLE_DIM), i, w_ff_in[i].astype(BF16), w_ff_out[i].astype(BF16),
                  w_ple_gate[i].astype(BF16), w_ple[i].astype(BF16), ln2_g[i][None, :], ln2_b[i][None, :])
        x = x2.reshape(b, s, d)
    return x
```

```python
import functools

import jax
import jax.numpy as jnp
from jax import lax
from jax.experimental import pallas as pl
from jax.experimental.pallas import tpu as pltpu

D_MODEL = 1024
HEAD_DIM = 64
HALF = HEAD_DIM // 2
HEADS_FOX = 6
HEADS_SB = 5
HEADS_DSA = 5
IDX_HEADS = 8
CHUNK = 64
DSA_TOPK_MAX = 256
D_FF = 4 * D_MODEL
PLE_DIM = 256
ROPE_THETA = 10000.0
LN_EPS = 1e-5
N_BRANCH = 3
NEG = -1e30
DEPTH = 2
DEEPNORM_ALPHA = (2 * DEPTH) ** 0.25
W_FOX = HEADS_FOX * HEAD_DIM
W_SB = HEADS_SB * HEAD_DIM
W_DSA = HEADS_DSA * HEAD_DIM
W_IDX = IDX_HEADS * HEAD_DIM
SCALE = HEAD_DIM ** -0.5

LANE = 128
SUBLANE = 8
PACKED_ROWS = 16
W_SB_PAD = 384
IW_LANE = 8
N_CTERM = 3
VMEM_LIMIT = 56 * 1024 * 1024

TM_PROJ = 512
TKC = 256
TQ_DSA = 256
TK_DSA = TKC
TM_MLP = 1024
MLP_SPLIT = 2

BF16 = jnp.bfloat16
F32 = jnp.float32


def _log_sigmoid(x):
    return jnp.minimum(x, 0.0) - jnp.log(1.0 + jnp.exp(-jnp.abs(x)))


def _dot(a, b):
    return jnp.dot(a, b, preferred_element_type=F32)


def _dot_nt(a, b):
    return lax.dot_general(a, b, (((1,), (1,)), ((), ())), preferred_element_type=F32)


def _split3(v):
    hi = v.astype(BF16)
    r = v - hi.astype(F32)
    mid = r.astype(BF16)
    lo = (r - mid.astype(F32)).astype(BF16)
    return hi, mid, lo


def _pair_rows(qt_h, h):
    z = jnp.zeros_like(qt_h)
    return jnp.concatenate([z, qt_h] if h % 2 else [qt_h, z], axis=0)


OFF_SM = 0
OFF_DK2 = 128
OFF_IK2 = 256
OFF_FK = 384
OFF_SK = 768
C_PACK = 1152
OFF_TFQ, OFF_TSQ, OFF_TDQ, OFF_TIQ = 0, 384, 704, 1024
OFF_TFV, OFF_TSV, OFF_TDV = 1536, 1920, 2240
R_PACK = 2304
CSTRIDE = PACKED_ROWS


def _swap_halves(h):
    lane = lax.broadcasted_iota(jnp.int32, h.shape, 1)
    first = (lane % HEAD_DIM) < HALF
    return jnp.where(first, pltpu.roll(h, LANE - HALF, 1), pltpu.roll(h, HALF, 1))


def _rope_rows(h, cos, sin):
    return h * cos + _swap_halves(h) * sin


def _rope_t(ht, cos_t, sin_t):
    parts = []
    for h in range(ht.shape[0] // HEAD_DIM):
        x = ht[h * HEAD_DIM:(h + 1) * HEAD_DIM]
        swapped = jnp.concatenate([x[HALF:], x[:HALF]], axis=0)
        parts.append(x * cos_t + swapped * sin_t)
    return jnp.concatenate(parts, axis=0)


def _proj_kernel(x_ref, w_ref, wt_ref, bias_ref, tab_ref, tabt_ref, place_ref, ones_ref,
                 fqt_ref, fqct_ref, fk_ref, fkc_ref, fvt_ref, sqt_ref, sk_ref, svt_ref,
                 dqt_ref, iqt_ref, dk2_ref, ik2_ref, dvt_ref, iwt_ref, carry_ref):
    tm = x_ref.shape[1]

    @pl.when(pl.program_id(1) == 0)
    def _():
        carry_ref[...] = jnp.zeros_like(carry_ref)

    xb = x_ref[0].astype(BF16)

    def seg(off, width):
        return _dot(xb, w_ref[:, off:off + width])

    def seg_t(off, nrows):
        return _dot_nt(wt_ref[off:off + nrows, :], xb)

    def put_chunks(ref, val, chunk):
        for r in range(tm // chunk):
            ref[0, r] = val[:, r * chunk:(r + 1) * chunk].astype(ref.dtype)

    narrow = seg(OFF_SM, 3 * LANE)
    small = narrow[:, 0:LANE]
    logf = _log_sigmoid(small + bias_ref[...])

    row = lax.broadcasted_iota(jnp.int32, (tm, tm), 0)
    col = lax.broadcasted_iota(jnp.int32, (tm, tm), 1)
    tri = jnp.where(col <= row, 1.0, 0.0).astype(BF16)
    c = _dot(tri, jnp.concatenate(_split3(logf), axis=1))
    c = c[:, 0:LANE] + c[:, LANE:2 * LANE] + c[:, 2 * LANE:3 * LANE] + carry_ref[...]
    carry_ref[...] = c[tm - 1:tm, :]
    placed = _dot(jnp.concatenate(_split3(c), axis=1), place_ref[...]) + ones_ref[...]
    put_chunks(fqct_ref, placed[:, 0:LANE].T, TKC)
    fkc_ref[0] = placed[:, LANE:2 * LANE].astype(BF16)
    put_chunks(iwt_ref, small.T[IW_LANE:IW_LANE + IDX_HEADS], TQ_DSA)

    keys = seg(OFF_FK, W_FOX + W_SB_PAD)
    fk_ref[0] = keys[:, 0:W_FOX].astype(BF16)
    sk_ref[0] = keys[:, W_FOX:W_FOX + W_SB].astype(BF16)
    cos, sin = tab_ref[:, 0:LANE], tab_ref[:, LANE:2 * LANE]
    dk2_ref[0] = _rope_rows(narrow[:, OFF_DK2:OFF_DK2 + LANE], cos, sin).astype(BF16)
    ik2_ref[0] = _rope_rows(narrow[:, OFF_IK2:OFF_IK2 + LANE], cos, sin).astype(BF16)

    cos_t, sin_t = tabt_ref[0:HEAD_DIM, :], tabt_ref[HEAD_DIM:2 * HEAD_DIM, :]
    put_chunks(fqt_ref, seg_t(OFF_TFQ, W_FOX) * SCALE, TKC)
    put_chunks(sqt_ref, seg_t(OFF_TSQ, W_SB) * SCALE, TKC)
    put_chunks(dqt_ref, _rope_t(seg_t(OFF_TDQ, W_DSA), cos_t, sin_t) * SCALE, TQ_DSA)
    put_chunks(iqt_ref, _rope_t(seg_t(OFF_TIQ, W_IDX), cos_t, sin_t), TQ_DSA)
    put_chunks(fvt_ref, seg_t(OFF_TFV, W_FOX), TKC)
    put_chunks(svt_ref, seg_t(OFF_TSV, W_SB), TKC)
    put_chunks(dvt_ref, seg_t(OFF_TDV, HEAD_DIM), TKC)


def _proj(x, w_pack, wt_pack, bias_row, rope_tab, rope_tab_t, place, ones_row):
    b, s, d = x.shape
    tm = min(TM_PROJ, s)
    grid = (b, s // tm)

    def rows(width, dtype=BF16):
        return (jax.ShapeDtypeStruct((b, s, width), dtype),
                pl.BlockSpec((1, tm, width), lambda bi, si: (bi, si, 0)))

    def chunks(nrows, chunk, dtype=BF16):
        return (jax.ShapeDtypeStruct((b, s // chunk, nrows, chunk), dtype),
                pl.BlockSpec((1, tm // chunk, nrows, chunk), lambda bi, si: (bi, si, 0, 0)))

    def const(shape):
        return pl.BlockSpec(shape, lambda bi, si: (0,) * len(shape))

    outs = [chunks(W_FOX, TKC), chunks(LANE, TKC), rows(W_FOX), rows(LANE), chunks(W_FOX, TKC),
            chunks(W_SB, TKC), rows(W_SB), chunks(W_SB, TKC),
            chunks(W_DSA, TQ_DSA), chunks(W_IDX, TQ_DSA), rows(LANE), rows(LANE), chunks(HEAD_DIM, TKC),
            chunks(IDX_HEADS, TQ_DSA, F32)]
    return pl.pallas_call(
        _proj_kernel,
        grid=grid,
        in_specs=[pl.BlockSpec((1, tm, d), lambda bi, si: (bi, si, 0)),
                  const((d, C_PACK)), const((R_PACK, d)), const((1, LANE)),
                  pl.BlockSpec((tm, 2 * LANE), lambda bi, si: (si, 0)),
                  pl.BlockSpec((2 * HEAD_DIM, tm), lambda bi, si: (0, si)),
                  const(place.shape), const(ones_row.shape)],
        out_specs=[o[1] for o in outs],
        out_shape=[o[0] for o in outs],
        scratch_shapes=[pltpu.VMEM((1, LANE), F32)],
        compiler_params=pltpu.CompilerParams(
            dimension_semantics=("arbitrary", "arbitrary"), vmem_limit_bytes=VMEM_LIMIT),
        name="proj",
    )(x, w_pack, wt_pack, bias_row, rope_tab, rope_tab_t, place, ones_row)


def _fox_kernel(qt_ref, qct_ref, k_ref, kc_ref, vt_ref, o_ref, qcat_ref, m_ref, l_ref, acc_ref):
    tq = tk = TKC
    i = pl.program_id(1)
    krow = lax.broadcasted_iota(jnp.int32, (tk, tq), 0)
    qcol = lax.broadcasted_iota(jnp.int32, (tk, tq), 1)
    causal = krow <= qcol

    for h in range(HEADS_FOX):
        gate_rows = jnp.concatenate(
            ([jnp.zeros((CSTRIDE * h, tq), BF16)] if h else [])
            + [qct_ref[0, 0, CSTRIDE * h:CSTRIDE * (h + 1), :],
               jnp.zeros((LANE - CSTRIDE * (h + 1), tq), BF16)], axis=0)
        qcat_ref[h] = jnp.concatenate(
            [_pair_rows(qt_ref[0, 0, h * HEAD_DIM:(h + 1) * HEAD_DIM, :], h), gate_rows], axis=0)
    m_ref[...] = jnp.full(m_ref.shape, NEG, F32)
    l_ref[...] = jnp.zeros(l_ref.shape, F32)
    acc_ref[...] = jnp.zeros(acc_ref.shape, F32)

    def qk(js):
        kcs = [kc_ref[0, j * tk:(j + 1) * tk, :] for j in js]
        scores = []
        for g in range(HEADS_FOX // 2):
            kcats = [jnp.concatenate([k_ref[0, j * tk:(j + 1) * tk, g * LANE:(g + 1) * LANE], kc], axis=1)
                     for j, kc in zip(js, kcs)]
            for h in (2 * g, 2 * g + 1):
                scores.append([_dot(kcat, qcat_ref[h]) for kcat in kcats])
        return scores

    def softmax_pv(js, scores, masked):
        vts = [vt_ref[0, j] for j in js]
        for h in range(HEADS_FOX):
            ss = [jnp.where(causal, s, NEG) if masked else s for s in scores[h]]
            m_old = m_ref[h]
            m_new = m_old
            for s in ss:
                m_new = jnp.maximum(m_new, jnp.max(s, axis=0, keepdims=True))
            alpha = jnp.exp(m_old - m_new)
            ps = [jnp.exp(s - m_new) for s in ss]
            m_ref[h] = m_new
            l_ref[h] = alpha * l_ref[h] + sum(jnp.sum(p, axis=0, keepdims=True) for p in ps)
            acc_ref[h] = alpha * acc_ref[h] + sum(
                _dot(vt[h * HEAD_DIM:(h + 1) * HEAD_DIM, :], p.astype(BF16)) for vt, p in zip(vts, ps))

    for i_static in range(k_ref.shape[1] // tk):
        @pl.when(i == i_static)
        def _(i_static=i_static):
            groups = [((i_static,), True)] + [((j, j + 1), False) for j in range(0, i_static - 1, 2)]
            if i_static % 2:
                groups.append(((i_static - 1,), False))
            ahead = qk(groups[0][0])
            for n, (js, masked) in enumerate(groups):
                scores = ahead
                if n + 1 < len(groups):
                    ahead = qk(groups[n + 1][0])
                softmax_pv(js, scores, masked)

    o_ref[0] = jnp.concatenate([(acc_ref[h] / l_ref[h]).T for h in range(HEADS_FOX)],
                               axis=1).astype(o_ref.dtype)


def _fox(fqt, fqct, fk, fkc, fvt):
    b, s, w = fk.shape
    tq = TKC
    return pl.pallas_call(
        _fox_kernel,
        grid=(b, s // tq),
        in_specs=[pl.BlockSpec((1, 1, w, tq), lambda bi, qi: (bi, qi, 0, 0)),
                  pl.BlockSpec((1, 1, LANE, tq), lambda bi, qi: (bi, qi, 0, 0)),
                  pl.BlockSpec((1, s, w), lambda bi, qi: (bi, 0, 0)),
                  pl.BlockSpec((1, s, LANE), lambda bi, qi: (bi, 0, 0)),
                  pl.BlockSpec((1, s // TKC, w, TKC), lambda bi, qi: (bi, 0, 0, 0))],
        out_specs=pl.BlockSpec((1, tq, w), lambda bi, qi: (bi, qi, 0)),
        out_shape=jax.ShapeDtypeStruct((b, s, w), BF16),
        scratch_shapes=[pltpu.VMEM((HEADS_FOX, 2 * LANE, tq), BF16),
                        pltpu.VMEM((HEADS_FOX, 1, tq), F32),
                        pltpu.VMEM((HEADS_FOX, 1, tq), F32),
                        pltpu.VMEM((HEADS_FOX, HEAD_DIM, tq), F32)],
        compiler_params=pltpu.CompilerParams(
            dimension_semantics=("arbitrary", "arbitrary"), vmem_limit_bytes=VMEM_LIMIT),
        name="fox",
    )(fqt, fqct, fk, fkc, fvt)


SB_DEAD_LOG = -110.0

def _sb_kernel(qt_ref, k_ref, vt_ref, o_ref, qm_ref, later_w_ref, run_ref, acc_ref):
    tq = tk = TKC
    i = pl.program_id(1)
    krow = lax.broadcasted_iota(jnp.int32, (tk, tq), 0)
    qcol = lax.broadcasted_iota(jnp.int32, (tk, tq), 1)
    strict = krow < qcol
    r = lax.broadcasted_iota(jnp.int32, (tk, 2 * tk), 0)
    c = lax.broadcasted_iota(jnp.int32, (tk, 2 * tk), 1) % tk
    later_w_ref[...] = jnp.where(c > r, -1.0, 0.0).astype(BF16)
    for h in range(HEADS_SB):
        qm_ref[h] = _pair_rows(qt_ref[0, 0, h * HEAD_DIM:(h + 1) * HEAD_DIM, :], h)
    run_ref[...] = jnp.zeros(run_ref.shape, F32)
    acc_ref[...] = jnp.zeros(acc_ref.shape, F32)

    def blocks(js, first_is_causal):
        starts = [pl.multiple_of(j * tk, tk) for j in js]
        zs, sps, laters = [], [], []
        for ks in starts:
            per_head = []
            for h in range(HEADS_SB):
                g = h // 2
                width = min(LANE, W_SB - g * LANE)
                kp = k_ref[0, pl.ds(ks, tk), g * LANE:g * LANE + width]
                per_head.append(_dot(kp, qm_ref[h][:width, :]))
            zs.append(per_head)
        for n in range(len(js)):
            per_head = []
            for z in zs[n]:
                sp = jnp.maximum(z, 0.0) + jnp.log(1.0 + jnp.exp(-jnp.abs(z)))
                per_head.append(jnp.where(strict, sp, 0.0) if (first_is_causal and n == 0) else sp)
            sps.append(per_head)
        for n in range(len(js)):
            per_head = []
            for sp in sps[n]:
                sp_hi = sp.astype(BF16)
                sp_lo = (sp - sp_hi.astype(F32)).astype(BF16)
                per_head.append(_dot(later_w_ref[...], jnp.concatenate([sp_hi, sp_lo], axis=0)))
            laters.append(per_head)
        vts = [vt_ref[0, j] for j in js]
        for h in range(HEADS_SB):
            run, acc = run_ref[h], acc_ref[h]
            for n in range(len(js)):
                a = jnp.exp(zs[n][h] - sps[n][h] + laters[n][h] + run)
                if first_is_causal and n == 0:
                    a = jnp.where(strict, a, 0.0)
                acc = acc + _dot(vts[n][h * HEAD_DIM:(h + 1) * HEAD_DIM, :], a.astype(BF16))
                run = run + laters[n][h][0:1, :] - sps[n][h][0:1, :]
            run_ref[h], acc_ref[h] = run, acc

    def alive():
        return jnp.max(run_ref[...]) > SB_DEAD_LOG

    @pl.when(i == 0)
    def _():
        blocks((i,), True)

    @pl.when(i > 0)
    def _():
        blocks((i, i - 1), True)

    def body(carry):
        t, _ = carry
        blocks((i - 1 - t,), False)
        return t + 1, alive()

    lax.while_loop(lambda carry: (carry[0] < i) & carry[1], body, (jnp.int32(1), alive()))
    o_ref[0] = jnp.concatenate([acc_ref[h].T for h in range(HEADS_SB)], axis=1).astype(o_ref.dtype)


def _sb(sqt, sk, svt):
    b, s, w = sk.shape
    tq = TKC
    return pl.pallas_call(
        _sb_kernel,
        grid=(b, s // tq),
        in_specs=[pl.BlockSpec((1, 1, w, tq), lambda bi, qi: (bi, qi, 0, 0)),
                  pl.BlockSpec((1, s, w), lambda bi, qi: (bi, 0, 0)),
                  pl.BlockSpec((1, s // TKC, w, TKC), lambda bi, qi: (bi, 0, 0, 0))],
        out_specs=pl.BlockSpec((1, tq, w), lambda bi, qi: (bi, qi, 0)),
        out_shape=jax.ShapeDtypeStruct((b, s, w), BF16),
        scratch_shapes=[pltpu.VMEM((HEADS_SB, LANE, tq), BF16),
                        pltpu.VMEM((TKC, 2 * TKC), BF16),
                        pltpu.VMEM((HEADS_SB, 1, tq), F32),
                        pltpu.VMEM((HEADS_SB, HEAD_DIM, tq), F32)],
        compiler_params=pltpu.CompilerParams(
            dimension_semantics=("arbitrary", "arbitrary"), vmem_limit_bytes=VMEM_LIMIT),
        name="sb",
    )(sqt, sk, svt)


NEG_KEY = -0x7149F2CB


def _sortable(score):
    bits = lax.bitcast_convert_type(score, jnp.int32)
    return jnp.where(bits < 0, bits ^ jnp.int32(0x7FFFFFFF), bits)


def _sum_sublane_groups(x):
    parts = [x[r * SUBLANE:(r + 1) * SUBLANE] for r in range(x.shape[0] // SUBLANE)]
    while len(parts) > 1:
        parts = [parts[k] + parts[k + 1] for k in range(0, len(parts) - 1, 2)] + (
            [parts[-1]] if len(parts) % 2 else [])
    return parts[0]


def _tree_sum(parts):
    parts = list(parts)
    while len(parts) > 1:
        parts = [parts[k] + parts[k + 1] for k in range(0, len(parts) - 1, 2)] + (
            [parts[-1]] if len(parts) % 2 else [])
    return parts[0]


def _count_ge16(ref, nkb, cand):
    tk, tq = ref.shape[1], ref.shape[2]
    cand16 = jnp.broadcast_to(cand, (PACKED_ROWS, tq)).astype(jnp.int16)
    one, zero = jnp.ones((PACKED_ROWS, tq), jnp.int16), jnp.zeros((PACKED_ROWS, tq), jnp.int16)
    parts = []
    for j in range(nkb):
        blk = ref[j]
        parts += [jnp.where(blk[r * PACKED_ROWS:(r + 1) * PACKED_ROWS] >= cand16, one, zero)
                  for r in range(tk // PACKED_ROWS)]
    return jnp.sum(_tree_sum(parts).astype(F32), axis=0, keepdims=True)


def _search16(ref, nkb, n_sel):
    tq = ref.shape[2]

    def bit_step(t, thr_u):
        bit = lax.shift_left(jnp.int32(1), jnp.int32(15) - t)
        cand_u = thr_u | bit
        cnt = _count_ge16(ref, nkb, cand_u - 32768)
        return jnp.where(cnt >= n_sel, cand_u, thr_u)

    return lax.fori_loop(0, 16, bit_step, jnp.zeros((1, tq), jnp.int32)) - 32768


def _dsa_kernel(dqt_ref, iqt_ref, iwt_ref, dk2_ref, ik2_ref, dvt_ref, o_ref,
                key_ref, hi_ref, lo_ref, bias_ref, iq8_ref, q5_ref, jmax_ref, *, n_sel, idx_bits):
    tq, tk = TQ_DSA, TK_DSA
    i = pl.program_id(1)
    kloc = lax.broadcasted_iota(jnp.int32, (tk, tq), 0)
    qpos = i * tq + lax.broadcasted_iota(jnp.int32, (tk, tq), 1)
    klimit = (qpos // CHUNK + 1) * CHUNK
    for h in range(IDX_HEADS):
        iq8_ref[:, h * tq:(h + 1) * tq] = _pair_rows(iqt_ref[0, 0, h * HEAD_DIM:(h + 1) * HEAD_DIM, :], h)
    for h in range(HEADS_DSA):
        q5_ref[:, h * tq:(h + 1) * tq] = _pair_rows(dqt_ref[0, 0, h * HEAD_DIM:(h + 1) * HEAD_DIM, :], h)
    iwt = iwt_ref[0, 0]

    def select(nkb):
        for j in range(nkb):
            a = _dot(ik2_ref[0, j * tk:(j + 1) * tk, :], iq8_ref[...])
            sc = jnp.zeros((tk, tq), F32)
            for h in range(IDX_HEADS):
                sc = sc + iwt[h:h + 1, :] * jnp.maximum(a[:, h * tq:(h + 1) * tq], 0.0)
            sc = sc + 0.0
            sc = jnp.where(kloc + j * tk < klimit, sc, NEG)
            key = _sortable(sc)
            key_ref[j] = key
            hi_ref[j] = lax.shift_right_arithmetic(key, jnp.int32(16)).astype(jnp.int16)
            lo_ref[j] = ((key & jnp.int32(0xFFFF)) - 32768).astype(jnp.int16)

        thr_hi = _search16(hi_ref, nkb, n_sel)
        thr_hi16 = jnp.broadcast_to(thr_hi, (PACKED_ROWS, tq)).astype(jnp.int16)
        for j in range(nkb):
            for r in range(tk // PACKED_ROWS):
                rows = slice(r * PACKED_ROWS, (r + 1) * PACKED_ROWS)
                hi = hi_ref[j, rows, :]
                pinned = jnp.where(hi > thr_hi16, jnp.int16(32767), jnp.int16(-32768))
                lo_ref[j, rows, :] = jnp.where(hi == thr_hi16, lo_ref[j, rows, :], pinned)
        thr = thr_hi * 65536 + (_search16(lo_ref, nkb, n_sel) + 32768)

        def count(pred):
            parts = [_sum_sublane_groups(jnp.where(pred(key_ref[j]), 1.0, 0.0)) for j in range(nkb)]
            return jnp.sum(_tree_sum(parts), axis=0, keepdims=True)

        cnt_ge = count(lambda k: k >= thr)
        jmax_ref[...] = jnp.full((1, tq), 2 ** idx_bits - 1, jnp.int32)
        tied = (cnt_ge > n_sel) & (thr > NEG_KEY)

        @pl.when(jnp.max(jnp.where(tied, 1.0, 0.0)) > 0.0)
        def _():
            need = n_sel - count(lambda k: k > thr)
            for j in range(nkb):
                lo_ref[j] = jnp.where(key_ref[j] == thr, kloc + j * tk, 32767).astype(jnp.int16)
            stored = float(nkb * tk)

            def idx_bit(t, jmax):
                bit = lax.shift_left(jnp.int32(1), jnp.int32(idx_bits - 1) - t)
                cand = jmax | bit
                cnt = stored - _count_ge16(lo_ref, nkb, cand)
                return jnp.where(cnt < need, cand, jmax)

            jmax_ref[...] = lax.fori_loop(0, idx_bits, idx_bit, jnp.zeros((1, tq), jnp.int32))

        jmax = jmax_ref[...]
        for j in range(nkb):
            k = key_ref[j]
            pos = kloc + j * tk
            sel = ((k > thr) | ((k == thr) & (pos <= jmax))) & (pos < klimit)
            bias_ref[j] = jnp.where(sel, 0.0, NEG)

    def attend(nkb):
        def qk(j):
            return _dot(dk2_ref[0, j * tk:(j + 1) * tk, :], q5_ref[...])

        m = jnp.full((1, HEADS_DSA * tq), NEG, F32)
        l = jnp.zeros((1, HEADS_DSA * tq), F32)
        acc = jnp.zeros((HEAD_DIM, HEADS_DSA * tq), F32)
        ahead = qk(0)
        for j in range(nkb):
            s = ahead + jnp.concatenate([bias_ref[j]] * HEADS_DSA, axis=1)
            if j + 1 < nkb:
                ahead = qk(j + 1)
            m_new = jnp.maximum(m, jnp.max(s, axis=0, keepdims=True))
            alpha = jnp.exp(m - m_new)
            p = jnp.exp(s - m_new)
            m = m_new
            l = alpha * l + jnp.sum(p, axis=0, keepdims=True)
            acc = alpha * acc + _dot(dvt_ref[0, j], p.astype(BF16))
        out = acc / l
        o_ref[0] = jnp.concatenate([out[:, h * tq:(h + 1) * tq].T for h in range(HEADS_DSA)],
                                   axis=1).astype(o_ref.dtype)

    nkb = (i * tq + tq - 1) // tk + 1
    nkb_even = lax.shift_left(lax.shift_right_logical(nkb + 1, 1), 1)
    for nkb_static in range(2, dk2_ref.shape[1] // tk + 1, 2):
        @pl.when(nkb_even == nkb_static)
        def _(nkb_static=nkb_static):
            select(nkb_static)
            attend(nkb_static)


def _dsa(dqt, iqt, iwt, dk2, ik2, dvt):
    b, s, _ = dk2.shape
    tq, tk = TQ_DSA, TK_DSA
    n_sel = min(DSA_TOPK_MAX, s // 4)
    assert n_sel <= tk and s % (2 * tk) == 0 and tk % tq == 0 and s < 2 ** 15
    idx_bits = max(1, (s - 1).bit_length())
    return pl.pallas_call(
        functools.partial(_dsa_kernel, n_sel=n_sel, idx_bits=idx_bits),
        grid=(b, s // tq),
        in_specs=[pl.BlockSpec((1, 1, W_DSA, tq), lambda bi, qi: (bi, qi, 0, 0)),
                  pl.BlockSpec((1, 1, W_IDX, tq), lambda bi, qi: (bi, qi, 0, 0)),
                  pl.BlockSpec((1, 1, IDX_HEADS, tq), lambda bi, qi: (bi, qi, 0, 0)),
                  pl.BlockSpec((1, s, LANE), lambda bi, qi: (bi, 0, 0)),
                  pl.BlockSpec((1, s, LANE), lambda bi, qi: (bi, 0, 0)),
                  pl.BlockSpec((1, s // tk, HEAD_DIM, tk), lambda bi, qi: (bi, 0, 0, 0))],
        out_specs=pl.BlockSpec((1, tq, W_DSA), lambda bi, qi: (bi, qi, 0)),
        out_shape=jax.ShapeDtypeStruct((b, s, W_DSA), BF16),
        scratch_shapes=[pltpu.VMEM((s // tk, tk, tq), jnp.int32),
                        pltpu.VMEM((s // tk, tk, tq), jnp.int16),
                        pltpu.VMEM((s // tk, tk, tq), jnp.int16),
                        pltpu.VMEM((s // tk, tk, tq), F32),
                        pltpu.VMEM((LANE, IDX_HEADS * tq), BF16),
                        pltpu.VMEM((LANE, HEADS_DSA * tq), BF16),
                        pltpu.VMEM((1, tq), jnp.int32)],
        compiler_params=pltpu.CompilerParams(
            dimension_semantics=("arbitrary", "arbitrary"), vmem_limit_bytes=VMEM_LIMIT),
        name="dsa",
    )(dqt, iqt, iwt, dk2, ik2, dvt)


def _layer_norm(y, g, b):
    mu = jnp.mean(y, axis=1, keepdims=True)
    d = y - mu
    var = jnp.mean(d * d, axis=1, keepdims=True)
    return d * lax.rsqrt(var + LN_EPS) * g + b


def _merge_kernel(x_ref, of_ref, os_ref, od_ref, wg_ref, wuf_ref, wus_ref, wud_ref, wo_ref,
                  g_ref, b_ref, o_ref):
    half = x_ref.shape[0] // MLP_SPLIT
    for r in range(MLP_SPLIT):
        rows = slice(r * half, (r + 1) * half)
        x = x_ref[rows, :]
        xb = x.astype(BF16)
        merged = None
        for n, (br_ref, wu_ref) in enumerate(((of_ref, wuf_ref), (os_ref, wus_ref), (od_ref, wud_ref))):
            gate = jax.nn.sigmoid(_dot(xb, wg_ref[:, n * D_MODEL:(n + 1) * D_MODEL]))
            term = gate * _dot(br_ref[rows, :], wu_ref[...])
            merged = term if merged is None else merged + term
        y = DEEPNORM_ALPHA * x + _dot(merged.astype(BF16), wo_ref[...])
        o_ref[rows, :] = _layer_norm(y, g_ref[...], b_ref[...])


def _const_spec(shape):
    return pl.BlockSpec(shape, lambda i: (0,) * len(shape), pipeline_mode=pl.Buffered(1))


def _merge(x2, o_fox, o_sb, o_dsa, wg, wuf, wus, wud, wo, g, bvec):
    t = x2.shape[0]
    tm = min(TM_MLP, t)

    def rows(width):
        return pl.BlockSpec((tm, width), lambda i: (i, 0))

    return pl.pallas_call(
        _merge_kernel,
        grid=(t // tm,),
        in_specs=[rows(D_MODEL), rows(W_FOX), rows(W_SB), rows(W_DSA),
                  _const_spec(wg.shape), _const_spec(wuf.shape), _const_spec(wus.shape),
                  _const_spec(wud.shape), _const_spec(wo.shape),
                  _const_spec(g.shape), _const_spec(bvec.shape)],
        out_specs=rows(D_MODEL),
        out_shape=jax.ShapeDtypeStruct((t, D_MODEL), F32),
        compiler_params=pltpu.CompilerParams(
            dimension_semantics=("arbitrary",), vmem_limit_bytes=VMEM_LIMIT),
        name="merge",
    )(x2, o_fox, o_sb, o_dsa, wg, wuf, wus, wud, wo, g, bvec)


FF_CHUNK = 1024


def _ffn_kernel(x_ref, p_ref, w1_ref, w2_ref, wpg_ref, wp_ref, g_ref, b_ref, o_ref):
    half = x_ref.shape[0] // MLP_SPLIT
    for r in range(MLP_SPLIT):
        rows = slice(r * half, (r + 1) * half)
        x = x_ref[rows, :]
        xb = x.astype(BF16)
        acc = None
        for c in range(D_FF // FF_CHUNK):
            h = jnp.maximum(_dot(xb, w1_ref[:, c * FF_CHUNK:(c + 1) * FF_CHUNK]), 0.0)
            term = _dot((h * h).astype(BF16), w2_ref[c * FF_CHUNK:(c + 1) * FF_CHUNK, :])
            acc = term if acc is None else acc + term
        ple = jax.nn.sigmoid(_dot(xb, wpg_ref[...])) * _dot(p_ref[0, rows, :].astype(BF16), wp_ref[...])
        y = DEEPNORM_ALPHA * x + acc + ple
        o_ref[rows, :] = _layer_norm(y, g_ref[...], b_ref[...])


def _ffn(x2, p3, layer, w1, w2, wpg, wp, g, bvec):
    t = x2.shape[0]
    tm = min(TM_MLP, t)

    def rows(width):
        return pl.BlockSpec((tm, width), lambda i: (i, 0))

    return pl.pallas_call(
        _ffn_kernel,
        grid=(t // tm,),
        in_specs=[rows(D_MODEL), pl.BlockSpec((1, tm, PLE_DIM), lambda i: (layer, i, 0)),
                  _const_spec(w1.shape), _const_spec(w2.shape), _const_spec(wpg.shape),
                  _const_spec(wp.shape), _const_spec(g.shape), _const_spec(bvec.shape)],
        out_specs=rows(D_MODEL),
        out_shape=jax.ShapeDtypeStruct((t, D_MODEL), F32),
        compiler_params=pltpu.CompilerParams(
            dimension_semantics=("arbitrary",), vmem_limit_bytes=VMEM_LIMIT),
        name="ffn",
    )(x2, p3, w1, w2, wpg, wp, g, bvec)


def _pack_w_in(w):
    d = w.shape[0]
    offs = {}
    o = 0
    for name, width in (("fq", W_FOX), ("fk", W_FOX), ("fv", W_FOX), ("ff", HEADS_FOX),
                        ("sq", W_SB), ("sk", W_SB), ("sv", W_SB),
                        ("dq", W_DSA), ("dk", HEAD_DIM), ("dv", HEAD_DIM),
                        ("iq", W_IDX), ("ik", HEAD_DIM), ("iw", IDX_HEADS),
                        ("g", N_BRANCH * D_MODEL)):
        offs[name] = w[:, o:o + width]
        o += width

    def z(n):
        return jnp.zeros((d, n), w.dtype)

    packed = jnp.concatenate(
        [offs["ff"], z(IW_LANE - HEADS_FOX), offs["iw"], z(LANE - IW_LANE - IDX_HEADS),
         offs["dk"], offs["dk"], offs["ik"], offs["ik"], offs["fk"], offs["sk"], z(W_SB_PAD - W_SB)], axis=1)
    assert packed.shape[1] == C_PACK
    packed_t = jnp.concatenate([offs["fq"], offs["sq"], offs["dq"], offs["iq"],
                                offs["fv"], offs["sv"], offs["dv"]], axis=1).T
    assert packed_t.shape[0] == R_PACK
    return packed.astype(BF16), packed_t.astype(BF16), offs["g"].astype(BF16)


def _rope_tables(s):
    inv = ROPE_THETA ** (-jnp.arange(HALF, dtype=F32) / HALF)
    ang = jnp.arange(s, dtype=F32)[:, None] * inv[None, :]
    cos, sin = jnp.cos(ang), jnp.sin(ang)
    row_major = jnp.concatenate([cos, cos, cos, cos, -sin, sin, -sin, sin], axis=1)
    transposed = jnp.concatenate([cos, cos, -sin, sin], axis=1).T
    return row_major, transposed


def _placement():
    rows_q, cols_q, rows_k, cols_k = [], [], [], []
    for h in range(HEADS_FOX):
        for t in range(N_CTERM):
            rows_q.append(t * LANE + h), cols_q.append(CSTRIDE * h + t)
            rows_k.append(t * LANE + h), cols_k.append(LANE + CSTRIDE * h + N_CTERM + t)
    place = jnp.zeros((N_CTERM * LANE, 2 * LANE), F32)
    place = place.at[jnp.array(rows_q), jnp.array(cols_q)].set(1.0)
    place = place.at[jnp.array(rows_k), jnp.array(cols_k)].set(-1.0)
    lane = jnp.arange(2 * LANE)
    head_lane = (lane % LANE) % CSTRIDE
    in_heads = (lane % LANE) < CSTRIDE * HEADS_FOX
    ones_q = (lane < LANE) & in_heads & (head_lane >= N_CTERM) & (head_lane < 2 * N_CTERM)
    ones_k = (lane >= LANE) & in_heads & (head_lane < N_CTERM)
    ones_row = jnp.where(ones_q | ones_k, 1.0, 0.0).astype(F32)[None, :]
    return place.astype(BF16), ones_row


def kernel(x, p, w_in, b_forget, w_up_fox, w_up_sb, w_up_dsa, w_out, ln1_g, ln1_b,
           w_ff_in, w_ff_out, w_ple, w_ple_gate, ln2_g, ln2_b):
    b, s, d = x.shape
    depth = w_in.shape[0]
    rope_tab, rope_tab_t = _rope_tables(s)
    place, ones_row = _placement()
    w_in_rows = w_in.reshape(depth * d, w_in.shape[2])
    for i in range(depth):
        w_pack, wt_pack, w_gate = _pack_w_in(w_in_rows[i * d:(i + 1) * d])
        bias_row = jnp.zeros((1, LANE), F32).at[0, :HEADS_FOX].set(b_forget[i].astype(F32))
        (fqt, fqct, fk, fkc, fvt, sqt, sk, svt, dqt, iqt, dk2, ik2, dvt, iwt) = _proj(
            x, w_pack, wt_pack, bias_row, rope_tab, rope_tab_t, place, ones_row)
        o_fox = _fox(fqt, fqct, fk, fkc, fvt)
        o_sb = _sb(sqt, sk, svt)
        o_dsa = _dsa(dqt, iqt, iwt, dk2, ik2, dvt)
        x1 = _merge(x.reshape(b * s, d), o_fox.reshape(b * s, W_FOX), o_sb.reshape(b * s, W_SB),
                    o_dsa.reshape(b * s, W_DSA), w_gate,
                    w_up_fox[i].astype(BF16), w_up_sb[i].astype(BF16), w_up_dsa[i].astype(BF16),
                    w_out[i].astype(BF16), ln1_g[i][None, :], ln1_b[i][None, :])
        x2 = _ffn(x1, p.reshape(depth, b * s, PLE_DIM), i, w_ff_in[i].astype(BF16), w_ff_out[i].astype(BF16),
                  w_ple_gate[i].astype(BF16), w_ple[i].astype(BF16), ln2_g[i][None, :], ln2_b[i][None, :])
        x = x2.reshape(b, s, d)
    return x
```

```python
import functools

import jax
import jax.numpy as jnp
from jax import lax
from jax.experimental import pallas as pl
from jax.experimental.pallas import tpu as pltpu

D_MODEL = 1024
HEAD_DIM = 64
HALF = HEAD_DIM // 2
HEADS_FOX = 6
HEADS_SB = 5
HEADS_DSA = 5
IDX_HEADS = 8
CHUNK = 64
DSA_TOPK_MAX = 256
D_FF = 4 * D_MODEL
PLE_DIM = 256
ROPE_THETA = 10000.0
LN_EPS = 1e-5
N_BRANCH = 3
NEG = -1e30
DEPTH = 2
DEEPNORM_ALPHA = (2 * DEPTH) ** 0.25
W_FOX = HEADS_FOX * HEAD_DIM
W_SB = HEADS_SB * HEAD_DIM
W_DSA = HEADS_DSA * HEAD_DIM
W_IDX = IDX_HEADS * HEAD_DIM
SCALE = HEAD_DIM ** -0.5

LANE = 128
SUBLANE = 8
PACKED_ROWS = 16
W_SB_PAD = 384
IW_LANE = 8
N_CTERM = 3
VMEM_LIMIT = 56 * 1024 * 1024

TM_PROJ = 512
TKC = 256
TQ_DSA = 256
TK_DSA = TKC
TM_MLP = 1024
MLP_SPLIT = 2

BF16 = jnp.bfloat16
F32 = jnp.float32


def _log_sigmoid(x):
    return jnp.minimum(x, 0.0) - jnp.log(1.0 + jnp.exp(-jnp.abs(x)))


def _dot(a, b):
    return jnp.dot(a, b, preferred_element_type=F32)


def _dot_nt(a, b):
    return lax.dot_general(a, b, (((1,), (1,)), ((), ())), preferred_element_type=F32)


def _split3(v):
    hi = v.astype(BF16)
    r = v - hi.astype(F32)
    mid = r.astype(BF16)
    lo = (r - mid.astype(F32)).astype(BF16)
    return hi, mid, lo


def _dispatch(index, values, body):
    if len(values) == 1:
        body(values[0])
        return
    mid = len(values) // 2
    lax.cond(index < values[mid], lambda: _dispatch(index, values[:mid], body),
             lambda: _dispatch(index, values[mid:], body))


def _pair_rows(qt_h, h):
    z = jnp.zeros_like(qt_h)
    return jnp.concatenate([z, qt_h] if h % 2 else [qt_h, z], axis=0)


OFF_SM = 0
OFF_DK2 = 128
OFF_IK2 = 256
OFF_FK = 384
OFF_SK = 768
C_PACK = 1152
OFF_TFQ, OFF_TSQ, OFF_TDQ, OFF_TIQ = 0, 384, 704, 1024
OFF_TFV, OFF_TSV, OFF_TDV = 1536, 1920, 2240
R_PACK = 2304
CSTRIDE = PACKED_ROWS


def _swap_halves(h):
    lane = lax.broadcasted_iota(jnp.int32, h.shape, 1)
    first = (lane % HEAD_DIM) < HALF
    return jnp.where(first, pltpu.roll(h, LANE - HALF, 1), pltpu.roll(h, HALF, 1))


def _rope_rows(h, cos, sin):
    return h * cos + _swap_halves(h) * sin


def _rope_t(ht, cos_t, sin_t):
    parts = []
    for h in range(ht.shape[0] // HEAD_DIM):
        x = ht[h * HEAD_DIM:(h + 1) * HEAD_DIM]
        swapped = jnp.concatenate([x[HALF:], x[:HALF]], axis=0)
        parts.append(x * cos_t + swapped * sin_t)
    return jnp.concatenate(parts, axis=0)


def _proj_kernel(x_ref, w_ref, wt_ref, bias_ref, tab_ref, tabt_ref, place_ref, ones_ref,
                 fqt_ref, fqct_ref, fk_ref, fkc_ref, fvt_ref, sqt_ref, sk_ref, svt_ref,
                 dqt_ref, iqt_ref, dk2_ref, ik2_ref, dvt_ref, iwt_ref, carry_ref):
    tm = x_ref.shape[1]

    @pl.when(pl.program_id(1) == 0)
    def _():
        carry_ref[...] = jnp.zeros_like(carry_ref)

    xb = x_ref[0].astype(BF16)

    def seg(off, width):
        return _dot(xb, w_ref[:, off:off + width])

    def seg_t(off, nrows):
        return _dot_nt(wt_ref[off:off + nrows, :], xb)

    def put_chunks(ref, val, chunk):
        for r in range(tm // chunk):
            ref[0, r] = val[:, r * chunk:(r + 1) * chunk].astype(ref.dtype)

    narrow = seg(OFF_SM, 3 * LANE)
    small = narrow[:, 0:LANE]
    logf = _log_sigmoid(small + bias_ref[...])

    row = lax.broadcasted_iota(jnp.int32, (tm, tm), 0)
    col = lax.broadcasted_iota(jnp.int32, (tm, tm), 1)
    tri = jnp.where(col <= row, 1.0, 0.0).astype(BF16)
    c = _dot(tri, jnp.concatenate(_split3(logf), axis=1))
    c = c[:, 0:LANE] + c[:, LANE:2 * LANE] + c[:, 2 * LANE:3 * LANE] + carry_ref[...]
    carry_ref[...] = c[tm - 1:tm, :]
    placed = _dot(jnp.concatenate(_split3(c), axis=1), place_ref[...]) + ones_ref[...]
    put_chunks(fqct_ref, placed[:, 0:LANE].T, TKC)
    fkc_ref[0] = placed[:, LANE:2 * LANE].astype(BF16)
    put_chunks(iwt_ref, small.T[IW_LANE:IW_LANE + IDX_HEADS], TQ_DSA)

    keys = seg(OFF_FK, W_FOX + W_SB_PAD)
    fk_ref[0] = keys[:, 0:W_FOX].astype(BF16)
    sk_ref[0] = keys[:, W_FOX:W_FOX + W_SB].astype(BF16)
    cos, sin = tab_ref[:, 0:LANE], tab_ref[:, LANE:2 * LANE]
    dk2_ref[0] = _rope_rows(narrow[:, OFF_DK2:OFF_DK2 + LANE], cos, sin).astype(BF16)
    ik2_ref[0] = _rope_rows(narrow[:, OFF_IK2:OFF_IK2 + LANE], cos, sin).astype(BF16)

    cos_t, sin_t = tabt_ref[0:HEAD_DIM, :], tabt_ref[HEAD_DIM:2 * HEAD_DIM, :]
    put_chunks(fqt_ref, seg_t(OFF_TFQ, W_FOX) * SCALE, TKC)
    put_chunks(sqt_ref, seg_t(OFF_TSQ, W_SB) * SCALE, TKC)
    put_chunks(dqt_ref, _rope_t(seg_t(OFF_TDQ, W_DSA), cos_t, sin_t) * SCALE, TQ_DSA)
    put_chunks(iqt_ref, _rope_t(seg_t(OFF_TIQ, W_IDX), cos_t, sin_t), TQ_DSA)
    put_chunks(fvt_ref, seg_t(OFF_TFV, W_FOX), TKC)
    put_chunks(svt_ref, seg_t(OFF_TSV, W_SB), TKC)
    put_chunks(dvt_ref, seg_t(OFF_TDV, HEAD_DIM), TKC)


def _proj(x, w_pack, wt_pack, bias_row, rope_tab, rope_tab_t, place, ones_row):
    b, s, d = x.shape
    tm = min(TM_PROJ, s)
    grid = (b, s // tm)

    def rows(width, dtype=BF16):
        return (jax.ShapeDtypeStruct((b, s, width), dtype),
                pl.BlockSpec((1, tm, width), lambda bi, si: (bi, si, 0)))

    def chunks(nrows, chunk, dtype=BF16):
        return (jax.ShapeDtypeStruct((b, s // chunk, nrows, chunk), dtype),
                pl.BlockSpec((1, tm // chunk, nrows, chunk), lambda bi, si: (bi, si, 0, 0)))

    def const(shape):
        return pl.BlockSpec(shape, lambda bi, si: (0,) * len(shape))

    outs = [chunks(W_FOX, TKC), chunks(LANE, TKC), rows(W_FOX), rows(LANE), chunks(W_FOX, TKC),
            chunks(W_SB, TKC), rows(W_SB), chunks(W_SB, TKC),
            chunks(W_DSA, TQ_DSA), chunks(W_IDX, TQ_DSA), rows(LANE), rows(LANE), chunks(HEAD_DIM, TKC),
            chunks(IDX_HEADS, TQ_DSA, F32)]
    return pl.pallas_call(
        _proj_kernel,
        grid=grid,
        in_specs=[pl.BlockSpec((1, tm, d), lambda bi, si: (bi, si, 0)),
                  const((d, C_PACK)), const((R_PACK, d)), const((1, LANE)),
                  pl.BlockSpec((tm, 2 * LANE), lambda bi, si: (si, 0)),
                  pl.BlockSpec((2 * HEAD_DIM, tm), lambda bi, si: (0, si)),
                  const(place.shape), const(ones_row.shape)],
        out_specs=[o[1] for o in outs],
        out_shape=[o[0] for o in outs],
        scratch_shapes=[pltpu.VMEM((1, LANE), F32)],
        compiler_params=pltpu.CompilerParams(
            dimension_semantics=("arbitrary", "arbitrary"), vmem_limit_bytes=VMEM_LIMIT),
        name="proj",
    )(x, w_pack, wt_pack, bias_row, rope_tab, rope_tab_t, place, ones_row)


def _fox_kernel(qt_ref, qct_ref, k_ref, kc_ref, vt_ref, o_ref, qcat_ref, m_ref, l_ref, acc_ref):
    tq = tk = TKC
    i = pl.program_id(1)
    krow = lax.broadcasted_iota(jnp.int32, (tk, tq), 0)
    qcol = lax.broadcasted_iota(jnp.int32, (tk, tq), 1)
    causal = krow <= qcol

    for h in range(HEADS_FOX):
        gate_rows = jnp.concatenate(
            ([jnp.zeros((CSTRIDE * h, tq), BF16)] if h else [])
            + [qct_ref[0, 0, CSTRIDE * h:CSTRIDE * (h + 1), :],
               jnp.zeros((LANE - CSTRIDE * (h + 1), tq), BF16)], axis=0)
        qcat_ref[h] = jnp.concatenate(
            [_pair_rows(qt_ref[0, 0, h * HEAD_DIM:(h + 1) * HEAD_DIM, :], h), gate_rows], axis=0)
    m_ref[...] = jnp.full(m_ref.shape, NEG, F32)
    l_ref[...] = jnp.zeros(l_ref.shape, F32)
    acc_ref[...] = jnp.zeros(acc_ref.shape, F32)

    def qk(js):
        kcs = [kc_ref[0, j * tk:(j + 1) * tk, :] for j in js]
        scores = []
        for g in range(HEADS_FOX // 2):
            kcats = [jnp.concatenate([k_ref[0, j * tk:(j + 1) * tk, g * LANE:(g + 1) * LANE], kc], axis=1)
                     for j, kc in zip(js, kcs)]
            for h in (2 * g, 2 * g + 1):
                scores.append([_dot(kcat, qcat_ref[h]) for kcat in kcats])
        return scores

    def softmax_pv(js, scores, masked):
        vts = [vt_ref[0, j] for j in js]
        for h in range(HEADS_FOX):
            ss = [jnp.where(causal, s, NEG) if masked else s for s in scores[h]]
            m_old = m_ref[h]
            m_new = m_old
            for s in ss:
                m_new = jnp.maximum(m_new, jnp.max(s, axis=0, keepdims=True))
            alpha = jnp.exp(m_old - m_new)
            ps = [jnp.exp(s - m_new) for s in ss]
            m_ref[h] = m_new
            l_ref[h] = alpha * l_ref[h] + sum(jnp.sum(p, axis=0, keepdims=True) for p in ps)
            acc_ref[h] = alpha * acc_ref[h] + sum(
                _dot(vt[h * HEAD_DIM:(h + 1) * HEAD_DIM, :], p.astype(BF16)) for vt, p in zip(vts, ps))

    def attend(i_static):
        groups = [((i_static,), True)] + [((j, j + 1), False) for j in range(0, i_static - 1, 2)]
        if i_static % 2:
            groups.append(((i_static - 1,), False))
        ahead = qk(groups[0][0])
        for n, (js, masked) in enumerate(groups):
            scores = ahead
            if n + 1 < len(groups):
                ahead = qk(groups[n + 1][0])
            softmax_pv(js, scores, masked)

    _dispatch(i, list(range(k_ref.shape[1] // tk)), attend)

    o_ref[0] = jnp.concatenate([(acc_ref[h] / l_ref[h]).T for h in range(HEADS_FOX)],
                               axis=1).astype(o_ref.dtype)


def _fox(fqt, fqct, fk, fkc, fvt):
    b, s, w = fk.shape
    tq = TKC
    return pl.pallas_call(
        _fox_kernel,
        grid=(b, s // tq),
        in_specs=[pl.BlockSpec((1, 1, w, tq), lambda bi, qi: (bi, qi, 0, 0)),
                  pl.BlockSpec((1, 1, LANE, tq), lambda bi, qi: (bi, qi, 0, 0)),
                  pl.BlockSpec((1, s, w), lambda bi, qi: (bi, 0, 0)),
                  pl.BlockSpec((1, s, LANE), lambda bi, qi: (bi, 0, 0)),
                  pl.BlockSpec((1, s // TKC, w, TKC), lambda bi, qi: (bi, 0, 0, 0))],
        out_specs=pl.BlockSpec((1, tq, w), lambda bi, qi: (bi, qi, 0)),
        out_shape=jax.ShapeDtypeStruct((b, s, w), BF16),
        scratch_shapes=[pltpu.VMEM((HEADS_FOX, 2 * LANE, tq), BF16),
                        pltpu.VMEM((HEADS_FOX, 1, tq), F32),
                        pltpu.VMEM((HEADS_FOX, 1, tq), F32),
                        pltpu.VMEM((HEADS_FOX, HEAD_DIM, tq), F32)],
        compiler_params=pltpu.CompilerParams(
            dimension_semantics=("arbitrary", "arbitrary"), vmem_limit_bytes=VMEM_LIMIT),
        name="fox",
    )(fqt, fqct, fk, fkc, fvt)


SB_DEAD_LOG = -110.0

def _sb_kernel(qt_ref, k_ref, vt_ref, o_ref, qm_ref, later_w_ref, run_ref, acc_ref):
    tq = tk = TKC
    i = pl.program_id(1)
    krow = lax.broadcasted_iota(jnp.int32, (tk, tq), 0)
    qcol = lax.broadcasted_iota(jnp.int32, (tk, tq), 1)
    strict = krow < qcol
    r = lax.broadcasted_iota(jnp.int32, (tk, 2 * tk), 0)
    c = lax.broadcasted_iota(jnp.int32, (tk, 2 * tk), 1) % tk
    later_w_ref[...] = jnp.where(c > r, -1.0, 0.0).astype(BF16)
    for h in range(HEADS_SB):
        qm_ref[h] = _pair_rows(qt_ref[0, 0, h * HEAD_DIM:(h + 1) * HEAD_DIM, :], h)
    run_ref[...] = jnp.zeros(run_ref.shape, F32)
    acc_ref[...] = jnp.zeros(acc_ref.shape, F32)

    def blocks(js, first_is_causal):
        starts = [pl.multiple_of(j * tk, tk) for j in js]
        zs, sps, laters = [], [], []
        for ks in starts:
            per_head = []
            for h in range(HEADS_SB):
                g = h // 2
                width = min(LANE, W_SB - g * LANE)
                kp = k_ref[0, pl.ds(ks, tk), g * LANE:g * LANE + width]
                per_head.append(_dot(kp, qm_ref[h][:width, :]))
            zs.append(per_head)
        for n in range(len(js)):
            per_head = []
            for z in zs[n]:
                sp = jnp.maximum(z, 0.0) + jnp.log(1.0 + jnp.exp(-jnp.abs(z)))
                per_head.append(jnp.where(strict, sp, 0.0) if (first_is_causal and n == 0) else sp)
            sps.append(per_head)
        for n in range(len(js)):
            per_head = []
            for sp in sps[n]:
                sp_hi = sp.astype(BF16)
                sp_lo = (sp - sp_hi.astype(F32)).astype(BF16)
                per_head.append(_dot(later_w_ref[...], jnp.concatenate([sp_hi, sp_lo], axis=0)))
            laters.append(per_head)
        vts = [vt_ref[0, j] for j in js]
        for h in range(HEADS_SB):
            run, acc = run_ref[h], acc_ref[h]
            for n in range(len(js)):
                a = jnp.exp(zs[n][h] - sps[n][h] + laters[n][h] + run)
                if first_is_causal and n == 0:
                    a = jnp.where(strict, a, 0.0)
                acc = acc + _dot(vts[n][h * HEAD_DIM:(h + 1) * HEAD_DIM, :], a.astype(BF16))
                run = run + laters[n][h][0:1, :] - sps[n][h][0:1, :]
            run_ref[h], acc_ref[h] = run, acc

    def alive():
        return jnp.max(run_ref[...]) > SB_DEAD_LOG

    @pl.when(i == 0)
    def _():
        blocks((i,), True)

    @pl.when(i > 0)
    def _():
        blocks((i, i - 1), True)

    def body(carry):
        t, _ = carry
        blocks((i - 1 - t,), False)
        return t + 1, alive()

    lax.while_loop(lambda carry: (carry[0] < i) & carry[1], body, (jnp.int32(1), alive()))
    o_ref[0] = jnp.concatenate([acc_ref[h].T for h in range(HEADS_SB)], axis=1).astype(o_ref.dtype)


def _sb(sqt, sk, svt):
    b, s, w = sk.shape
    tq = TKC
    return pl.pallas_call(
        _sb_kernel,
        grid=(b, s // tq),
        in_specs=[pl.BlockSpec((1, 1, w, tq), lambda bi, qi: (bi, qi, 0, 0)),
                  pl.BlockSpec((1, s, w), lambda bi, qi: (bi, 0, 0)),
                  pl.BlockSpec((1, s // TKC, w, TKC), lambda bi, qi: (bi, 0, 0, 0))],
        out_specs=pl.BlockSpec((1, tq, w), lambda bi, qi: (bi, qi, 0)),
        out_shape=jax.ShapeDtypeStruct((b, s, w), BF16),
        scratch_shapes=[pltpu.VMEM((HEADS_SB, LANE, tq), BF16),
                        pltpu.VMEM((TKC, 2 * TKC), BF16),
                        pltpu.VMEM((HEADS_SB, 1, tq), F32),
                        pltpu.VMEM((HEADS_SB, HEAD_DIM, tq), F32)],
        compiler_params=pltpu.CompilerParams(
            dimension_semantics=("arbitrary", "arbitrary"), vmem_limit_bytes=VMEM_LIMIT),
        name="sb",
    )(sqt, sk, svt)


NEG_KEY = -0x7149F2CB


def _sortable(score):
    bits = lax.bitcast_convert_type(score, jnp.int32)
    return jnp.where(bits < 0, bits ^ jnp.int32(0x7FFFFFFF), bits)


def _sum_sublane_groups(x):
    parts = [x[r * SUBLANE:(r + 1) * SUBLANE] for r in range(x.shape[0] // SUBLANE)]
    while len(parts) > 1:
        parts = [parts[k] + parts[k + 1] for k in range(0, len(parts) - 1, 2)] + (
            [parts[-1]] if len(parts) % 2 else [])
    return parts[0]


def _tree_sum(parts):
    parts = list(parts)
    while len(parts) > 1:
        parts = [parts[k] + parts[k + 1] for k in range(0, len(parts) - 1, 2)] + (
            [parts[-1]] if len(parts) % 2 else [])
    return parts[0]


def _count_ge16(ref, nkb, cand):
    tk, tq = ref.shape[1], ref.shape[2]
    cand16 = jnp.broadcast_to(cand, (PACKED_ROWS, tq)).astype(jnp.int16)
    one, zero = jnp.ones((PACKED_ROWS, tq), jnp.int16), jnp.zeros((PACKED_ROWS, tq), jnp.int16)
    parts = []
    for j in range(nkb):
        blk = ref[j]
        parts += [jnp.where(blk[r * PACKED_ROWS:(r + 1) * PACKED_ROWS] >= cand16, one, zero)
                  for r in range(tk // PACKED_ROWS)]
    return jnp.sum(_tree_sum(parts).astype(F32), axis=0, keepdims=True)


def _search16(ref, nkb, n_sel):
    tq = ref.shape[2]

    def bit_step(t, thr_u):
        bit = lax.shift_left(jnp.int32(1), jnp.int32(15) - t)
        cand_u = thr_u | bit
        cnt = _count_ge16(ref, nkb, cand_u - 32768)
        return jnp.where(cnt >= n_sel, cand_u, thr_u)

    return lax.fori_loop(0, 16, bit_step, jnp.zeros((1, tq), jnp.int32)) - 32768


def _dsa_kernel(dqt_ref, iqt_ref, iwt_ref, dk2_ref, ik2_ref, dvt_ref, o_ref,
                key_ref, hi_ref, lo_ref, bias_ref, iq8_ref, q5_ref, jmax_ref, *, n_sel, idx_bits):
    tq, tk = TQ_DSA, TK_DSA
    i = pl.program_id(1)
    kloc = lax.broadcasted_iota(jnp.int32, (tk, tq), 0)
    qpos = i * tq + lax.broadcasted_iota(jnp.int32, (tk, tq), 1)
    klimit = (qpos // CHUNK + 1) * CHUNK
    for h in range(IDX_HEADS):
        iq8_ref[:, h * tq:(h + 1) * tq] = _pair_rows(iqt_ref[0, 0, h * HEAD_DIM:(h + 1) * HEAD_DIM, :], h)
    for h in range(HEADS_DSA):
        q5_ref[:, h * tq:(h + 1) * tq] = _pair_rows(dqt_ref[0, 0, h * HEAD_DIM:(h + 1) * HEAD_DIM, :], h)
    iwt = iwt_ref[0, 0]

    def select(nkb):
        for j in range(nkb):
            a = _dot(ik2_ref[0, j * tk:(j + 1) * tk, :], iq8_ref[...])
            sc = jnp.zeros((tk, tq), F32)
            for h in range(IDX_HEADS):
                sc = sc + iwt[h:h + 1, :] * jnp.maximum(a[:, h * tq:(h + 1) * tq], 0.0)
            sc = sc + 0.0
            sc = jnp.where(kloc + j * tk < klimit, sc, NEG)
            key = _sortable(sc)
            key_ref[j] = key
            hi_ref[j] = lax.shift_right_arithmetic(key, jnp.int32(16)).astype(jnp.int16)
            lo_ref[j] = ((key & jnp.int32(0xFFFF)) - 32768).astype(jnp.int16)

        thr_hi = _search16(hi_ref, nkb, n_sel)
        thr_hi16 = jnp.broadcast_to(thr_hi, (PACKED_ROWS, tq)).astype(jnp.int16)
        for j in range(nkb):
            for r in range(tk // PACKED_ROWS):
                rows = slice(r * PACKED_ROWS, (r + 1) * PACKED_ROWS)
                hi = hi_ref[j, rows, :]
                pinned = jnp.where(hi > thr_hi16, jnp.int16(32767), jnp.int16(-32768))
                lo_ref[j, rows, :] = jnp.where(hi == thr_hi16, lo_ref[j, rows, :], pinned)
        thr = thr_hi * 65536 + (_search16(lo_ref, nkb, n_sel) + 32768)

        def count(pred):
            parts = [_sum_sublane_groups(jnp.where(pred(key_ref[j]), 1.0, 0.0)) for j in range(nkb)]
            return jnp.sum(_tree_sum(parts), axis=0, keepdims=True)

        cnt_ge = count(lambda k: k >= thr)
        jmax_ref[...] = jnp.full((1, tq), 2 ** idx_bits - 1, jnp.int32)
        tied = (cnt_ge > n_sel) & (thr > NEG_KEY)

        @pl.when(jnp.max(jnp.where(tied, 1.0, 0.0)) > 0.0)
        def _():
            need = n_sel - count(lambda k: k > thr)
            for j in range(nkb):
                lo_ref[j] = jnp.where(key_ref[j] == thr, kloc + j * tk, 32767).astype(jnp.int16)
            stored = float(nkb * tk)

            def idx_bit(t, jmax):
                bit = lax.shift_left(jnp.int32(1), jnp.int32(idx_bits - 1) - t)
                cand = jmax | bit
                cnt = stored - _count_ge16(lo_ref, nkb, cand)
                return jnp.where(cnt < need, cand, jmax)

            jmax_ref[...] = lax.fori_loop(0, idx_bits, idx_bit, jnp.zeros((1, tq), jnp.int32))

        jmax = jmax_ref[...]
        for j in range(nkb):
            k = key_ref[j]
            pos = kloc + j * tk
            sel = ((k > thr) | ((k == thr) & (pos <= jmax))) & (pos < klimit)
            bias_ref[j] = jnp.where(sel, 0.0, NEG)

    def attend(nkb):
        def qk(j):
            return _dot(dk2_ref[0, j * tk:(j + 1) * tk, :], q5_ref[...])

        m = jnp.full((1, HEADS_DSA * tq), NEG, F32)
        l = jnp.zeros((1, HEADS_DSA * tq), F32)
        acc = jnp.zeros((HEAD_DIM, HEADS_DSA * tq), F32)
        ahead = qk(0)
        for j in range(nkb):
            s = ahead + jnp.concatenate([bias_ref[j]] * HEADS_DSA, axis=1)
            if j + 1 < nkb:
                ahead = qk(j + 1)
            m_new = jnp.maximum(m, jnp.max(s, axis=0, keepdims=True))
            alpha = jnp.exp(m - m_new)
            p = jnp.exp(s - m_new)
            m = m_new
            l = alpha * l + jnp.sum(p, axis=0, keepdims=True)
            acc = alpha * acc + _dot(dvt_ref[0, j], p.astype(BF16))
        out = acc / l
        o_ref[0] = jnp.concatenate([out[:, h * tq:(h + 1) * tq].T for h in range(HEADS_DSA)],
                                   axis=1).astype(o_ref.dtype)

    nkb = (i * tq + tq - 1) // tk + 1
    nkb_even = lax.shift_left(lax.shift_right_logical(nkb + 1, 1), 1)
    def select_and_attend(nkb_static):
        select(nkb_static)
        attend(nkb_static)

    _dispatch(nkb_even, list(range(2, dk2_ref.shape[1] // tk + 1, 2)), select_and_attend)


def _dsa(dqt, iqt, iwt, dk2, ik2, dvt):
    b, s, _ = dk2.shape
    tq, tk = TQ_DSA, TK_DSA
    n_sel = min(DSA_TOPK_MAX, s // 4)
    assert n_sel <= tk and s % (2 * tk) == 0 and tk % tq == 0 and s < 2 ** 15
    idx_bits = max(1, (s - 1).bit_length())
    return pl.pallas_call(
        functools.partial(_dsa_kernel, n_sel=n_sel, idx_bits=idx_bits),
        grid=(b, s // tq),
        in_specs=[pl.BlockSpec((1, 1, W_DSA, tq), lambda bi, qi: (bi, qi, 0, 0)),
                  pl.BlockSpec((1, 1, W_IDX, tq), lambda bi, qi: (bi, qi, 0, 0)),
                  pl.BlockSpec((1, 1, IDX_HEADS, tq), lambda bi, qi: (bi, qi, 0, 0)),
                  pl.BlockSpec((1, s, LANE), lambda bi, qi: (bi, 0, 0)),
                  pl.BlockSpec((1, s, LANE), lambda bi, qi: (bi, 0, 0)),
                  pl.BlockSpec((1, s // tk, HEAD_DIM, tk), lambda bi, qi: (bi, 0, 0, 0))],
        out_specs=pl.BlockSpec((1, tq, W_DSA), lambda bi, qi: (bi, qi, 0)),
        out_shape=jax.ShapeDtypeStruct((b, s, W_DSA), BF16),
        scratch_shapes=[pltpu.VMEM((s // tk, tk, tq), jnp.int32),
                        pltpu.VMEM((s // tk, tk, tq), jnp.int16),
                        pltpu.VMEM((s // tk, tk, tq), jnp.int16),
                        pltpu.VMEM((s // tk, tk, tq), F32),
                        pltpu.VMEM((LANE, IDX_HEADS * tq), BF16),
                        pltpu.VMEM((LANE, HEADS_DSA * tq), BF16),
                        pltpu.VMEM((1, tq), jnp.int32)],
        compiler_params=pltpu.CompilerParams(
            dimension_semantics=("arbitrary", "arbitrary"), vmem_limit_bytes=VMEM_LIMIT),
        name="dsa",
    )(dqt, iqt, iwt, dk2, ik2, dvt)


def _layer_norm(y, g, b):
    mu = jnp.mean(y, axis=1, keepdims=True)
    d = y - mu
    var = jnp.mean(d * d, axis=1, keepdims=True)
    return d * lax.rsqrt(var + LN_EPS) * g + b


def _merge_kernel(x_ref, of_ref, os_ref, od_ref, wg_ref, wuf_ref, wus_ref, wud_ref, wo_ref,
                  g_ref, b_ref, o_ref):
    half = x_ref.shape[0] // MLP_SPLIT
    for r in range(MLP_SPLIT):
        rows = slice(r * half, (r + 1) * half)
        x = x_ref[rows, :]
        xb = x.astype(BF16)
        merged = None
        for n, (br_ref, wu_ref) in enumerate(((of_ref, wuf_ref), (os_ref, wus_ref), (od_ref, wud_ref))):
            gate = jax.nn.sigmoid(_dot(xb, wg_ref[:, n * D_MODEL:(n + 1) * D_MODEL]))
            term = gate * _dot(br_ref[rows, :], wu_ref[...])
            merged = term if merged is None else merged + term
        y = DEEPNORM_ALPHA * x + _dot(merged.astype(BF16), wo_ref[...])
        o_ref[rows, :] = _layer_norm(y, g_ref[...], b_ref[...])


def _const_spec(shape):
    return pl.BlockSpec(shape, lambda i: (0,) * len(shape), pipeline_mode=pl.Buffered(1))


def _merge(x2, o_fox, o_sb, o_dsa, wg, wuf, wus, wud, wo, g, bvec):
    t = x2.shape[0]
    tm = min(TM_MLP, t)

    def rows(width):
        return pl.BlockSpec((tm, width), lambda i: (i, 0))

    return pl.pallas_call(
        _merge_kernel,
        grid=(t // tm,),
        in_specs=[rows(D_MODEL), rows(W_FOX), rows(W_SB), rows(W_DSA),
                  _const_spec(wg.shape), _const_spec(wuf.shape), _const_spec(wus.shape),
                  _const_spec(wud.shape), _const_spec(wo.shape),
                  _const_spec(g.shape), _const_spec(bvec.shape)],
        out_specs=rows(D_MODEL),
        out_shape=jax.ShapeDtypeStruct((t, D_MODEL), F32),
        compiler_params=pltpu.CompilerParams(
            dimension_semantics=("arbitrary",), vmem_limit_bytes=VMEM_LIMIT),
        name="merge",
    )(x2, o_fox, o_sb, o_dsa, wg, wuf, wus, wud, wo, g, bvec)


FF_CHUNK = 1024


def _ffn_kernel(x_ref, p_ref, w1_ref, w2_ref, wpg_ref, wp_ref, g_ref, b_ref, o_ref):
    half = x_ref.shape[0] // MLP_SPLIT
    for r in range(MLP_SPLIT):
        rows = slice(r * half, (r + 1) * half)
        x = x_ref[rows, :]
        xb = x.astype(BF16)
        acc = None
        for c in range(D_FF // FF_CHUNK):
            h = jnp.maximum(_dot(xb, w1_ref[:, c * FF_CHUNK:(c + 1) * FF_CHUNK]), 0.0)
            term = _dot((h * h).astype(BF16), w2_ref[c * FF_CHUNK:(c + 1) * FF_CHUNK, :])
            acc = term if acc is None else acc + term
        ple = jax.nn.sigmoid(_dot(xb, wpg_ref[...])) * _dot(p_ref[0, rows, :].astype(BF16), wp_ref[...])
        y = DEEPNORM_ALPHA * x + acc + ple
        o_ref[rows, :] = _layer_norm(y, g_ref[...], b_ref[...])


def _ffn(x2, p3, layer, w1, w2, wpg, wp, g, bvec):
    t = x2.shape[0]
    tm = min(TM_MLP, t)

    def rows(width):
        return pl.BlockSpec((tm, width), lambda i: (i, 0))

    return pl.pallas_call(
        _ffn_kernel,
        grid=(t // tm,),
        in_specs=[rows(D_MODEL), pl.BlockSpec((1, tm, PLE_DIM), lambda i: (layer, i, 0)),
                  _const_spec(w1.shape), _const_spec(w2.shape), _const_spec(wpg.shape),
                  _const_spec(wp.shape), _const_spec(g.shape), _const_spec(bvec.shape)],
        out_specs=rows(D_MODEL),
        out_shape=jax.ShapeDtypeStruct((t, D_MODEL), F32),
        compiler_params=pltpu.CompilerParams(
            dimension_semantics=("arbitrary",), vmem_limit_bytes=VMEM_LIMIT),
        name="ffn",
    )(x2, p3, w1, w2, wpg, wp, g, bvec)


def _pack_w_in(w):
    d = w.shape[0]
    offs = {}
    o = 0
    for name, width in (("fq", W_FOX), ("fk", W_FOX), ("fv", W_FOX), ("ff", HEADS_FOX),
                        ("sq", W_SB), ("sk", W_SB), ("sv", W_SB),
                        ("dq", W_DSA), ("dk", HEAD_DIM), ("dv", HEAD_DIM),
                        ("iq", W_IDX), ("ik", HEAD_DIM), ("iw", IDX_HEADS),
                        ("g", N_BRANCH * D_MODEL)):
        offs[name] = w[:, o:o + width]
        o += width

    def z(n):
        return jnp.zeros((d, n), w.dtype)

    packed = jnp.concatenate(
        [offs["ff"], z(IW_LANE - HEADS_FOX), offs["iw"], z(LANE - IW_LANE - IDX_HEADS),
         offs["dk"], offs["dk"], offs["ik"], offs["ik"], offs["fk"], offs["sk"], z(W_SB_PAD - W_SB)], axis=1)
    assert packed.shape[1] == C_PACK
    packed_t = jnp.concatenate([offs["fq"], offs["sq"], offs["dq"], offs["iq"],
                                offs["fv"], offs["sv"], offs["dv"]], axis=1).T
    assert packed_t.shape[0] == R_PACK
    return packed.astype(BF16), packed_t.astype(BF16), offs["g"].astype(BF16)


def _rope_tables(s):
    inv = ROPE_THETA ** (-jnp.arange(HALF, dtype=F32) / HALF)
    ang = jnp.arange(s, dtype=F32)[:, None] * inv[None, :]
    cos, sin = jnp.cos(ang), jnp.sin(ang)
    row_major = jnp.concatenate([cos, cos, cos, cos, -sin, sin, -sin, sin], axis=1)
    transposed = jnp.concatenate([cos, cos, -sin, sin], axis=1).T
    return row_major, transposed


def _placement():
    rows_q, cols_q, rows_k, cols_k = [], [], [], []
    for h in range(HEADS_FOX):
        for t in range(N_CTERM):
            rows_q.append(t * LANE + h), cols_q.append(CSTRIDE * h + t)
            rows_k.append(t * LANE + h), cols_k.append(LANE + CSTRIDE * h + N_CTERM + t)
    place = jnp.zeros((N_CTERM * LANE, 2 * LANE), F32)
    place = place.at[jnp.array(rows_q), jnp.array(cols_q)].set(1.0)
    place = place.at[jnp.array(rows_k), jnp.array(cols_k)].set(-1.0)
    lane = jnp.arange(2 * LANE)
    head_lane = (lane % LANE) % CSTRIDE
    in_heads = (lane % LANE) < CSTRIDE * HEADS_FOX
    ones_q = (lane < LANE) & in_heads & (head_lane >= N_CTERM) & (head_lane < 2 * N_CTERM)
    ones_k = (lane >= LANE) & in_heads & (head_lane < N_CTERM)
    ones_row = jnp.where(ones_q | ones_k, 1.0, 0.0).astype(F32)[None, :]
    return place.astype(BF16), ones_row


def kernel(x, p, w_in, b_forget, w_up_fox, w_up_sb, w_up_dsa, w_out, ln1_g, ln1_b,
           w_ff_in, w_ff_out, w_ple, w_ple_gate, ln2_g, ln2_b):
    b, s, d = x.shape
    depth = w_in.shape[0]
    rope_tab, rope_tab_t = _rope_tables(s)
    place, ones_row = _placement()
    w_in_rows = w_in.reshape(depth * d, w_in.shape[2])
    for i in range(depth):
        w_pack, wt_pack, w_gate = _pack_w_in(w_in_rows[i * d:(i + 1) * d])
        bias_row = jnp.zeros((1, LANE), F32).at[0, :HEADS_FOX].set(b_forget[i].astype(F32))
        (fqt, fqct, fk, fkc, fvt, sqt, sk, svt, dqt, iqt, dk2, ik2, dvt, iwt) = _proj(
            x, w_pack, wt_pack, bias_row, rope_tab, rope_tab_t, place, ones_row)
        o_fox = _fox(fqt, fqct, fk, fkc, fvt)
        o_sb = _sb(sqt, sk, svt)
        o_dsa = _dsa(dqt, iqt, iwt, dk2, ik2, dvt)
        x1 = _merge(x.reshape(b * s, d), o_fox.reshape(b * s, W_FOX), o_sb.reshape(b * s, W_SB),
                    o_dsa.reshape(b * s, W_DSA), w_gate,
                    w_up_fox[i].astype(BF16), w_up_sb[i].astype(BF16), w_up_dsa[i].astype(BF16),
                    w_out[i].astype(BF16), ln1_g[i][None, :], ln1_b[i][None, :])
        x2 = _ffn(x1, p.reshape(depth, b * s, PLE_DIM), i, w_ff_in[i].astype(BF16), w_ff_out[i].astype(BF16),
                  w_ple_gate[i].astype(BF16), w_ple[i].astype(BF16), ln2_g[i][None, :], ln2_b[i][None, :])
        x = x2.reshape(b, s, d)
    return x
```

```python
import functools

import jax
import jax.numpy as jnp
from jax import lax
from jax.experimental import pallas as pl
from jax.experimental.pallas import tpu as pltpu

D_MODEL = 1024
HEAD_DIM = 64
HALF = HEAD_DIM // 2
HEADS_FOX = 6
HEADS_SB = 5
HEADS_DSA = 5
IDX_HEADS = 8
CHUNK = 64
DSA_TOPK_MAX = 256
D_FF = 4 * D_MODEL
PLE_DIM = 256
ROPE_THETA = 10000.0
LN_EPS = 1e-5
N_BRANCH = 3
NEG = -1e30
DEPTH = 2
DEEPNORM_ALPHA = (2 * DEPTH) ** 0.25
W_FOX = HEADS_FOX * HEAD_DIM
W_SB = HEADS_SB * HEAD_DIM
W_DSA = HEADS_DSA * HEAD_DIM
W_IDX = IDX_HEADS * HEAD_DIM
SCALE = HEAD_DIM ** -0.5

LANE = 128
SUBLANE = 8
PACKED_ROWS = 16
W_SB_PAD = 384
IW_LANE = 8
N_CTERM = 3
VMEM_LIMIT = 56 * 1024 * 1024

TM_PROJ = 512
TKC = 256
TQ_DSA = 256
TK_DSA = TKC
TM_MLP = 1024
MLP_SPLIT = 2

BF16 = jnp.bfloat16
F32 = jnp.float32


def _log_sigmoid(x):
    return jnp.minimum(x, 0.0) - jnp.log(1.0 + jnp.exp(-jnp.abs(x)))


def _dot(a, b):
    return jnp.dot(a, b, preferred_element_type=F32)


def _dot_nt(a, b):
    return lax.dot_general(a, b, (((1,), (1,)), ((), ())), preferred_element_type=F32)


def _split3(v):
    hi = v.astype(BF16)
    r = v - hi.astype(F32)
    mid = r.astype(BF16)
    lo = (r - mid.astype(F32)).astype(BF16)
    return hi, mid, lo


def _dispatch(index, values, body):
    if len(values) == 1:
        body(values[0])
        return
    mid = len(values) // 2
    lax.cond(index < values[mid], lambda: _dispatch(index, values[:mid], body),
             lambda: _dispatch(index, values[mid:], body))


def _pair_rows(qt_h, h):
    z = jnp.zeros_like(qt_h)
    return jnp.concatenate([z, qt_h] if h % 2 else [qt_h, z], axis=0)


OFF_SM = 0
OFF_DK2 = 128
OFF_IK2 = 256
OFF_FK = 384
OFF_SK = 768
C_PACK = 1152
OFF_TFQ, OFF_TSQ, OFF_TDQ, OFF_TIQ = 0, 384, 704, 1024
OFF_TFV, OFF_TSV, OFF_TDV = 1536, 1920, 2240
R_PACK = 2304
CSTRIDE = PACKED_ROWS


def _swap_halves(h):
    lane = lax.broadcasted_iota(jnp.int32, h.shape, 1)
    first = (lane % HEAD_DIM) < HALF
    return jnp.where(first, pltpu.roll(h, LANE - HALF, 1), pltpu.roll(h, HALF, 1))


def _rope_rows(h, cos, sin):
    return h * cos + _swap_halves(h) * sin


def _rope_t(ht, cos_t, sin_t):
    parts = []
    for h in range(ht.shape[0] // HEAD_DIM):
        x = ht[h * HEAD_DIM:(h + 1) * HEAD_DIM]
        swapped = jnp.concatenate([x[HALF:], x[:HALF]], axis=0)
        parts.append(x * cos_t + swapped * sin_t)
    return jnp.concatenate(parts, axis=0)


def _proj_kernel(x_ref, w_ref, wt_ref, bias_ref, tab_ref, tabt_ref, place_ref, ones_ref,
                 fqt_ref, fqct_ref, fk_ref, fkc_ref, fvt_ref, sqt_ref, sk_ref, svt_ref,
                 dqt_ref, iqt_ref, dk2_ref, ik2_ref, dvt_ref, iwt_ref, carry_ref):
    tm = x_ref.shape[1]

    @pl.when(pl.program_id(1) == 0)
    def _():
        carry_ref[...] = jnp.zeros_like(carry_ref)

    xb = x_ref[0].astype(BF16)

    def seg(off, width):
        return _dot(xb, w_ref[:, off:off + width])

    def seg_t(off, nrows):
        return _dot_nt(wt_ref[off:off + nrows, :], xb)

    def put_chunks(ref, val, chunk):
        for r in range(tm // chunk):
            ref[0, r] = val[:, r * chunk:(r + 1) * chunk].astype(ref.dtype)

    narrow = seg(OFF_SM, 3 * LANE)
    small = narrow[:, 0:LANE]
    logf = _log_sigmoid(small + bias_ref[...])

    row = lax.broadcasted_iota(jnp.int32, (tm, tm), 0)
    col = lax.broadcasted_iota(jnp.int32, (tm, tm), 1)
    tri = jnp.where(col <= row, 1.0, 0.0).astype(BF16)
    c = _dot(tri, jnp.concatenate(_split3(logf), axis=1))
    c = c[:, 0:LANE] + c[:, LANE:2 * LANE] + c[:, 2 * LANE:3 * LANE] + carry_ref[...]
    carry_ref[...] = c[tm - 1:tm, :]
    placed = _dot(jnp.concatenate(_split3(c), axis=1), place_ref[...]) + ones_ref[...]
    put_chunks(fqct_ref, placed[:, 0:LANE].T, TKC)
    fkc_ref[0] = placed[:, LANE:2 * LANE].astype(BF16)
    put_chunks(iwt_ref, small.T[IW_LANE:IW_LANE + IDX_HEADS], TQ_DSA)

    keys = seg(OFF_FK, W_FOX + W_SB_PAD)
    fk_ref[0] = keys[:, 0:W_FOX].astype(BF16)
    sk_ref[0] = keys[:, W_FOX:W_FOX + W_SB].astype(BF16)
    cos, sin = tab_ref[:, 0:LANE], tab_ref[:, LANE:2 * LANE]
    dk2_ref[0] = _rope_rows(narrow[:, OFF_DK2:OFF_DK2 + LANE], cos, sin).astype(BF16)
    ik2_ref[0] = _rope_rows(narrow[:, OFF_IK2:OFF_IK2 + LANE], cos, sin).astype(BF16)

    cos_t, sin_t = tabt_ref[0:HEAD_DIM, :], tabt_ref[HEAD_DIM:2 * HEAD_DIM, :]
    put_chunks(fqt_ref, seg_t(OFF_TFQ, W_FOX) * SCALE, TKC)
    put_chunks(sqt_ref, seg_t(OFF_TSQ, W_SB) * SCALE, TKC)
    put_chunks(dqt_ref, _rope_t(seg_t(OFF_TDQ, W_DSA), cos_t, sin_t) * SCALE, TQ_DSA)
    put_chunks(iqt_ref, _rope_t(seg_t(OFF_TIQ, W_IDX), cos_t, sin_t), TQ_DSA)
    put_chunks(fvt_ref, seg_t(OFF_TFV, W_FOX), TKC)
    put_chunks(svt_ref, seg_t(OFF_TSV, W_SB), TKC)
    put_chunks(dvt_ref, seg_t(OFF_TDV, HEAD_DIM), TKC)


def _proj(x, w_pack, wt_pack, bias_row, rope_tab, rope_tab_t, place, ones_row):
    b, s, d = x.shape
    tm = min(TM_PROJ, s)
    grid = (b, s // tm)

    def rows(width, dtype=BF16):
        return (jax.ShapeDtypeStruct((b, s, width), dtype),
                pl.BlockSpec((1, tm, width), lambda bi, si: (bi, si, 0)))

    def chunks(nrows, chunk, dtype=BF16):
        return (jax.ShapeDtypeStruct((b, s // chunk, nrows, chunk), dtype),
                pl.BlockSpec((1, tm // chunk, nrows, chunk), lambda bi, si: (bi, si, 0, 0)))

    def const(shape):
        return pl.BlockSpec(shape, lambda bi, si: (0,) * len(shape))

    outs = [chunks(W_FOX, TKC), chunks(LANE, TKC), rows(W_FOX), rows(LANE), chunks(W_FOX, TKC),
            chunks(W_SB, TKC), rows(W_SB), chunks(W_SB, TKC),
            chunks(W_DSA, TQ_DSA), chunks(W_IDX, TQ_DSA), rows(LANE), rows(LANE), chunks(HEAD_DIM, TKC),
            chunks(IDX_HEADS, TQ_DSA, F32)]
    return pl.pallas_call(
        _proj_kernel,
        grid=grid,
        in_specs=[pl.BlockSpec((1, tm, d), lambda bi, si: (bi, si, 0)),
                  const((d, C_PACK)), const((R_PACK, d)), const((1, LANE)),
                  pl.BlockSpec((tm, 2 * LANE), lambda bi, si: (si, 0)),
                  pl.BlockSpec((2 * HEAD_DIM, tm), lambda bi, si: (0, si)),
                  const(place.shape), const(ones_row.shape)],
        out_specs=[o[1] for o in outs],
        out_shape=[o[0] for o in outs],
        scratch_shapes=[pltpu.VMEM((1, LANE), F32)],
        compiler_params=pltpu.CompilerParams(
            dimension_semantics=("arbitrary", "arbitrary"), vmem_limit_bytes=VMEM_LIMIT),
        name="proj",
    )(x, w_pack, wt_pack, bias_row, rope_tab, rope_tab_t, place, ones_row)


def _fox_kernel(qt_ref, qct_ref, k_ref, kc_ref, vt_ref, o_ref, qcat_ref, m_ref, l_ref, acc_ref):
    tq = tk = TKC
    i = pl.program_id(1)
    krow = lax.broadcasted_iota(jnp.int32, (tk, tq), 0)
    qcol = lax.broadcasted_iota(jnp.int32, (tk, tq), 1)
    causal = krow <= qcol

    for h in range(HEADS_FOX):
        gate_rows = jnp.concatenate(
            ([jnp.zeros((CSTRIDE * h, tq), BF16)] if h else [])
            + [qct_ref[0, 0, CSTRIDE * h:CSTRIDE * (h + 1), :],
               jnp.zeros((LANE - CSTRIDE * (h + 1), tq), BF16)], axis=0)
        qcat_ref[h] = jnp.concatenate(
            [_pair_rows(qt_ref[0, 0, h * HEAD_DIM:(h + 1) * HEAD_DIM, :], h), gate_rows], axis=0)
    m_ref[...] = jnp.full(m_ref.shape, NEG, F32)
    l_ref[...] = jnp.zeros(l_ref.shape, F32)
    acc_ref[...] = jnp.zeros(acc_ref.shape, F32)

    def qk(js):
        kcs = [kc_ref[0, j * tk:(j + 1) * tk, :] for j in js]
        scores = []
        for g in range(HEADS_FOX // 2):
            kcats = [jnp.concatenate([k_ref[0, j * tk:(j + 1) * tk, g * LANE:(g + 1) * LANE], kc], axis=1)
                     for j, kc in zip(js, kcs)]
            for h in (2 * g, 2 * g + 1):
                scores.append([_dot(kcat, qcat_ref[h]) for kcat in kcats])
        return scores

    def softmax_pv(js, scores, masked):
        vts = [vt_ref[0, j] for j in js]
        for h in range(HEADS_FOX):
            ss = [jnp.where(causal, s, NEG) if masked else s for s in scores[h]]
            m_old = m_ref[h]
            m_new = m_old
            for s in ss:
                m_new = jnp.maximum(m_new, jnp.max(s, axis=0, keepdims=True))
            alpha = jnp.exp(m_old - m_new)
            ps = [jnp.exp(s - m_new) for s in ss]
            m_ref[h] = m_new
            l_ref[h] = alpha * l_ref[h] + sum(jnp.sum(p, axis=0, keepdims=True) for p in ps)
            acc_ref[h] = alpha * acc_ref[h] + sum(
                _dot(vt[h * HEAD_DIM:(h + 1) * HEAD_DIM, :], p.astype(BF16)) for vt, p in zip(vts, ps))

    def attend(i_static):
        groups = [((i_static,), True)] + [((j, j + 1), False) for j in range(0, i_static - 1, 2)]
        if i_static % 2:
            groups.append(((i_static - 1,), False))
        ahead = qk(groups[0][0])
        for n, (js, masked) in enumerate(groups):
            scores = ahead
            if n + 1 < len(groups):
                ahead = qk(groups[n + 1][0])
            softmax_pv(js, scores, masked)

    _dispatch(i, list(range(k_ref.shape[1] // tk)), attend)

    o_ref[0] = jnp.concatenate([(acc_ref[h] / l_ref[h]).T for h in range(HEADS_FOX)],
                               axis=1).astype(o_ref.dtype)


def _fox(fqt, fqct, fk, fkc, fvt):
    b, s, w = fk.shape
    tq = TKC
    return pl.pallas_call(
        _fox_kernel,
        grid=(b, s // tq),
        in_specs=[pl.BlockSpec((1, 1, w, tq), lambda bi, qi: (bi, qi, 0, 0)),
                  pl.BlockSpec((1, 1, LANE, tq), lambda bi, qi: (bi, qi, 0, 0)),
                  pl.BlockSpec((1, s, w), lambda bi, qi: (bi, 0, 0)),
                  pl.BlockSpec((1, s, LANE), lambda bi, qi: (bi, 0, 0)),
                  pl.BlockSpec((1, s // TKC, w, TKC), lambda bi, qi: (bi, 0, 0, 0))],
        out_specs=pl.BlockSpec((1, tq, w), lambda bi, qi: (bi, qi, 0)),
        out_shape=jax.ShapeDtypeStruct((b, s, w), BF16),
        scratch_shapes=[pltpu.VMEM((HEADS_FOX, 2 * LANE, tq), BF16),
                        pltpu.VMEM((HEADS_FOX, 1, tq), F32),
                        pltpu.VMEM((HEADS_FOX, 1, tq), F32),
                        pltpu.VMEM((HEADS_FOX, HEAD_DIM, tq), F32)],
        compiler_params=pltpu.CompilerParams(
            dimension_semantics=("arbitrary", "arbitrary"), vmem_limit_bytes=VMEM_LIMIT),
        name="fox",
    )(fqt, fqct, fk, fkc, fvt)


SB_DEAD_LOG = -110.0

def _sb_kernel(qt_ref, k_ref, vt_ref, o_ref, qm_ref, later_w_ref, run_ref, acc_ref):
    tq = tk = TKC
    i = pl.program_id(1)
    krow = lax.broadcasted_iota(jnp.int32, (tk, tq), 0)
    qcol = lax.broadcasted_iota(jnp.int32, (tk, tq), 1)
    strict = krow < qcol
    r = lax.broadcasted_iota(jnp.int32, (tk, 2 * tk), 0)
    c = lax.broadcasted_iota(jnp.int32, (tk, 2 * tk), 1) % tk
    later_w_ref[...] = jnp.where(c > r, -1.0, 0.0).astype(BF16)
    for h in range(HEADS_SB):
        qm_ref[h] = _pair_rows(qt_ref[0, 0, h * HEAD_DIM:(h + 1) * HEAD_DIM, :], h)
    run_ref[...] = jnp.zeros(run_ref.shape, F32)
    acc_ref[...] = jnp.zeros(acc_ref.shape, F32)

    def blocks(js, first_is_causal):
        starts = [pl.multiple_of(j * tk, tk) for j in js]
        zs, sps, laters = [], [], []
        for ks in starts:
            per_head = []
            for h in range(HEADS_SB):
                g = h // 2
                width = min(LANE, W_SB - g * LANE)
                kp = k_ref[0, pl.ds(ks, tk), g * LANE:g * LANE + width]
                per_head.append(_dot(kp, qm_ref[h][:width, :]))
            zs.append(per_head)
        for n in range(len(js)):
            per_head = []
            for z in zs[n]:
                sp = jnp.maximum(z, 0.0) + jnp.log(1.0 + jnp.exp(-jnp.abs(z)))
                per_head.append(jnp.where(strict, sp, 0.0) if (first_is_causal and n == 0) else sp)
            sps.append(per_head)
        for n in range(len(js)):
            per_head = []
            for sp in sps[n]:
                sp_hi = sp.astype(BF16)
                sp_lo = (sp - sp_hi.astype(F32)).astype(BF16)
                per_head.append(_dot(later_w_ref[...], jnp.concatenate([sp_hi, sp_lo], axis=0)))
            laters.append(per_head)
        vts = [vt_ref[0, j] for j in js]
        for h in range(HEADS_SB):
            run, acc = run_ref[h], acc_ref[h]
            for n in range(len(js)):
                a = jnp.exp(zs[n][h] - sps[n][h] + laters[n][h] + run)
                if first_is_causal and n == 0:
                    a = jnp.where(strict, a, 0.0)
                acc = acc + _dot(vts[n][h * HEAD_DIM:(h + 1) * HEAD_DIM, :], a.astype(BF16))
                run = run + laters[n][h][0:1, :] - sps[n][h][0:1, :]
            run_ref[h], acc_ref[h] = run, acc

    def alive():
        return jnp.max(run_ref[...]) > SB_DEAD_LOG

    @pl.when(i == 0)
    def _():
        blocks((i,), True)

    @pl.when(i > 0)
    def _():
        blocks((i, i - 1), True)

    def body(carry):
        t, _ = carry
        blocks((i - 1 - t,), False)
        return t + 1, alive()

    lax.while_loop(lambda carry: (carry[0] < i) & carry[1], body, (jnp.int32(1), alive()))
    o_ref[0] = jnp.concatenate([acc_ref[h].T for h in range(HEADS_SB)], axis=1).astype(o_ref.dtype)


def _sb(sqt, sk, svt):
    b, s, w = sk.shape
    tq = TKC
    return pl.pallas_call(
        _sb_kernel,
        grid=(b, s // tq),
        in_specs=[pl.BlockSpec((1, 1, w, tq), lambda bi, qi: (bi, qi, 0, 0)),
                  pl.BlockSpec((1, s, w), lambda bi, qi: (bi, 0, 0)),
                  pl.BlockSpec((1, s // TKC, w, TKC), lambda bi, qi: (bi, 0, 0, 0))],
        out_specs=pl.BlockSpec((1, tq, w), lambda bi, qi: (bi, qi, 0)),
        out_shape=jax.ShapeDtypeStruct((b, s, w), BF16),
        scratch_shapes=[pltpu.VMEM((HEADS_SB, LANE, tq), BF16),
                        pltpu.VMEM((TKC, 2 * TKC), BF16),
                        pltpu.VMEM((HEADS_SB, 1, tq), F32),
                        pltpu.VMEM((HEADS_SB, HEAD_DIM, tq), F32)],
        compiler_params=pltpu.CompilerParams(
            dimension_semantics=("arbitrary", "arbitrary"), vmem_limit_bytes=VMEM_LIMIT),
        name="sb",
    )(sqt, sk, svt)


NEG_KEY = -0x7149F2CB
HALF_BITS = 16
I16_MIN, I16_MAX = -2 ** (HALF_BITS - 1), 2 ** (HALF_BITS - 1) - 1


def _sortable(score):
    bits = lax.bitcast_convert_type(score, jnp.int32)
    return jnp.where(bits < 0, bits ^ jnp.int32(0x7FFFFFFF), bits)


def _sum_sublane_groups(x):
    parts = [x[r * SUBLANE:(r + 1) * SUBLANE] for r in range(x.shape[0] // SUBLANE)]
    while len(parts) > 1:
        parts = [parts[k] + parts[k + 1] for k in range(0, len(parts) - 1, 2)] + (
            [parts[-1]] if len(parts) % 2 else [])
    return parts[0]


def _tree_sum(parts):
    parts = list(parts)
    while len(parts) > 1:
        parts = [parts[k] + parts[k + 1] for k in range(0, len(parts) - 1, 2)] + (
            [parts[-1]] if len(parts) % 2 else [])
    return parts[0]


def _count_ge16(ref, nkb, cand):
    tk, tq = ref.shape[1], ref.shape[2]
    cand16 = jnp.broadcast_to(cand, (PACKED_ROWS, tq)).astype(jnp.int16)
    one, zero = jnp.ones((PACKED_ROWS, tq), jnp.int16), jnp.zeros((PACKED_ROWS, tq), jnp.int16)
    parts = []
    for j in range(nkb):
        blk = ref[j]
        parts += [jnp.where(blk[r * PACKED_ROWS:(r + 1) * PACKED_ROWS] >= cand16, one, zero)
                  for r in range(tk // PACKED_ROWS)]
    return jnp.sum(_tree_sum(parts).astype(F32), axis=0, keepdims=True)


def _search16(ref, nkb, n_sel, cnt_floor):
    tq = ref.shape[2]

    def bit_step(t, carry):
        thr_u, cnt_thr = carry
        bit = lax.shift_left(jnp.int32(1), jnp.int32(HALF_BITS - 1) - t)
        cand_u = thr_u | bit
        cnt = _count_ge16(ref, nkb, cand_u + I16_MIN)
        keep = cnt >= n_sel
        return jnp.where(keep, cand_u, thr_u), jnp.where(keep, cnt, cnt_thr)

    thr_u, cnt_thr = lax.fori_loop(0, HALF_BITS, bit_step, (jnp.zeros((1, tq), jnp.int32), cnt_floor))
    return thr_u + I16_MIN, cnt_thr


def _dsa_kernel(dqt_ref, iqt_ref, iwt_ref, dk2_ref, ik2_ref, dvt_ref, o_ref,
                key_ref, hi_ref, lo_ref, bias_ref, iq8_ref, q5_ref, jmax_ref, *, n_sel, idx_bits):
    tq, tk = TQ_DSA, TK_DSA
    i = pl.program_id(1)
    kloc = lax.broadcasted_iota(jnp.int32, (tk, tq), 0)
    qpos = i * tq + lax.broadcasted_iota(jnp.int32, (tk, tq), 1)
    klimit = (qpos // CHUNK + 1) * CHUNK
    for h in range(IDX_HEADS):
        iq8_ref[:, h * tq:(h + 1) * tq] = _pair_rows(iqt_ref[0, 0, h * HEAD_DIM:(h + 1) * HEAD_DIM, :], h)
    for h in range(HEADS_DSA):
        q5_ref[:, h * tq:(h + 1) * tq] = _pair_rows(dqt_ref[0, 0, h * HEAD_DIM:(h + 1) * HEAD_DIM, :], h)
    iwt = iwt_ref[0, 0]

    def select(nkb):
        for j in range(nkb):
            a = _dot(ik2_ref[0, j * tk:(j + 1) * tk, :], iq8_ref[...])
            sc = jnp.zeros((tk, tq), F32)
            for h in range(IDX_HEADS):
                sc = sc + iwt[h:h + 1, :] * jnp.maximum(a[:, h * tq:(h + 1) * tq], 0.0)
            sc = sc + 0.0
            sc = jnp.where(kloc + j * tk < klimit, sc, NEG)
            key = _sortable(sc)
            key_ref[j] = key
            hi_ref[j] = lax.shift_right_arithmetic(key, jnp.int32(HALF_BITS)).astype(jnp.int16)
            lo_ref[j] = ((key & jnp.int32(2 ** HALF_BITS - 1)) + I16_MIN).astype(jnp.int16)

        thr_hi, cnt_hi = _search16(hi_ref, nkb, n_sel, jnp.full((1, tq), float(nkb * tk), F32))
        thr_hi16 = jnp.broadcast_to(thr_hi, (PACKED_ROWS, tq)).astype(jnp.int16)
        for j in range(nkb):
            for r in range(tk // PACKED_ROWS):
                rows = slice(r * PACKED_ROWS, (r + 1) * PACKED_ROWS)
                hi = hi_ref[j, rows, :]
                pinned = jnp.where(hi > thr_hi16, jnp.int16(I16_MAX), jnp.int16(I16_MIN))
                lo_ref[j, rows, :] = jnp.where(hi == thr_hi16, lo_ref[j, rows, :], pinned)
        thr_lo, cnt_ge = _search16(lo_ref, nkb, n_sel, cnt_hi)
        thr = thr_hi * 2 ** HALF_BITS + (thr_lo - I16_MIN)

        def count(pred):
            parts = [_sum_sublane_groups(jnp.where(pred(key_ref[j]), 1.0, 0.0)) for j in range(nkb)]
            return jnp.sum(_tree_sum(parts), axis=0, keepdims=True)

        jmax_ref[...] = jnp.full((1, tq), 2 ** idx_bits - 1, jnp.int32)
        tied = (cnt_ge > n_sel) & (thr > NEG_KEY)

        @pl.when(jnp.max(jnp.where(tied, 1.0, 0.0)) > 0.0)
        def _():
            need = n_sel - count(lambda k: k > thr)
            for j in range(nkb):
                lo_ref[j] = jnp.where(key_ref[j] == thr, kloc + j * tk, I16_MAX).astype(jnp.int16)
            stored = float(nkb * tk)

            def idx_bit(t, jmax):
                bit = lax.shift_left(jnp.int32(1), jnp.int32(idx_bits - 1) - t)
                cand = jmax | bit
                cnt = stored - _count_ge16(lo_ref, nkb, cand)
                return jnp.where(cnt < need, cand, jmax)

            jmax_ref[...] = lax.fori_loop(0, idx_bits, idx_bit, jnp.zeros((1, tq), jnp.int32))

        jmax = jmax_ref[...]
        for j in range(nkb):
            k = key_ref[j]
            pos = kloc + j * tk
            sel = ((k > thr) | ((k == thr) & (pos <= jmax))) & (pos < klimit)
            bias_ref[j] = jnp.where(sel, 0.0, NEG)

    def attend(nkb):
        def qk(j):
            return _dot(dk2_ref[0, j * tk:(j + 1) * tk, :], q5_ref[...])

        m = jnp.full((1, HEADS_DSA * tq), NEG, F32)
        l = jnp.zeros((1, HEADS_DSA * tq), F32)
        acc = jnp.zeros((HEAD_DIM, HEADS_DSA * tq), F32)
        ahead = qk(0)
        for j in range(nkb):
            s = ahead + jnp.concatenate([bias_ref[j]] * HEADS_DSA, axis=1)
            if j + 1 < nkb:
                ahead = qk(j + 1)
            m_new = jnp.maximum(m, jnp.max(s, axis=0, keepdims=True))
            alpha = jnp.exp(m - m_new)
            p = jnp.exp(s - m_new)
            m = m_new
            l = alpha * l + jnp.sum(p, axis=0, keepdims=True)
            acc = alpha * acc + _dot(dvt_ref[0, j], p.astype(BF16))
        out = acc / l
        o_ref[0] = jnp.concatenate([out[:, h * tq:(h + 1) * tq].T for h in range(HEADS_DSA)],
                                   axis=1).astype(o_ref.dtype)

    nkb = (i * tq + tq - 1) // tk + 1
    nkb_even = lax.shift_left(lax.shift_right_logical(nkb + 1, 1), 1)
    def select_and_attend(nkb_static):
        if (nkb_static - 2) * tk < n_sel:
            few = (i * tq + tq - 1) // CHUNK * CHUNK + CHUNK <= n_sel

            def admit_all():
                for j in range(nkb_static):
                    bias_ref[j] = jnp.where(kloc + j * tk < klimit, 0.0, NEG)

            lax.cond(few, admit_all, lambda: select(nkb_static))
        else:
            select(nkb_static)
        attend(nkb_static)

    _dispatch(nkb_even, list(range(2, dk2_ref.shape[1] // tk + 1, 2)), select_and_attend)


def _dsa(dqt, iqt, iwt, dk2, ik2, dvt):
    b, s, _ = dk2.shape
    tq, tk = TQ_DSA, TK_DSA
    n_sel = min(DSA_TOPK_MAX, s // 4)
    assert n_sel <= tk and s % (2 * tk) == 0 and tk % tq == 0 and s < I16_MAX
    idx_bits = max(1, (s - 1).bit_length())
    return pl.pallas_call(
        functools.partial(_dsa_kernel, n_sel=n_sel, idx_bits=idx_bits),
        grid=(b, s // tq),
        in_specs=[pl.BlockSpec((1, 1, W_DSA, tq), lambda bi, qi: (bi, qi, 0, 0)),
                  pl.BlockSpec((1, 1, W_IDX, tq), lambda bi, qi: (bi, qi, 0, 0)),
                  pl.BlockSpec((1, 1, IDX_HEADS, tq), lambda bi, qi: (bi, qi, 0, 0)),
                  pl.BlockSpec((1, s, LANE), lambda bi, qi: (bi, 0, 0)),
                  pl.BlockSpec((1, s, LANE), lambda bi, qi: (bi, 0, 0)),
                  pl.BlockSpec((1, s // tk, HEAD_DIM, tk), lambda bi, qi: (bi, 0, 0, 0))],
        out_specs=pl.BlockSpec((1, tq, W_DSA), lambda bi, qi: (bi, qi, 0)),
        out_shape=jax.ShapeDtypeStruct((b, s, W_DSA), BF16),
        scratch_shapes=[pltpu.VMEM((s // tk, tk, tq), jnp.int32),
                        pltpu.VMEM((s // tk, tk, tq), jnp.int16),
                        pltpu.VMEM((s // tk, tk, tq), jnp.int16),
                        pltpu.VMEM((s // tk, tk, tq), F32),
                        pltpu.VMEM((LANE, IDX_HEADS * tq), BF16),
                        pltpu.VMEM((LANE, HEADS_DSA * tq), BF16),
                        pltpu.VMEM((1, tq), jnp.int32)],
        compiler_params=pltpu.CompilerParams(
            dimension_semantics=("arbitrary", "arbitrary"), vmem_limit_bytes=VMEM_LIMIT),
        name="dsa",
    )(dqt, iqt, iwt, dk2, ik2, dvt)


def _layer_norm(y, g, b):
    mu = jnp.mean(y, axis=1, keepdims=True)
    d = y - mu
    var = jnp.mean(d * d, axis=1, keepdims=True)
    return d * lax.rsqrt(var + LN_EPS) * g + b


def _merge_kernel(x_ref, of_ref, os_ref, od_ref, wg_ref, wuf_ref, wus_ref, wud_ref, wo_ref,
                  g_ref, b_ref, o_ref):
    half = x_ref.shape[0] // MLP_SPLIT
    for r in range(MLP_SPLIT):
        rows = slice(r * half, (r + 1) * half)
        x = x_ref[rows, :]
        xb = x.astype(BF16)
        merged = None
        for n, (br_ref, wu_ref) in enumerate(((of_ref, wuf_ref), (os_ref, wus_ref), (od_ref, wud_ref))):
            gate = jax.nn.sigmoid(_dot(xb, wg_ref[:, n * D_MODEL:(n + 1) * D_MODEL]))
            term = gate * _dot(br_ref[rows, :], wu_ref[...])
            merged = term if merged is None else merged + term
        y = DEEPNORM_ALPHA * x + _dot(merged.astype(BF16), wo_ref[...])
        o_ref[rows, :] = _layer_norm(y, g_ref[...], b_ref[...])


def _const_spec(shape):
    return pl.BlockSpec(shape, lambda i: (0,) * len(shape), pipeline_mode=pl.Buffered(1))


def _merge(x2, o_fox, o_sb, o_dsa, wg, wuf, wus, wud, wo, g, bvec):
    t = x2.shape[0]
    tm = min(TM_MLP, t)

    def rows(width):
        return pl.BlockSpec((tm, width), lambda i: (i, 0))

    return pl.pallas_call(
        _merge_kernel,
        grid=(t // tm,),
        in_specs=[rows(D_MODEL), rows(W_FOX), rows(W_SB), rows(W_DSA),
                  _const_spec(wg.shape), _const_spec(wuf.shape), _const_spec(wus.shape),
                  _const_spec(wud.shape), _const_spec(wo.shape),
                  _const_spec(g.shape), _const_spec(bvec.shape)],
        out_specs=rows(D_MODEL),
        out_shape=jax.ShapeDtypeStruct((t, D_MODEL), F32),
        compiler_params=pltpu.CompilerParams(
            dimension_semantics=("arbitrary",), vmem_limit_bytes=VMEM_LIMIT),
        name="merge",
    )(x2, o_fox, o_sb, o_dsa, wg, wuf, wus, wud, wo, g, bvec)


FF_CHUNK = 1024


def _ffn_kernel(x_ref, p_ref, w1_ref, w2_ref, wpg_ref, wp_ref, g_ref, b_ref, o_ref):
    half = x_ref.shape[0] // MLP_SPLIT
    for r in range(MLP_SPLIT):
        rows = slice(r * half, (r + 1) * half)
        x = x_ref[rows, :]
        xb = x.astype(BF16)
        acc = None
        for c in range(D_FF // FF_CHUNK):
            h = jnp.maximum(_dot(xb, w1_ref[:, c * FF_CHUNK:(c + 1) * FF_CHUNK]), 0.0)
            term = _dot((h * h).astype(BF16), w2_ref[c * FF_CHUNK:(c + 1) * FF_CHUNK, :])
            acc = term if acc is None else acc + term
        ple = jax.nn.sigmoid(_dot(xb, wpg_ref[...])) * _dot(p_ref[0, rows, :].astype(BF16), wp_ref[...])
        y = DEEPNORM_ALPHA * x + acc + ple
        o_ref[rows, :] = _layer_norm(y, g_ref[...], b_ref[...])


def _ffn(x2, p3, layer, w1, w2, wpg, wp, g, bvec):
    t = x2.shape[0]
    tm = min(TM_MLP, t)

    def rows(width):
        return pl.BlockSpec((tm, width), lambda i: (i, 0))

    return pl.pallas_call(
        _ffn_kernel,
        grid=(t // tm,),
        in_specs=[rows(D_MODEL), pl.BlockSpec((1, tm, PLE_DIM), lambda i: (layer, i, 0)),
                  _const_spec(w1.shape), _const_spec(w2.shape), _const_spec(wpg.shape),
                  _const_spec(wp.shape), _const_spec(g.shape), _const_spec(bvec.shape)],
        out_specs=rows(D_MODEL),
        out_shape=jax.ShapeDtypeStruct((t, D_MODEL), F32),
        compiler_params=pltpu.CompilerParams(
            dimension_semantics=("arbitrary",), vmem_limit_bytes=VMEM_LIMIT),
        name="ffn",
    )(x2, p3, w1, w2, wpg, wp, g, bvec)


def _pack_w_in(w):
    d = w.shape[0]
    offs = {}
    o = 0
    for name, width in (("fq", W_FOX), ("fk", W_FOX), ("fv", W_FOX), ("ff", HEADS_FOX),
                        ("sq", W_SB), ("sk", W_SB), ("sv", W_SB),
                        ("dq", W_DSA), ("dk", HEAD_DIM), ("dv", HEAD_DIM),
                        ("iq", W_IDX), ("ik", HEAD_DIM), ("iw", IDX_HEADS),
                        ("g", N_BRANCH * D_MODEL)):
        offs[name] = w[:, o:o + width]
        o += width

    def z(n):
        return jnp.zeros((d, n), w.dtype)

    packed = jnp.concatenate(
        [offs["ff"], z(IW_LANE - HEADS_FOX), offs["iw"], z(LANE - IW_LANE - IDX_HEADS),
         offs["dk"], offs["dk"], offs["ik"], offs["ik"], offs["fk"], offs["sk"], z(W_SB_PAD - W_SB)], axis=1)
    assert packed.shape[1] == C_PACK
    packed_t = jnp.concatenate([offs["fq"], offs["sq"], offs["dq"], offs["iq"],
                                offs["fv"], offs["sv"], offs["dv"]], axis=1).T
    assert packed_t.shape[0] == R_PACK
    return packed.astype(BF16), packed_t.astype(BF16), offs["g"].astype(BF16)


def _rope_tables(s):
    inv = ROPE_THETA ** (-jnp.arange(HALF, dtype=F32) / HALF)
    ang = jnp.arange(s, dtype=F32)[:, None] * inv[None, :]
    cos, sin = jnp.cos(ang), jnp.sin(ang)
    row_major = jnp.concatenate([cos, cos, cos, cos, -sin, sin, -sin, sin], axis=1)
    transposed = jnp.concatenate([cos, cos, -sin, sin], axis=1).T
    return row_major, transposed


def _placement():
    rows_q, cols_q, rows_k, cols_k = [], [], [], []
    for h in range(HEADS_FOX):
        for t in range(N_CTERM):
            rows_q.append(t * LANE + h), cols_q.append(CSTRIDE * h + t)
            rows_k.append(t * LANE + h), cols_k.append(LANE + CSTRIDE * h + N_CTERM + t)
    place = jnp.zeros((N_CTERM * LANE, 2 * LANE), F32)
    place = place.at[jnp.array(rows_q), jnp.array(cols_q)].set(1.0)
    place = place.at[jnp.array(rows_k), jnp.array(cols_k)].set(-1.0)
    lane = jnp.arange(2 * LANE)
    head_lane = (lane % LANE) % CSTRIDE
    in_heads = (lane % LANE) < CSTRIDE * HEADS_FOX
    ones_q = (lane < LANE) & in_heads & (head_lane >= N_CTERM) & (head_lane < 2 * N_CTERM)
    ones_k = (lane >= LANE) & in_heads & (head_lane < N_CTERM)
    ones_row = jnp.where(ones_q | ones_k, 1.0, 0.0).astype(F32)[None, :]
    return place.astype(BF16), ones_row


def kernel(x, p, w_in, b_forget, w_up_fox, w_up_sb, w_up_dsa, w_out, ln1_g, ln1_b,
           w_ff_in, w_ff_out, w_ple, w_ple_gate, ln2_g, ln2_b):
    b, s, d = x.shape
    depth = w_in.shape[0]
    rope_tab, rope_tab_t = _rope_tables(s)
    place, ones_row = _placement()
    w_in_rows = w_in.reshape(depth * d, w_in.shape[2])
    for i in range(depth):
        w_pack, wt_pack, w_gate = _pack_w_in(w_in_rows[i * d:(i + 1) * d])
        bias_row = jnp.zeros((1, LANE), F32).at[0, :HEADS_FOX].set(b_forget[i].astype(F32))
        (fqt, fqct, fk, fkc, fvt, sqt, sk, svt, dqt, iqt, dk2, ik2, dvt, iwt) = _proj(
            x, w_pack, wt_pack, bias_row, rope_tab, rope_tab_t, place, ones_row)
        o_fox = _fox(fqt, fqct, fk, fkc, fvt)
        o_sb = _sb(sqt, sk, svt)
        o_dsa = _dsa(dqt, iqt, iwt, dk2, ik2, dvt)
        x1 = _merge(x.reshape(b * s, d), o_fox.reshape(b * s, W_FOX), o_sb.reshape(b * s, W_SB),
                    o_dsa.reshape(b * s, W_DSA), w_gate,
                    w_up_fox[i].astype(BF16), w_up_sb[i].astype(BF16), w_up_dsa[i].astype(BF16),
                    w_out[i].astype(BF16), ln1_g[i][None, :], ln1_b[i][None, :])
        x2 = _ffn(x1, p.reshape(depth, b * s, PLE_DIM), i, w_ff_in[i].astype(BF16), w_ff_out[i].astype(BF16),
                  w_ple_gate[i].astype(BF16), w_ple[i].astype(BF16), ln2_g[i][None, :], ln2_b[i][None, :])
        x = x2.reshape(b, s, d)
    return x
```

```python
import functools

import jax
import jax.numpy as jnp
from jax import lax
from jax.experimental import pallas as pl
from jax.experimental.pallas import tpu as pltpu

D_MODEL = 1024
HEAD_DIM = 64
HALF = HEAD_DIM // 2
HEADS_FOX = 6
HEADS_SB = 5
HEADS_DSA = 5
IDX_HEADS = 8
CHUNK = 64
DSA_TOPK_MAX = 256
D_FF = 4 * D_MODEL
PLE_DIM = 256
ROPE_THETA = 10000.0
LN_EPS = 1e-5
N_BRANCH = 3
NEG = -1e30
DEPTH = 2
DEEPNORM_ALPHA = (2 * DEPTH) ** 0.25
W_FOX = HEADS_FOX * HEAD_DIM
W_SB = HEADS_SB * HEAD_DIM
W_DSA = HEADS_DSA * HEAD_DIM
W_IDX = IDX_HEADS * HEAD_DIM
SCALE = HEAD_DIM ** -0.5

LANE = 128
SUBLANE = 8
PACKED_ROWS = 16
W_SB_PAD = 384
IW_LANE = 8
N_CTERM = 3
VMEM_LIMIT = 56 * 1024 * 1024

TM_PROJ = 512
TKC = 256
TQ_DSA = 256
TK_DSA = TKC
TM_MLP = 1024
MLP_SPLIT = 2

BF16 = jnp.bfloat16
F32 = jnp.float32


def _log_sigmoid(x):
    return jnp.minimum(x, 0.0) - jnp.log(1.0 + jnp.exp(-jnp.abs(x)))


def _dot(a, b):
    return jnp.dot(a, b, preferred_element_type=F32)


def _dot_nt(a, b):
    return lax.dot_general(a, b, (((1,), (1,)), ((), ())), preferred_element_type=F32)


def _split3(v):
    hi = v.astype(BF16)
    r = v - hi.astype(F32)
    mid = r.astype(BF16)
    lo = (r - mid.astype(F32)).astype(BF16)
    return hi, mid, lo


def _dispatch(index, values, body):
    if len(values) == 1:
        body(values[0])
        return
    mid = len(values) // 2
    lax.cond(index < values[mid], lambda: _dispatch(index, values[:mid], body),
             lambda: _dispatch(index, values[mid:], body))


def _pair_rows(qt_h, h):
    z = jnp.zeros_like(qt_h)
    return jnp.concatenate([z, qt_h] if h % 2 else [qt_h, z], axis=0)


OFF_SM = 0
OFF_DK2 = 128
OFF_IK2 = 256
OFF_FK = 384
OFF_SK = 768
C_PACK = 1152
OFF_TFQ, OFF_TSQ, OFF_TDQ, OFF_TIQ = 0, 384, 704, 1024
OFF_TFV, OFF_TSV, OFF_TDV = 1536, 1920, 2240
R_PACK = 2304
CSTRIDE = PACKED_ROWS


def _swap_halves(h):
    lane = lax.broadcasted_iota(jnp.int32, h.shape, 1)
    first = (lane % HEAD_DIM) < HALF
    return jnp.where(first, pltpu.roll(h, LANE - HALF, 1), pltpu.roll(h, HALF, 1))


def _rope_rows(h, cos, sin):
    return h * cos + _swap_halves(h) * sin


def _rope_t(ht, cos_t, sin_t):
    parts = []
    for h in range(ht.shape[0] // HEAD_DIM):
        x = ht[h * HEAD_DIM:(h + 1) * HEAD_DIM]
        swapped = jnp.concatenate([x[HALF:], x[:HALF]], axis=0)
        parts.append(x * cos_t + swapped * sin_t)
    return jnp.concatenate(parts, axis=0)


def _proj_kernel(x_ref, w_ref, wt_ref, bias_ref, tab_ref, tabt_ref, place_ref, ones_ref,
                 fqt_ref, fqct_ref, fk_ref, fkc_ref, fvt_ref, sqt_ref, sk_ref, svt_ref,
                 dqt_ref, iqt_ref, dk2_ref, ik2_ref, dvt_ref, iwt_ref, carry_ref):
    tm = x_ref.shape[1]

    @pl.when(pl.program_id(1) == 0)
    def _():
        carry_ref[...] = jnp.zeros_like(carry_ref)

    xb = x_ref[0].astype(BF16)

    def seg(off, width):
        return _dot(xb, w_ref[:, off:off + width])

    def seg_t(off, nrows):
        return _dot_nt(wt_ref[off:off + nrows, :], xb)

    def put_chunks(ref, val, chunk):
        for r in range(tm // chunk):
            ref[0, r] = val[:, r * chunk:(r + 1) * chunk].astype(ref.dtype)

    narrow = seg(OFF_SM, 3 * LANE)
    small = narrow[:, 0:LANE]
    logf = _log_sigmoid(small + bias_ref[...])

    row = lax.broadcasted_iota(jnp.int32, (tm, tm), 0)
    col = lax.broadcasted_iota(jnp.int32, (tm, tm), 1)
    tri = jnp.where(col <= row, 1.0, 0.0).astype(BF16)
    c = _dot(tri, jnp.concatenate(_split3(logf), axis=1))
    c = c[:, 0:LANE] + c[:, LANE:2 * LANE] + c[:, 2 * LANE:3 * LANE] + carry_ref[...]
    carry_ref[...] = c[tm - 1:tm, :]
    placed = _dot(jnp.concatenate(_split3(c), axis=1), place_ref[...]) + ones_ref[...]
    put_chunks(fqct_ref, placed[:, 0:LANE].T, TKC)
    fkc_ref[0] = placed[:, LANE:2 * LANE].astype(BF16)
    put_chunks(iwt_ref, small.T[IW_LANE:IW_LANE + IDX_HEADS], TQ_DSA)

    keys = seg(OFF_FK, W_FOX + W_SB_PAD)
    fk_ref[0] = keys[:, 0:W_FOX].astype(BF16)
    sk_ref[0] = keys[:, W_FOX:W_FOX + W_SB].astype(BF16)
    cos, sin = tab_ref[:, 0:LANE], tab_ref[:, LANE:2 * LANE]
    dk2_ref[0] = _rope_rows(narrow[:, OFF_DK2:OFF_DK2 + LANE], cos, sin).astype(BF16)
    ik2_ref[0] = _rope_rows(narrow[:, OFF_IK2:OFF_IK2 + LANE], cos, sin).astype(BF16)

    cos_t, sin_t = tabt_ref[0:HEAD_DIM, :], tabt_ref[HEAD_DIM:2 * HEAD_DIM, :]
    put_chunks(fqt_ref, seg_t(OFF_TFQ, W_FOX) * SCALE, TKC)
    put_chunks(sqt_ref, seg_t(OFF_TSQ, W_SB) * SCALE, TKC)
    put_chunks(dqt_ref, _rope_t(seg_t(OFF_TDQ, W_DSA), cos_t, sin_t) * SCALE, TQ_DSA)
    put_chunks(iqt_ref, _rope_t(seg_t(OFF_TIQ, W_IDX), cos_t, sin_t), TQ_DSA)
    put_chunks(fvt_ref, seg_t(OFF_TFV, W_FOX), TKC)
    put_chunks(svt_ref, seg_t(OFF_TSV, W_SB), TKC)
    put_chunks(dvt_ref, seg_t(OFF_TDV, HEAD_DIM), TKC)


def _proj(x, w_pack, wt_pack, bias_row, rope_tab, rope_tab_t, place, ones_row):
    b, s, d = x.shape
    tm = min(TM_PROJ, s)
    grid = (b, s // tm)

    def rows(width, dtype=BF16):
        return (jax.ShapeDtypeStruct((b, s, width), dtype),
                pl.BlockSpec((1, tm, width), lambda bi, si: (bi, si, 0)))

    def chunks(nrows, chunk, dtype=BF16):
        return (jax.ShapeDtypeStruct((b, s // chunk, nrows, chunk), dtype),
                pl.BlockSpec((1, tm // chunk, nrows, chunk), lambda bi, si: (bi, si, 0, 0)))

    def const(shape):
        return pl.BlockSpec(shape, lambda bi, si: (0,) * len(shape))

    outs = [chunks(W_FOX, TKC), chunks(LANE, TKC), rows(W_FOX), rows(LANE), chunks(W_FOX, TKC),
            chunks(W_SB, TKC), rows(W_SB), chunks(W_SB, TKC),
            chunks(W_DSA, TQ_DSA), chunks(W_IDX, TQ_DSA), rows(LANE), rows(LANE), chunks(HEAD_DIM, TKC),
            chunks(IDX_HEADS, TQ_DSA, F32)]
    return pl.pallas_call(
        _proj_kernel,
        grid=grid,
        in_specs=[pl.BlockSpec((1, tm, d), lambda bi, si: (bi, si, 0)),
                  const((d, C_PACK)), const((R_PACK, d)), const((1, LANE)),
                  pl.BlockSpec((tm, 2 * LANE), lambda bi, si: (si, 0)),
                  pl.BlockSpec((2 * HEAD_DIM, tm), lambda bi, si: (0, si)),
                  const(place.shape), const(ones_row.shape)],
        out_specs=[o[1] for o in outs],
        out_shape=[o[0] for o in outs],
        scratch_shapes=[pltpu.VMEM((1, LANE), F32)],
        compiler_params=pltpu.CompilerParams(
            dimension_semantics=("arbitrary", "arbitrary"), vmem_limit_bytes=VMEM_LIMIT),
        name="proj",
    )(x, w_pack, wt_pack, bias_row, rope_tab, rope_tab_t, place, ones_row)


def _fox_kernel(qt_ref, qct_ref, k_ref, kc_ref, vt_ref, o_ref, qcat_ref, m_ref, l_ref, acc_ref):
    tq = tk = TKC
    i = pl.program_id(1)
    krow = lax.broadcasted_iota(jnp.int32, (tk, tq), 0)
    qcol = lax.broadcasted_iota(jnp.int32, (tk, tq), 1)
    causal = krow <= qcol

    for h in range(HEADS_FOX):
        gate_rows = jnp.concatenate(
            ([jnp.zeros((CSTRIDE * h, tq), BF16)] if h else [])
            + [qct_ref[0, 0, CSTRIDE * h:CSTRIDE * (h + 1), :],
               jnp.zeros((LANE - CSTRIDE * (h + 1), tq), BF16)], axis=0)
        qcat_ref[h] = jnp.concatenate(
            [_pair_rows(qt_ref[0, 0, h * HEAD_DIM:(h + 1) * HEAD_DIM, :], h), gate_rows], axis=0)
    m_ref[...] = jnp.full(m_ref.shape, NEG, F32)
    l_ref[...] = jnp.zeros(l_ref.shape, F32)
    acc_ref[...] = jnp.zeros(acc_ref.shape, F32)

    def qk(js):
        kcs = [kc_ref[0, j * tk:(j + 1) * tk, :] for j in js]
        scores = []
        for g in range(HEADS_FOX // 2):
            kcats = [jnp.concatenate([k_ref[0, j * tk:(j + 1) * tk, g * LANE:(g + 1) * LANE], kc], axis=1)
                     for j, kc in zip(js, kcs)]
            for h in (2 * g, 2 * g + 1):
                scores.append([_dot(kcat, qcat_ref[h]) for kcat in kcats])
        return scores

    def softmax_pv(js, scores, masked):
        vts = [vt_ref[0, j] for j in js]
        for h in range(HEADS_FOX):
            ss = [jnp.where(causal, s, NEG) if masked else s for s in scores[h]]
            m_old = m_ref[h]
            m_new = m_old
            for s in ss:
                m_new = jnp.maximum(m_new, jnp.max(s, axis=0, keepdims=True))
            alpha = jnp.exp(m_old - m_new)
            ps = [jnp.exp(s - m_new) for s in ss]
            m_ref[h] = m_new
            l_ref[h] = alpha * l_ref[h] + sum(jnp.sum(p, axis=0, keepdims=True) for p in ps)
            acc_ref[h] = alpha * acc_ref[h] + sum(
                _dot(vt[h * HEAD_DIM:(h + 1) * HEAD_DIM, :], p.astype(BF16)) for vt, p in zip(vts, ps))

    def attend(i_static):
        groups = [((i_static,), True)] + [((j, j + 1), False) for j in range(0, i_static - 1, 2)]
        if i_static % 2:
            groups.append(((i_static - 1,), False))
        ahead = qk(groups[0][0])
        for n, (js, masked) in enumerate(groups):
            scores = ahead
            if n + 1 < len(groups):
                ahead = qk(groups[n + 1][0])
            softmax_pv(js, scores, masked)

    _dispatch(i, list(range(k_ref.shape[1] // tk)), attend)

    o_ref[0] = jnp.concatenate([(acc_ref[h] / l_ref[h]).T for h in range(HEADS_FOX)],
                               axis=1).astype(o_ref.dtype)


def _fox(fqt, fqct, fk, fkc, fvt):
    b, s, w = fk.shape
    tq = TKC
    return pl.pallas_call(
        _fox_kernel,
        grid=(b, s // tq),
        in_specs=[pl.BlockSpec((1, 1, w, tq), lambda bi, qi: (bi, qi, 0, 0)),
                  pl.BlockSpec((1, 1, LANE, tq), lambda bi, qi: (bi, qi, 0, 0)),
                  pl.BlockSpec((1, s, w), lambda bi, qi: (bi, 0, 0)),
                  pl.BlockSpec((1, s, LANE), lambda bi, qi: (bi, 0, 0)),
                  pl.BlockSpec((1, s // TKC, w, TKC), lambda bi, qi: (bi, 0, 0, 0))],
        out_specs=pl.BlockSpec((1, tq, w), lambda bi, qi: (bi, qi, 0)),
        out_shape=jax.ShapeDtypeStruct((b, s, w), BF16),
        scratch_shapes=[pltpu.VMEM((HEADS_FOX, 2 * LANE, tq), BF16),
                        pltpu.VMEM((HEADS_FOX, 1, tq), F32),
                        pltpu.VMEM((HEADS_FOX, 1, tq), F32),
                        pltpu.VMEM((HEADS_FOX, HEAD_DIM, tq), F32)],
        compiler_params=pltpu.CompilerParams(
            dimension_semantics=("arbitrary", "arbitrary"), vmem_limit_bytes=VMEM_LIMIT),
        name="fox",
    )(fqt, fqct, fk, fkc, fvt)


SB_DEAD_LOG = -110.0

def _sb_kernel(qt_ref, k_ref, vt_ref, o_ref, qm_ref, later_w_ref, run_ref, acc_ref):
    tq = tk = TKC
    i = pl.program_id(1)
    krow = lax.broadcasted_iota(jnp.int32, (tk, tq), 0)
    qcol = lax.broadcasted_iota(jnp.int32, (tk, tq), 1)
    strict = krow < qcol
    r = lax.broadcasted_iota(jnp.int32, (tk, 2 * tk), 0)
    c = lax.broadcasted_iota(jnp.int32, (tk, 2 * tk), 1) % tk
    later_w_ref[...] = jnp.where(c > r, -1.0, 0.0).astype(BF16)
    for h in range(HEADS_SB):
        qm_ref[h] = _pair_rows(qt_ref[0, 0, h * HEAD_DIM:(h + 1) * HEAD_DIM, :], h)
    run_ref[...] = jnp.zeros(run_ref.shape, F32)
    acc_ref[...] = jnp.zeros(acc_ref.shape, F32)

    def blocks(js, first_is_causal):
        starts = [pl.multiple_of(j * tk, tk) for j in js]
        zs, sps, laters = [], [], []
        for ks in starts:
            per_head = []
            for h in range(HEADS_SB):
                g = h // 2
                width = min(LANE, W_SB - g * LANE)
                kp = k_ref[0, pl.ds(ks, tk), g * LANE:g * LANE + width]
                per_head.append(_dot(kp, qm_ref[h][:width, :]))
            zs.append(per_head)
        for n in range(len(js)):
            per_head = []
            for z in zs[n]:
                sp = jnp.maximum(z, 0.0) + jnp.log(1.0 + jnp.exp(-jnp.abs(z)))
                per_head.append(jnp.where(strict, sp, 0.0) if (first_is_causal and n == 0) else sp)
            sps.append(per_head)
        for n in range(len(js)):
            per_head = []
            for sp in sps[n]:
                sp_hi = sp.astype(BF16)
                sp_lo = (sp - sp_hi.astype(F32)).astype(BF16)
                per_head.append(_dot(later_w_ref[...], jnp.concatenate([sp_hi, sp_lo], axis=0)))
            laters.append(per_head)
        vts = [vt_ref[0, j] for j in js]
        for h in range(HEADS_SB):
            run, acc = run_ref[h], acc_ref[h]
            for n in range(len(js)):
                a = jnp.exp(zs[n][h] - sps[n][h] + laters[n][h] + run)
                if first_is_causal and n == 0:
                    a = jnp.where(strict, a, 0.0)
                acc = acc + _dot(vts[n][h * HEAD_DIM:(h + 1) * HEAD_DIM, :], a.astype(BF16))
                run = run + laters[n][h][0:1, :] - sps[n][h][0:1, :]
            run_ref[h], acc_ref[h] = run, acc

    def alive():
        return jnp.max(run_ref[...]) > SB_DEAD_LOG

    @pl.when(i == 0)
    def _():
        blocks((i,), True)

    @pl.when(i > 0)
    def _():
        blocks((i, i - 1), True)

    def body(carry):
        t, _ = carry
        blocks((i - 1 - t,), False)
        return t + 1, alive()

    lax.while_loop(lambda carry: (carry[0] < i) & carry[1], body, (jnp.int32(1), alive()))
    o_ref[0] = jnp.concatenate([acc_ref[h].T for h in range(HEADS_SB)], axis=1).astype(o_ref.dtype)


def _sb(sqt, sk, svt):
    b, s, w = sk.shape
    tq = TKC
    return pl.pallas_call(
        _sb_kernel,
        grid=(b, s // tq),
        in_specs=[pl.BlockSpec((1, 1, w, tq), lambda bi, qi: (bi, qi, 0, 0)),
                  pl.BlockSpec((1, s, w), lambda bi, qi: (bi, 0, 0)),
                  pl.BlockSpec((1, s // TKC, w, TKC), lambda bi, qi: (bi, 0, 0, 0))],
        out_specs=pl.BlockSpec((1, tq, w), lambda bi, qi: (bi, qi, 0)),
        out_shape=jax.ShapeDtypeStruct((b, s, w), BF16),
        scratch_shapes=[pltpu.VMEM((HEADS_SB, LANE, tq), BF16),
                        pltpu.VMEM((TKC, 2 * TKC), BF16),
                        pltpu.VMEM((HEADS_SB, 1, tq), F32),
                        pltpu.VMEM((HEADS_SB, HEAD_DIM, tq), F32)],
        compiler_params=pltpu.CompilerParams(
            dimension_semantics=("arbitrary", "arbitrary"), vmem_limit_bytes=VMEM_LIMIT),
        name="sb",
    )(sqt, sk, svt)


NEG_KEY = -0x7149F2CB
HALF_BITS = 16
I16_MIN, I16_MAX = -2 ** (HALF_BITS - 1), 2 ** (HALF_BITS - 1) - 1
COUNT_ACCS = 8


def _sortable(score):
    bits = lax.bitcast_convert_type(score, jnp.int32)
    return jnp.where(bits < 0, bits ^ jnp.int32(0x7FFFFFFF), bits)


def _sum_sublane_groups(x):
    parts = [x[r * SUBLANE:(r + 1) * SUBLANE] for r in range(x.shape[0] // SUBLANE)]
    while len(parts) > 1:
        parts = [parts[k] + parts[k + 1] for k in range(0, len(parts) - 1, 2)] + (
            [parts[-1]] if len(parts) % 2 else [])
    return parts[0]


def _tree_sum(parts):
    parts = list(parts)
    while len(parts) > 1:
        parts = [parts[k] + parts[k + 1] for k in range(0, len(parts) - 1, 2)] + (
            [parts[-1]] if len(parts) % 2 else [])
    return parts[0]


def _count_ge16(ref, nkb, cand):
    tk, tq = ref.shape[1], ref.shape[2]
    cand16 = jnp.broadcast_to(cand, (PACKED_ROWS, tq)).astype(jnp.int16)
    one, zero = jnp.ones((PACKED_ROWS, tq), jnp.int16), jnp.zeros((PACKED_ROWS, tq), jnp.int16)
    accs = [zero] * COUNT_ACCS
    n = 0
    for j in range(nkb):
        for r in range(tk // PACKED_ROWS):
            hit = jnp.where(ref[j, r * PACKED_ROWS:(r + 1) * PACKED_ROWS, :] >= cand16, one, zero)
            accs[n % COUNT_ACCS] = accs[n % COUNT_ACCS] + hit
            n += 1
    return jnp.sum(_tree_sum(accs).astype(F32), axis=0, keepdims=True)


def _search16(ref, nkb, n_sel, cnt_floor):
    tq = ref.shape[2]

    def bit_step(t, carry):
        thr_u, cnt_thr = carry
        bit = lax.shift_left(jnp.int32(1), jnp.int32(HALF_BITS - 1) - t)
        cand_u = thr_u | bit
        cnt = _count_ge16(ref, nkb, cand_u + I16_MIN)
        keep = cnt >= n_sel
        return jnp.where(keep, cand_u, thr_u), jnp.where(keep, cnt, cnt_thr)

    thr_u, cnt_thr = lax.fori_loop(0, HALF_BITS, bit_step, (jnp.zeros((1, tq), jnp.int32), cnt_floor))
    return thr_u + I16_MIN, cnt_thr


def _dsa_kernel(dqt_ref, iqt_ref, iwt_ref, dk2_ref, ik2_ref, dvt_ref, o_ref,
                key_ref, hi_ref, lo_ref, bias_ref, iq8_ref, q5_ref, jmax_ref, *, n_sel, idx_bits):
    tq, tk = TQ_DSA, TK_DSA
    i = pl.program_id(1)
    kloc = lax.broadcasted_iota(jnp.int32, (tk, tq), 0)
    qpos = i * tq + lax.broadcasted_iota(jnp.int32, (tk, tq), 1)
    klimit = (qpos // CHUNK + 1) * CHUNK
    for h in range(IDX_HEADS):
        iq8_ref[:, h * tq:(h + 1) * tq] = _pair_rows(iqt_ref[0, 0, h * HEAD_DIM:(h + 1) * HEAD_DIM, :], h)
    for h in range(HEADS_DSA):
        q5_ref[:, h * tq:(h + 1) * tq] = _pair_rows(dqt_ref[0, 0, h * HEAD_DIM:(h + 1) * HEAD_DIM, :], h)
    iwt = iwt_ref[0, 0]

    def select(nkb):
        for j in range(nkb):
            a = _dot(ik2_ref[0, j * tk:(j + 1) * tk, :], iq8_ref[...])
            sc = jnp.zeros((tk, tq), F32)
            for h in range(IDX_HEADS):
                sc = sc + iwt[h:h + 1, :] * jnp.maximum(a[:, h * tq:(h + 1) * tq], 0.0)
            sc = sc + 0.0
            sc = jnp.where(kloc + j * tk < klimit, sc, NEG)
            key = _sortable(sc)
            key_ref[j] = key
            hi_ref[j] = lax.shift_right_arithmetic(key, jnp.int32(HALF_BITS)).astype(jnp.int16)
            lo_ref[j] = ((key & jnp.int32(2 ** HALF_BITS - 1)) + I16_MIN).astype(jnp.int16)

        thr_hi, cnt_hi = _search16(hi_ref, nkb, n_sel, jnp.full((1, tq), float(nkb * tk), F32))
        thr_hi16 = jnp.broadcast_to(thr_hi, (PACKED_ROWS, tq)).astype(jnp.int16)
        for j in range(nkb):
            for r in range(tk // PACKED_ROWS):
                rows = slice(r * PACKED_ROWS, (r + 1) * PACKED_ROWS)
                hi = hi_ref[j, rows, :]
                pinned = jnp.where(hi > thr_hi16, jnp.int16(I16_MAX), jnp.int16(I16_MIN))
                lo_ref[j, rows, :] = jnp.where(hi == thr_hi16, lo_ref[j, rows, :], pinned)
        thr_lo, cnt_ge = _search16(lo_ref, nkb, n_sel, cnt_hi)
        thr = thr_hi * 2 ** HALF_BITS + (thr_lo - I16_MIN)

        def count(pred):
            parts = [_sum_sublane_groups(jnp.where(pred(key_ref[j]), 1.0, 0.0)) for j in range(nkb)]
            return jnp.sum(_tree_sum(parts), axis=0, keepdims=True)

        jmax_ref[...] = jnp.full((1, tq), 2 ** idx_bits - 1, jnp.int32)
        tied = (cnt_ge > n_sel) & (thr > NEG_KEY)

        @pl.when(jnp.max(jnp.where(tied, 1.0, 0.0)) > 0.0)
        def _():
            need = n_sel - count(lambda k: k > thr)
            for j in range(nkb):
                lo_ref[j] = jnp.where(key_ref[j] == thr, kloc + j * tk, I16_MAX).astype(jnp.int16)
            stored = float(nkb * tk)

            def idx_bit(t, jmax):
                bit = lax.shift_left(jnp.int32(1), jnp.int32(idx_bits - 1) - t)
                cand = jmax | bit
                cnt = stored - _count_ge16(lo_ref, nkb, cand)
                return jnp.where(cnt < need, cand, jmax)

            jmax_ref[...] = lax.fori_loop(0, idx_bits, idx_bit, jnp.zeros((1, tq), jnp.int32))

        jmax = jmax_ref[...]
        for j in range(nkb):
            k = key_ref[j]
            pos = kloc + j * tk
            sel = ((k > thr) | ((k == thr) & (pos <= jmax))) & (pos < klimit)
            bias_ref[j] = jnp.where(sel, 0.0, NEG)

    def attend(nkb):
        def qk(j):
            return _dot(dk2_ref[0, j * tk:(j + 1) * tk, :], q5_ref[...])

        m = jnp.full((1, HEADS_DSA * tq), NEG, F32)
        l = jnp.zeros((1, HEADS_DSA * tq), F32)
        acc = jnp.zeros((HEAD_DIM, HEADS_DSA * tq), F32)
        ahead = qk(0)
        for j in range(nkb):
            s = ahead + jnp.concatenate([bias_ref[j]] * HEADS_DSA, axis=1)
            if j + 1 < nkb:
                ahead = qk(j + 1)
            m_new = jnp.maximum(m, jnp.max(s, axis=0, keepdims=True))
            alpha = jnp.exp(m - m_new)
            p = jnp.exp(s - m_new)
            m = m_new
            l = alpha * l + jnp.sum(p, axis=0, keepdims=True)
            acc = alpha * acc + _dot(dvt_ref[0, j], p.astype(BF16))
        out = acc / l
        o_ref[0] = jnp.concatenate([out[:, h * tq:(h + 1) * tq].T for h in range(HEADS_DSA)],
                                   axis=1).astype(o_ref.dtype)

    nkb = (i * tq + tq - 1) // tk + 1
    nkb_even = lax.shift_left(lax.shift_right_logical(nkb + 1, 1), 1)
    def select_and_attend(nkb_static):
        if (nkb_static - 2) * tk < n_sel:
            few = (i * tq + tq - 1) // CHUNK * CHUNK + CHUNK <= n_sel

            def admit_all():
                for j in range(nkb_static):
                    bias_ref[j] = jnp.where(kloc + j * tk < klimit, 0.0, NEG)

            lax.cond(few, admit_all, lambda: select(nkb_static))
        else:
            select(nkb_static)
        attend(nkb_static)

    _dispatch(nkb_even, list(range(2, dk2_ref.shape[1] // tk + 1, 2)), select_and_attend)


def _dsa(dqt, iqt, iwt, dk2, ik2, dvt):
    b, s, _ = dk2.shape
    tq, tk = TQ_DSA, TK_DSA
    n_sel = min(DSA_TOPK_MAX, s // 4)
    assert n_sel <= tk and s % (2 * tk) == 0 and tk % tq == 0 and s < I16_MAX
    idx_bits = max(1, (s - 1).bit_length())
    return pl.pallas_call(
        functools.partial(_dsa_kernel, n_sel=n_sel, idx_bits=idx_bits),
        grid=(b, s // tq),
        in_specs=[pl.BlockSpec((1, 1, W_DSA, tq), lambda bi, qi: (bi, qi, 0, 0)),
                  pl.BlockSpec((1, 1, W_IDX, tq), lambda bi, qi: (bi, qi, 0, 0)),
                  pl.BlockSpec((1, 1, IDX_HEADS, tq), lambda bi, qi: (bi, qi, 0, 0)),
                  pl.BlockSpec((1, s, LANE), lambda bi, qi: (bi, 0, 0)),
                  pl.BlockSpec((1, s, LANE), lambda bi, qi: (bi, 0, 0)),
                  pl.BlockSpec((1, s // tk, HEAD_DIM, tk), lambda bi, qi: (bi, 0, 0, 0))],
        out_specs=pl.BlockSpec((1, tq, W_DSA), lambda bi, qi: (bi, qi, 0)),
        out_shape=jax.ShapeDtypeStruct((b, s, W_DSA), BF16),
        scratch_shapes=[pltpu.VMEM((s // tk, tk, tq), jnp.int32),
                        pltpu.VMEM((s // tk, tk, tq), jnp.int16),
                        pltpu.VMEM((s // tk, tk, tq), jnp.int16),
                        pltpu.VMEM((s // tk, tk, tq), F32),
                        pltpu.VMEM((LANE, IDX_HEADS * tq), BF16),
                        pltpu.VMEM((LANE, HEADS_DSA * tq), BF16),
                        pltpu.VMEM((1, tq), jnp.int32)],
        compiler_params=pltpu.CompilerParams(
            dimension_semantics=("arbitrary", "arbitrary"), vmem_limit_bytes=VMEM_LIMIT),
        name="dsa",
    )(dqt, iqt, iwt, dk2, ik2, dvt)


def _layer_norm(y, g, b):
    mu = jnp.mean(y, axis=1, keepdims=True)
    d = y - mu
    var = jnp.mean(d * d, axis=1, keepdims=True)
    return d * lax.rsqrt(var + LN_EPS) * g + b


def _merge_kernel(x_ref, of_ref, os_ref, od_ref, wg_ref, wuf_ref, wus_ref, wud_ref, wo_ref,
                  g_ref, b_ref, o_ref):
    half = x_ref.shape[0] // MLP_SPLIT
    for r in range(MLP_SPLIT):
        rows = slice(r * half, (r + 1) * half)
        x = x_ref[rows, :]
        xb = x.astype(BF16)
        merged = None
        for n, (br_ref, wu_ref) in enumerate(((of_ref, wuf_ref), (os_ref, wus_ref), (od_ref, wud_ref))):
            gate = jax.nn.sigmoid(_dot(xb, wg_ref[:, n * D_MODEL:(n + 1) * D_MODEL]))
            term = gate * _dot(br_ref[rows, :], wu_ref[...])
            merged = term if merged is None else merged + term
        y = DEEPNORM_ALPHA * x + _dot(merged.astype(BF16), wo_ref[...])
        o_ref[rows, :] = _layer_norm(y, g_ref[...], b_ref[...])


def _const_spec(shape):
    return pl.BlockSpec(shape, lambda i: (0,) * len(shape), pipeline_mode=pl.Buffered(1))


def _merge(x2, o_fox, o_sb, o_dsa, wg, wuf, wus, wud, wo, g, bvec):
    t = x2.shape[0]
    tm = min(TM_MLP, t)

    def rows(width):
        return pl.BlockSpec((tm, width), lambda i: (i, 0))

    return pl.pallas_call(
        _merge_kernel,
        grid=(t // tm,),
        in_specs=[rows(D_MODEL), rows(W_FOX), rows(W_SB), rows(W_DSA),
                  _const_spec(wg.shape), _const_spec(wuf.shape), _const_spec(wus.shape),
                  _const_spec(wud.shape), _const_spec(wo.shape),
                  _const_spec(g.shape), _const_spec(bvec.shape)],
        out_specs=rows(D_MODEL),
        out_shape=jax.ShapeDtypeStruct((t, D_MODEL), F32),
        compiler_params=pltpu.CompilerParams(
            dimension_semantics=("arbitrary",), vmem_limit_bytes=VMEM_LIMIT),
        name="merge",
    )(x2, o_fox, o_sb, o_dsa, wg, wuf, wus, wud, wo, g, bvec)


FF_CHUNK = 1024


def _ffn_kernel(x_ref, p_ref, w1_ref, w2_ref, wpg_ref, wp_ref, g_ref, b_ref, o_ref):
    half = x_ref.shape[0] // MLP_SPLIT
    for r in range(MLP_SPLIT):
        rows = slice(r * half, (r + 1) * half)
        x = x_ref[rows, :]
        xb = x.astype(BF16)
        acc = None
        for c in range(D_FF // FF_CHUNK):
            h = jnp.maximum(_dot(xb, w1_ref[:, c * FF_CHUNK:(c + 1) * FF_CHUNK]), 0.0)
            term = _dot((h * h).astype(BF16), w2_ref[c * FF_CHUNK:(c + 1) * FF_CHUNK, :])
            acc = term if acc is None else acc + term
        ple = jax.nn.sigmoid(_dot(xb, wpg_ref[...])) * _dot(p_ref[0, rows, :].astype(BF16), wp_ref[...])
        y = DEEPNORM_ALPHA * x + acc + ple
        o_ref[rows, :] = _layer_norm(y, g_ref[...], b_ref[...])


def _ffn(x2, p3, layer, w1, w2, wpg, wp, g, bvec):
    t = x2.shape[0]
    tm = min(TM_MLP, t)

    def rows(width):
        return pl.BlockSpec((tm, width), lambda i: (i, 0))

    return pl.pallas_call(
        _ffn_kernel,
        grid=(t // tm,),
        in_specs=[rows(D_MODEL), pl.BlockSpec((1, tm, PLE_DIM), lambda i: (layer, i, 0)),
                  _const_spec(w1.shape), _const_spec(w2.shape), _const_spec(wpg.shape),
                  _const_spec(wp.shape), _const_spec(g.shape), _const_spec(bvec.shape)],
        out_specs=rows(D_MODEL),
        out_shape=jax.ShapeDtypeStruct((t, D_MODEL), F32),
        compiler_params=pltpu.CompilerParams(
            dimension_semantics=("arbitrary",), vmem_limit_bytes=VMEM_LIMIT),
        name="ffn",
    )(x2, p3, w1, w2, wpg, wp, g, bvec)


def _pack_w_in(w):
    d = w.shape[0]
    offs = {}
    o = 0
    for name, width in (("fq", W_FOX), ("fk", W_FOX), ("fv", W_FOX), ("ff", HEADS_FOX),
                        ("sq", W_SB), ("sk", W_SB), ("sv", W_SB),
                        ("dq", W_DSA), ("dk", HEAD_DIM), ("dv", HEAD_DIM),
                        ("iq", W_IDX), ("ik", HEAD_DIM), ("iw", IDX_HEADS),
                        ("g", N_BRANCH * D_MODEL)):
        offs[name] = w[:, o:o + width]
        o += width

    def z(n):
        return jnp.zeros((d, n), w.dtype)

    packed = jnp.concatenate(
        [offs["ff"], z(IW_LANE - HEADS_FOX), offs["iw"], z(LANE - IW_LANE - IDX_HEADS),
         offs["dk"], offs["dk"], offs["ik"], offs["ik"], offs["fk"], offs["sk"], z(W_SB_PAD - W_SB)], axis=1)
    assert packed.shape[1] == C_PACK
    packed_t = jnp.concatenate([offs["fq"], offs["sq"], offs["dq"], offs["iq"],
                                offs["fv"], offs["sv"], offs["dv"]], axis=1).T
    assert packed_t.shape[0] == R_PACK
    return packed.astype(BF16), packed_t.astype(BF16), offs["g"].astype(BF16)


def _rope_tables(s):
    inv = ROPE_THETA ** (-jnp.arange(HALF, dtype=F32) / HALF)
    ang = jnp.arange(s, dtype=F32)[:, None] * inv[None, :]
    cos, sin = jnp.cos(ang), jnp.sin(ang)
    row_major = jnp.concatenate([cos, cos, cos, cos, -sin, sin, -sin, sin], axis=1)
    transposed = jnp.concatenate([cos, cos, -sin, sin], axis=1).T
    return row_major, transposed


def _placement():
    rows_q, cols_q, rows_k, cols_k = [], [], [], []
    for h in range(HEADS_FOX):
        for t in range(N_CTERM):
            rows_q.append(t * LANE + h), cols_q.append(CSTRIDE * h + t)
            rows_k.append(t * LANE + h), cols_k.append(LANE + CSTRIDE * h + N_CTERM + t)
    place = jnp.zeros((N_CTERM * LANE, 2 * LANE), F32)
    place = place.at[jnp.array(rows_q), jnp.array(cols_q)].set(1.0)
    place = place.at[jnp.array(rows_k), jnp.array(cols_k)].set(-1.0)
    lane = jnp.arange(2 * LANE)
    head_lane = (lane % LANE) % CSTRIDE
    in_heads = (lane % LANE) < CSTRIDE * HEADS_FOX
    ones_q = (lane < LANE) & in_heads & (head_lane >= N_CTERM) & (head_lane < 2 * N_CTERM)
    ones_k = (lane >= LANE) & in_heads & (head_lane < N_CTERM)
    ones_row = jnp.where(ones_q | ones_k, 1.0, 0.0).astype(F32)[None, :]
    return place.astype(BF16), ones_row


def kernel(x, p, w_in, b_forget, w_up_fox, w_up_sb, w_up_dsa, w_out, ln1_g, ln1_b,
           w_ff_in, w_ff_out, w_ple, w_ple_gate, ln2_g, ln2_b):
    b, s, d = x.shape
    depth = w_in.shape[0]
    rope_tab, rope_tab_t = _rope_tables(s)
    place, ones_row = _placement()
    w_in_rows = w_in.reshape(depth * d, w_in.shape[2])
    for i in range(depth):
        w_pack, wt_pack, w_gate = _pack_w_in(w_in_rows[i * d:(i + 1) * d])
        bias_row = jnp.zeros((1, LANE), F32).at[0, :HEADS_FOX].set(b_forget[i].astype(F32))
        (fqt, fqct, fk, fkc, fvt, sqt, sk, svt, dqt, iqt, dk2, ik2, dvt, iwt) = _proj(
            x, w_pack, wt_pack, bias_row, rope_tab, rope_tab_t, place, ones_row)
        o_fox = _fox(fqt, fqct, fk, fkc, fvt)
        o_sb = _sb(sqt, sk, svt)
        o_dsa = _dsa(dqt, iqt, iwt, dk2, ik2, dvt)
        x1 = _merge(x.reshape(b * s, d), o_fox.reshape(b * s, W_FOX), o_sb.reshape(b * s, W_SB),
                    o_dsa.reshape(b * s, W_DSA), w_gate,
                    w_up_fox[i].astype(BF16), w_up_sb[i].astype(BF16), w_up_dsa[i].astype(BF16),
                    w_out[i].astype(BF16), ln1_g[i][None, :], ln1_b[i][None, :])
        x2 = _ffn(x1, p.reshape(depth, b * s, PLE_DIM), i, w_ff_in[i].astype(BF16), w_ff_out[i].astype(BF16),
                  w_ple_gate[i].astype(BF16), w_ple[i].astype(BF16), ln2_g[i][None, :], ln2_b[i][None, :])
        x = x2.reshape(b, s, d)
    return x
```
